```python
import math
import jax
import jax.numpy as jnp
from jax import lax
import numpy as np

D_MODEL = 2048
BATCH = 4
SEQ = 2048
DEPTH = 4

N_MIXERS = 4
N_HEADS = 16
HEAD_DIM = 128
REL_BUCKETS = 32
REL_MAX_DIST = 128
BAND_BLOCK = 128
MOBA_BLOCK = 256
MOBA_TOPK = 3
MOBA_Q_CHUNK = 16
SWA_WINDOW = 128
SWA_KV_HEADS = 2
NSA_KV_HEADS = 4
NSA_CMP_LEN = 32
NSA_CMP_STRIDE = 16
NSA_CMP_HIDDEN = 256
NSA_SEL_BLOCK = 64
NSA_TOPN = 16
NSA_WINDOW = 512
NSA_Q_CHUNK = 32
MLA_Q_LORA = 512
MLA_KV_LORA = 512
MLA_NOPE_DIM = 128
MLA_ROPE_DIM = 64
MLA_V_DIM = 128
MLA_Q_BLOCK = 128
ROPE_BASE = 10000.0
D_FF = -(-8 * D_MODEL // 768) * 256
ALPHA = (2 * DEPTH) ** 0.25
BETA = (8 * DEPTH) ** -0.25
LN_EPS = 1e-5
RMS_EPS = 1e-6

kernel_name = 'hybrid_moba_swa_nsa_mla_deepnorm'


def layer_norm(x, g, b):
    xf = x.astype(jnp.float32)
    mu = jnp.mean(xf, axis=-1, keepdims=True)
    var = jnp.mean(jnp.square(xf - mu), axis=-1, keepdims=True)
    return ((xf - mu) * lax.rsqrt(var + LN_EPS) * g.astype(jnp.float32) + b.astype(jnp.float32)).astype(x.dtype)


def rms_norm(x, g):
    xf = x.astype(jnp.float32)
    y = xf * lax.rsqrt(jnp.mean(jnp.square(xf), axis=-1, keepdims=True) + RMS_EPS)
    return (y * g.astype(jnp.float32)).astype(x.dtype)


def t5_bucket(dist):
    n = jnp.maximum(dist, 0)
    max_exact = REL_BUCKETS // 2
    nf = jnp.maximum(n, max_exact).astype(jnp.float32)
    large = max_exact + (jnp.log(nf / max_exact) / math.log(REL_MAX_DIST / max_exact)
                         * (REL_BUCKETS - max_exact)).astype(jnp.int32)
    return jnp.where(n < max_exact, n, jnp.minimum(large, REL_BUCKETS - 1))


def masked_softmax(logits, mask):
    logits = jnp.where(mask, logits.astype(jnp.float32), -jnp.inf)
    m = jnp.max(logits, axis=-1, keepdims=True)
    m = jnp.where(jnp.isfinite(m), m, 0.0)
    p = jnp.exp(logits - m)
    return p / jnp.maximum(jnp.sum(p, axis=-1, keepdims=True), 1e-30)


def rope_tables(s):
    inv = ROPE_BASE ** (-jnp.arange(0, MLA_ROPE_DIM, 2, dtype=jnp.float32) / MLA_ROPE_DIM)
    ang = jnp.arange(s, dtype=jnp.float32)[:, None] * inv[None, :]
    return jnp.cos(ang), jnp.sin(ang)


def apply_rope(t, cos, sin):
    t1, t2 = jnp.split(t, 2, axis=-1)
    cos = cos.astype(t.dtype)
    sin = sin.astype(t.dtype)
    return jnp.concatenate([t1 * cos - t2 * sin, t1 * sin + t2 * cos], axis=-1)


def band_blocks(t, n_prev):
    b, hk, s, dh = t.shape
    nb = s // BAND_BLOCK
    tb = t.reshape(b, hk, nb, BAND_BLOCK, dh)
    tp = jnp.pad(tb, ((0, 0), (0, 0), (n_prev, 0), (0, 0), (0, 0)))
    return jnp.concatenate([tp[:, :, i:i + nb] for i in range(n_prev + 1)], axis=3)


def banded_gqa(q, k, v, window, bias_hg, sinks=None):
    b, hk, g, s, dh = q.shape
    nb = s // BAND_BLOCK
    n_prev = -(-(window - 1) // BAND_BLOCK)
    span = (n_prev + 1) * BAND_BLOCK
    kw = band_blocks(k, n_prev)
    vw = band_blocks(v, n_prev)
    qb = q.reshape(b, hk, g, nb, BAND_BLOCK, dh)
    logits = jnp.einsum('bkgnid,bknjd->bkgnij', qb, kw).astype(jnp.float32)
    i = jnp.arange(BAND_BLOCK)[:, None]
    j = jnp.arange(span)[None, :]
    dist = i + n_prev * BAND_BLOCK - j
    logits = logits + bias_hg[:, :, t5_bucket(dist)][None, :, :, None].astype(jnp.float32)
    qpos = (jnp.arange(nb) * BAND_BLOCK)[:, None, None] + i[None]
    mask = (dist >= 0) & (dist < window) & (qpos - dist >= 0)
    if sinks is None:
        p = masked_softmax(logits, mask)
    else:
        sink_col = jnp.broadcast_to(sinks.astype(jnp.float32)[None, :, :, None, None, None],
                                    logits.shape[:-1] + (1,))
        mask_s = jnp.pad(mask, ((0, 0), (0, 0), (0, 1)), constant_values=True)
        p = masked_softmax(jnp.concatenate([logits, sink_col], axis=-1), mask_s)[..., :span]
    out = jnp.einsum('bkgnij,bknjd->bkgnid', p.astype(v.dtype), vw)
    return out.reshape(b, hk, g, s, dh)


def moba_attention(q, k, v, bias_h):
    b, h, sp, dh = q.shape
    nb = sp // MOBA_BLOCK
    kb = k.reshape(b, h, nb, MOBA_BLOCK, dh)
    vb = v.reshape(b, h, nb, MOBA_BLOCK, dh)
    k_mean = jnp.mean(kb.astype(jnp.float32), axis=3)
    gate = jnp.einsum('bhsd,bhnd->bhsn', q.astype(jnp.float32), k_mean)
    q_blk = jnp.arange(sp) // MOBA_BLOCK
    fully_past = jnp.arange(nb)[None, :] < q_blk[:, None]
    gate = jnp.where(fully_past, gate, -jnp.inf)
    tk = max(1, min(MOBA_TOPK, nb - 1))
    _, sel = lax.top_k(gate, tk)
    nc = sp // MOBA_Q_CHUNK
    q_c = q.reshape(b, h, nc, MOBA_Q_CHUNK, dh).transpose(2, 0, 1, 3, 4)
    sel_c = sel.reshape(b, h, nc, MOBA_Q_CHUNK, tk).transpose(2, 0, 1, 3, 4)
    bi = jnp.arange(b)[:, None, None, None]
    hi = jnp.arange(h)[None, :, None, None]
    offs = jnp.arange(MOBA_BLOCK)
    n_sel = tk * MOBA_BLOCK

    def one_chunk(args):
        qc, selc, c = args
        qpos = c * MOBA_Q_CHUNK + jnp.arange(MOBA_Q_CHUNK)
        own = (c * MOBA_Q_CHUNK) // MOBA_BLOCK
        k_own = lax.dynamic_index_in_dim(kb, own, axis=2, keepdims=False)
        v_own = lax.dynamic_index_in_dim(vb, own, axis=2, keepdims=False)
        k_sel = kb[bi, hi, selc].reshape(b, h, MOBA_Q_CHUNK, n_sel, dh)
        v_sel = vb[bi, hi, selc].reshape(b, h, MOBA_Q_CHUNK, n_sel, dh)
        kpos_sel = (selc[..., None] * MOBA_BLOCK + offs).reshape(b, h, MOBA_Q_CHUNK, n_sel)
        kpos_own = own * MOBA_BLOCK + offs
        slot_ok = jnp.arange(tk)[None, :] < (qpos // MOBA_BLOCK)[:, None]
        mask_sel = jnp.repeat(slot_ok, MOBA_BLOCK, axis=-1)
        mask_own = kpos_own[None, :] <= qpos[:, None]
        l_sel = (jnp.einsum('bhqd,bhqjd->bhqj', qc, k_sel).astype(jnp.float32)
                 + bias_h[hi, t5_bucket(qpos[:, None] - kpos_sel)].astype(jnp.float32))
        l_own = (jnp.einsum('bhqd,bhjd->bhqj', qc, k_own).astype(jnp.float32)
                 + bias_h[:, t5_bucket(qpos[:, None] - kpos_own[None, :])].astype(jnp.float32))
        p = masked_softmax(jnp.concatenate([l_sel, l_own], axis=-1),
                           jnp.concatenate([mask_sel, mask_own], axis=-1)).astype(v.dtype)
        return (jnp.einsum('bhqj,bhqjd->bhqd', p[..., :n_sel], v_sel)
                + jnp.einsum('bhqj,bhjd->bhqd', p[..., n_sel:], v_own))

    out = lax.map(one_chunk, (q_c, sel_c, jnp.arange(nc)))
    return out.transpose(1, 2, 0, 3, 4).reshape(b, h, sp, dh)


def moba_mixer(x, w_qkv, w_o, rel_bias):
    b, s, _ = x.shape
    q, k, v = jnp.split(x @ w_qkv, 3, axis=-1)
    heads = lambda t: t.reshape(b, s, N_HEADS, HEAD_DIM).transpose(0, 2, 1, 3)
    q, k, v = heads(q) * HEAD_DIM ** -0.5, heads(k), heads(v)
    sp = -(-s // MOBA_BLOCK) * MOBA_BLOCK
    pad = ((0, 0), (0, 0), (0, sp - s), (0, 0))
    o = moba_attention(jnp.pad(q, pad), jnp.pad(k, pad), jnp.pad(v, pad), rel_bias.T)[:, :, :s]
    return o.transpose(0, 2, 1, 3).reshape(b, s, N_HEADS * HEAD_DIM) @ w_o


def swa_mixer(x, w_qkv, sinks, w_o, rel_bias):
    b, s, _ = x.shape
    g = N_HEADS // SWA_KV_HEADS
    q, k, v = jnp.split(x @ w_qkv, [N_HEADS * HEAD_DIM, (N_HEADS + SWA_KV_HEADS) * HEAD_DIM], axis=-1)
    q = q.reshape(b, s, SWA_KV_HEADS, g, HEAD_DIM).transpose(0, 2, 3, 1, 4) * HEAD_DIM ** -0.5
    k = k.reshape(b, s, SWA_KV_HEADS, HEAD_DIM).transpose(0, 2, 1, 3)
    v = v.reshape(b, s, SWA_KV_HEADS, HEAD_DIM).transpose(0, 2, 1, 3)
    bias_hg = rel_bias.T.reshape(SWA_KV_HEADS, g, REL_BUCKETS)
    o = banded_gqa(q, k, v, SWA_WINDOW, bias_hg, sinks.reshape(SWA_KV_HEADS, g))
    return o.transpose(0, 3, 1, 2, 4).reshape(b, s, N_HEADS * HEAD_DIM) @ w_o


def nsa_selected_attention(q, k, v, sel, sel_ok, bias_hg):
    b, hk, g, s, dh = q.shape
    n = sel.shape[-1]
    kb = k.reshape(b, hk, s // NSA_SEL_BLOCK, NSA_SEL_BLOCK, dh)
    vb = v.reshape(b, hk, s // NSA_SEL_BLOCK, NSA_SEL_BLOCK, dh)
    nc = s // NSA_Q_CHUNK
    q_c = q.reshape(b, hk, g, nc, NSA_Q_CHUNK, dh).transpose(3, 0, 1, 2, 4, 5)
    sel_c = sel.reshape(b, hk, nc, NSA_Q_CHUNK, n).transpose(2, 0, 1, 3, 4)
    ok_c = sel_ok.reshape(b, hk, nc, NSA_Q_CHUNK, n).transpose(2, 0, 1, 3, 4)
    bi = jnp.arange(b)[:, None, None, None]
    ki = jnp.arange(hk)[None, :, None, None]
    gi = jnp.arange(g)[None, None, :, None, None]
    offs = jnp.arange(NSA_SEL_BLOCK)

    def one_chunk(args):
        qc, selc, okc, c = args
        qpos = c * NSA_Q_CHUNK + jnp.arange(NSA_Q_CHUNK)
        ks = kb[bi, ki, selc].reshape(b, hk, NSA_Q_CHUNK, n * NSA_SEL_BLOCK, dh)
        vs = vb[bi, ki, selc].reshape(b, hk, NSA_Q_CHUNK, n * NSA_SEL_BLOCK, dh)
        kpos = (selc[..., None] * NSA_SEL_BLOCK + offs).reshape(b, hk, NSA_Q_CHUNK, n * NSA_SEL_BLOCK)
        dist = qpos[:, None] - kpos
        mask = (dist >= 0) & jnp.repeat(okc, NSA_SEL_BLOCK, axis=-1)
        bias = bias_hg[ki[..., None], gi, t5_bucket(dist)[:, :, None]].astype(jnp.float32)
        logits = jnp.einsum('bkgqd,bkqjd->bkgqj', qc, ks).astype(jnp.float32) + bias
        p = masked_softmax(logits, mask[:, :, None])
        return jnp.einsum('bkgqj,bkqjd->bkgqd', p.astype(vs.dtype), vs)

    out = lax.map(one_chunk, (q_c, sel_c, ok_c, jnp.arange(nc)))
    return out.transpose(1, 2, 3, 0, 4, 5).reshape(b, hk, g, s, dh)


def nsa_mixer(x, w_in, cmp_pos, cmp_w1, cmp_w2, w_o, rel_bias):
    b, s, _ = x.shape
    hk, dh = NSA_KV_HEADS, HEAD_DIM
    g = N_HEADS // hk
    kv_w = hk * dh
    splits = [int(c) for c in np.cumsum([N_HEADS * dh] + [kv_w] * 6)]
    parts = jnp.split(x @ w_in, splits, axis=-1)
    q = parts[0].reshape(b, s, hk, g, dh).transpose(0, 2, 3, 1, 4) * dh ** -0.5
    heads = lambda t: t.reshape(b, s, hk, dh).transpose(0, 2, 1, 3)
    k_cmp, v_cmp, k_slc, v_slc, k_win, v_win = [heads(t) for t in parts[1:7]]
    gates = jax.nn.sigmoid(parts[7].astype(jnp.float32)).reshape(b, s, hk, g, 3).transpose(0, 2, 3, 1, 4)
    bias_hg = rel_bias.T.reshape(hk, g, REL_BUCKETS)
    pos = jnp.arange(s)

    n_cmp = (s - NSA_CMP_LEN) // NSA_CMP_STRIDE + 1
    idx = jnp.arange(n_cmp)[:, None] * NSA_CMP_STRIDE + jnp.arange(NSA_CMP_LEN)[None, :]

    def compress(t, pe, w1, w2):
        blocks = t[:, :, idx] + pe
        hid = jax.nn.gelu(blocks.reshape(b, hk, n_cmp, NSA_CMP_LEN * dh) @ w1)
        return hid @ w2

    kc = compress(k_cmp, cmp_pos[0], cmp_w1[0], cmp_w2[0])
    vc = compress(v_cmp, cmp_pos[1], cmp_w1[1], cmp_w2[1])
    cmp_end = jnp.arange(n_cmp) * NSA_CMP_STRIDE + NSA_CMP_LEN - 1
    dist = pos[:, None] - cmp_end[None, :]
    logits = (jnp.einsum('bkgsd,bkcd->bkgsc', q, kc).astype(jnp.float32)
              + bias_hg[:, :, t5_bucket(dist)][None].astype(jnp.float32))
    p_cmp = masked_softmax(logits, dist >= 0)
    o_cmp = jnp.einsum('bkgsc,bkcd->bkgsd', p_cmp.astype(vc.dtype), vc)

    n_sel = s // NSA_SEL_BLOCK
    ci = jnp.arange(n_cmp)[:, None] * NSA_CMP_STRIDE
    sj = jnp.arange(n_sel)[None, :] * NSA_SEL_BLOCK
    overlap = ((ci < sj + NSA_SEL_BLOCK) & (ci + NSA_CMP_LEN > sj)).astype(jnp.float32)
    importance = jnp.einsum('bkgsc,cj->bksj', p_cmp, overlap)
    q_blk = (pos // NSA_SEL_BLOCK)[:, None]
    jj = jnp.arange(n_sel)[None, :]
    forced = (jj == 0) | (jj == q_blk) | (jj == q_blk - 1)
    importance = jnp.where(jj > q_blk, -jnp.inf, jnp.where(forced, jnp.inf, importance))
    top_val, sel = lax.top_k(importance, min(NSA_TOPN, n_sel))
    o_slc = nsa_selected_attention(q, k_slc, v_slc, sel, top_val > -jnp.inf, bias_hg)

    o_win = banded_gqa(q, k_win, v_win, NSA_WINDOW, bias_hg)

    o = (gates[..., 0:1] * o_cmp + gates[..., 1:2] * o_slc + gates[..., 2:3] * o_win).astype(x.dtype)
    return o.transpose(0, 3, 1, 2, 4).reshape(b, s, N_HEADS * dh) @ w_o


def mla_mixer(x, w_down, q_norm, kv_norm, w_uq, w_ukv, w_o):
    b, s, _ = x.shape
    h = N_HEADS
    c_q, c_kv, k_rope = jnp.split(x @ w_down, [MLA_Q_LORA, MLA_Q_LORA + MLA_KV_LORA], axis=-1)
    q = (rms_norm(c_q, q_norm) @ w_uq).reshape(b, s, h, MLA_NOPE_DIM + MLA_ROPE_DIM).transpose(0, 2, 1, 3)
    kv = (rms_norm(c_kv, kv_norm) @ w_ukv).reshape(b, s, h, MLA_NOPE_DIM + MLA_V_DIM).transpose(0, 2, 1, 3)
    cos, sin = rope_tables(s)
    scale = (MLA_NOPE_DIM + MLA_ROPE_DIM) ** -0.5
    q_nope = q[..., :MLA_NOPE_DIM] * scale
    q_rope = apply_rope(q[..., MLA_NOPE_DIM:], cos, sin) * scale
    k_nope, v = kv[..., :MLA_NOPE_DIM], kv[..., MLA_NOPE_DIM:]
    k_rope = apply_rope(k_rope, cos, sin)
    nqb = s // MLA_Q_BLOCK
    qn_c = q_nope.reshape(b, h, nqb, MLA_Q_BLOCK, MLA_NOPE_DIM).transpose(2, 0, 1, 3, 4)
    qr_c = q_rope.reshape(b, h, nqb, MLA_Q_BLOCK, MLA_ROPE_DIM).transpose(2, 0, 1, 3, 4)
    kpos = jnp.arange(s)

    def one_block(args):
        qn, qr, c = args
        qpos = c * MLA_Q_BLOCK + jnp.arange(MLA_Q_BLOCK)
        logits = (jnp.einsum('bhqd,bhkd->bhqk', qn, k_nope)
                  + jnp.einsum('bhqd,bkd->bhqk', qr, k_rope))
        p = masked_softmax(logits, kpos[None, :] <= qpos[:, None])
        return jnp.einsum('bhqk,bhkd->bhqd', p.astype(v.dtype), v)

    o = lax.map(one_block, (qn_c, qr_c, jnp.arange(nqb)))
    return o.transpose(1, 0, 3, 2, 4).reshape(b, s, h * MLA_V_DIM) @ w_o


def swiglu(x, w_gate, w_up, w_down):
    return (jax.nn.silu(x @ w_gate) * (x @ w_up)) @ w_down


def setup_inputs(seed: int = 0) -> dict:
    key = jax.random.key(seed)
    ks = iter(jax.random.split(key, 32))
    n_of = [(DEPTH - kind + N_MIXERS - 1) // N_MIXERS for kind in range(N_MIXERS)]
    na, nb_, nc, nd = n_of

    def nrm(shape, fan_in, gain=1.0):
        return jax.random.normal(next(ks), shape, jnp.float32) * (gain * fan_in ** -0.5)

    def noise(shape, scale):
        return jax.random.normal(next(ks), shape, jnp.float32) * scale

    hd = N_HEADS * HEAD_DIM
    nsa_in = hd + 6 * NSA_KV_HEADS * HEAD_DIM + 3 * N_HEADS
    return {
        'x': jax.random.normal(next(ks), (BATCH, SEQ, D_MODEL), jnp.float32),
        'rel_bias': noise((REL_BUCKETS, N_HEADS), 0.5),
        'moba_w_qkv': nrm((na, D_MODEL, 3 * hd), D_MODEL),
        'moba_w_o': nrm((na, hd, D_MODEL), hd, BETA),
        'swa_w_qkv': nrm((nb_, D_MODEL, (N_HEADS + 2 * SWA_KV_HEADS) * HEAD_DIM), D_MODEL),
        'swa_sinks': noise((nb_, N_HEADS), 1.0),
        'swa_w_o': nrm((nb_, hd, D_MODEL), hd, BETA),
        'nsa_w_in': nrm((nc, D_MODEL, nsa_in), D_MODEL),
        'nsa_cmp_pos': noise((nc, 2, NSA_CMP_LEN, HEAD_DIM), 0.1),
        'nsa_cmp_w1': nrm((nc, 2, NSA_CMP_LEN * HEAD_DIM, NSA_CMP_HIDDEN), NSA_CMP_LEN * HEAD_DIM),
        'nsa_cmp_w2': nrm((nc, 2, NSA_CMP_HIDDEN, HEAD_DIM), NSA_CMP_HIDDEN),
        'nsa_w_o': nrm((nc, hd, D_MODEL), hd, BETA),
        'mla_w_down': nrm((nd, D_MODEL, MLA_Q_LORA + MLA_KV_LORA + MLA_ROPE_DIM), D_MODEL),
        'mla_q_norm': 1.0 + noise((nd, MLA_Q_LORA), 0.02),
        'mla_kv_norm': 1.0 + noise((nd, MLA_KV_LORA), 0.02),
        'mla_w_uq': nrm((nd, MLA_Q_LORA, N_HEADS * (MLA_NOPE_DIM + MLA_ROPE_DIM)), MLA_Q_LORA),
        'mla_w_ukv': nrm((nd, MLA_KV_LORA, N_HEADS * (MLA_NOPE_DIM + MLA_V_DIM)), MLA_KV_LORA),
        'mla_w_o': nrm((nd, N_HEADS * MLA_V_DIM, D_MODEL), N_HEADS * MLA_V_DIM, BETA),
        'ln1_g': 1.0 + noise((DEPTH, D_MODEL), 0.02),
        'ln1_b': noise((DEPTH, D_MODEL), 0.02),
        'ffn_w_gate': nrm((DEPTH, D_MODEL, D_FF), D_MODEL),
        'ffn_w_up': nrm((DEPTH, D_MODEL, D_FF), D_MODEL),
        'ffn_w_down': nrm((DEPTH, D_FF, D_MODEL), D_FF, BETA),
        'ln2_g': 1.0 + noise((DEPTH, D_MODEL), 0.02),
        'ln2_b': noise((DEPTH, D_MODEL), 0.02),
    }


def reference(x, rel_bias, moba_w_qkv, moba_w_o, swa_w_qkv, swa_sinks, swa_w_o,
              nsa_w_in, nsa_cmp_pos, nsa_cmp_w1, nsa_cmp_w2, nsa_w_o,
              mla_w_down, mla_q_norm, mla_kv_norm, mla_w_uq, mla_w_ukv, mla_w_o,
              ln1_g, ln1_b, ffn_w_gate, ffn_w_up, ffn_w_down, ln2_g, ln2_b):
    for i in range(DEPTH):
        kind, j = i % N_MIXERS, i // N_MIXERS
        if kind == 0:
            h = moba_mixer(x, moba_w_qkv[j], moba_w_o[j], rel_bias)
        elif kind == 1:
            h = swa_mixer(x, swa_w_qkv[j], swa_sinks[j], swa_w_o[j], rel_bias)
        elif kind == 2:
            h = nsa_mixer(x, nsa_w_in[j], nsa_cmp_pos[j], nsa_cmp_w1[j], nsa_cmp_w2[j], nsa_w_o[j], rel_bias)
        else:
            h = mla_mixer(x, mla_w_down[j], mla_q_norm[j], mla_kv_norm[j], mla_w_uq[j], mla_w_ukv[j], mla_w_o[j])
        x = layer_norm(ALPHA * x + h, ln1_g[i], ln1_b[i])
        x = layer_norm(ALPHA * x + swiglu(x, ffn_w_gate[i], ffn_w_up[i], ffn_w_down[i]), ln2_g[i], ln2_b[i])
    return x
```

```python
import functools
import math

import jax
import jax.numpy as jnp
from jax import lax
from jax.experimental import pallas as pl
from jax.experimental.pallas import tpu as pltpu

F32 = jnp.float32
BF16 = jnp.bfloat16

N_HEADS = 16
HEAD_DIM = 128
REL_BUCKETS = 32
REL_MAX_DIST = 128
MOBA_BLOCK = 256
MOBA_TOPK = 3
SWA_WINDOW = 128
SWA_KV_HEADS = 2
NSA_KV_HEADS = 4
NSA_CMP_LEN = 32
NSA_CMP_STRIDE = 16
NSA_CMP_HIDDEN = 256
NSA_SEL_BLOCK = 64
NSA_TOPN = 16
NSA_WINDOW = 512
MLA_Q_LORA = 512
MLA_KV_LORA = 512
MLA_NOPE_DIM = 128
MLA_ROPE_DIM = 64
MLA_V_DIM = 128
ROPE_BASE = 10000.0
DEPTH = 4
ALPHA = (2 * DEPTH) ** 0.25
LN_EPS = 1e-5
RMS_EPS = 1e-6

LANES = 128
TILE = 256
NEG = -1e30
VMEM_LIMIT = 56 * 1024 * 1024

_NT = (((1,), (1,)), ((), ()))


def _cparams(sem):
    return pltpu.CompilerParams(dimension_semantics=sem, vmem_limit_bytes=VMEM_LIMIT)


def _matmul_kernel(x_ref, w_ref, o_ref, *, scale_tiles, scale):
    acc = jnp.dot(x_ref[...], w_ref[...], preferred_element_type=F32)
    if scale_tiles:
        acc = acc * jnp.where(pl.program_id(1) < scale_tiles, scale, 1.0).astype(F32)
    o_ref[...] = acc.astype(o_ref.dtype)


def _matmul(x, w, out_dtype, *, tn, tm=1024, scale_cols=0, scale=1.0):
    m, k = x.shape
    n = w.shape[1]
    tm = min(tm, m)
    assert m % tm == 0 and n % tn == 0 and scale_cols % tn == 0
    kern = functools.partial(_matmul_kernel, scale_tiles=scale_cols // tn, scale=scale)
    return pl.pallas_call(
        kern,
        out_shape=jax.ShapeDtypeStruct((m, n), out_dtype),
        grid=(m // tm, n // tn),
        in_specs=[pl.BlockSpec((tm, k), lambda i, j: (i, 0)),
                  pl.BlockSpec((k, tn), lambda i, j: (0, j))],
        out_specs=pl.BlockSpec((tm, tn), lambda i, j: (i, j)),
        compiler_params=_cparams(("parallel", "arbitrary")),
        name="proj_matmul",
    )(x, w)


def _proj_ln_kernel(a_ref, w_ref, x_ref, g_ref, b_ref, o_ref, ob_ref, acc_ref):
    kk = pl.program_id(1)
    part = jnp.dot(a_ref[...], w_ref[...], preferred_element_type=F32)

    @pl.when(kk == 0)
    def _():
        acc_ref[...] = part

    @pl.when(kk > 0)
    def _():
        acc_ref[...] += part

    @pl.when(kk == pl.num_programs(1) - 1)
    def _():
        y = ALPHA * x_ref[...] + acc_ref[...]
        mu = jnp.mean(y, axis=-1, keepdims=True)
        yc = y - mu
        var = jnp.mean(yc * yc, axis=-1, keepdims=True)
        out = yc * lax.rsqrt(var + LN_EPS) * g_ref[...] + b_ref[...]
        o_ref[...] = out
        ob_ref[...] = out.astype(BF16)


def _proj_ln(a, w, x, g, b, *, tm=512, tk=512):
    m, k = a.shape
    d = w.shape[1]
    tm = min(tm, m)
    tk = min(tk, k)
    assert m % tm == 0 and k % tk == 0
    return pl.pallas_call(
        _proj_ln_kernel,
        out_shape=(jax.ShapeDtypeStruct((m, d), F32), jax.ShapeDtypeStruct((m, d), BF16)),
        grid=(m // tm, k // tk),
        in_specs=[pl.BlockSpec((tm, tk), lambda i, j: (i, j)),
                  pl.BlockSpec((tk, d), lambda i, j: (j, 0)),
                  pl.BlockSpec((tm, d), lambda i, j: (i, 0)),
                  pl.BlockSpec((1, d), lambda i, j: (0, 0)),
                  pl.BlockSpec((1, d), lambda i, j: (0, 0))],
        out_specs=(pl.BlockSpec((tm, d), lambda i, j: (i, 0)),
                   pl.BlockSpec((tm, d), lambda i, j: (i, 0))),
        scratch_shapes=[pltpu.VMEM((tm, d), F32)],
        compiler_params=_cparams(("parallel", "arbitrary")),
        name="proj_deepnorm",
    )(a, w, x, g.reshape(1, d), b.reshape(1, d))


def _ffn_up_kernel(x_ref, wg_ref, wu_ref, o_ref):
    x = x_ref[...]
    gate = jnp.dot(x, wg_ref[...], preferred_element_type=F32)
    up = jnp.dot(x, wu_ref[...], preferred_element_type=F32)
    o_ref[...] = (gate * jax.nn.sigmoid(gate) * up).astype(o_ref.dtype)


def _ffn_up(x, wg, wu, *, tm=1024, tn=512):
    m, k = x.shape
    f = wg.shape[1]
    tm = min(tm, m)
    tn = min(tn, f)
    assert m % tm == 0 and f % tn == 0
    return pl.pallas_call(
        _ffn_up_kernel,
        out_shape=jax.ShapeDtypeStruct((m, f), BF16),
        grid=(m // tm, f // tn),
        in_specs=[pl.BlockSpec((tm, k), lambda i, j: (i, 0)),
                  pl.BlockSpec((k, tn), lambda i, j: (0, j)),
                  pl.BlockSpec((k, tn), lambda i, j: (0, j))],
        out_specs=pl.BlockSpec((tm, tn), lambda i, j: (i, j)),
        compiler_params=_cparams(("parallel", "arbitrary")),
        name="ffn_gate_up",
    )(x, wg, wu)


def _t5_bucket(dist):
    n = jnp.maximum(dist, 0)
    max_exact = REL_BUCKETS // 2
    nf = jnp.maximum(n, max_exact).astype(F32)
    large = max_exact + (jnp.log(nf / max_exact) / math.log(REL_MAX_DIST / max_exact)
                         * (REL_BUCKETS - max_exact)).astype(jnp.int32)
    return jnp.where(n < max_exact, n, jnp.minimum(large, REL_BUCKETS - 1))


def _bias_tiles(rel_bias, window=None):
    r = jnp.arange(TILE)[:, None]
    c = jnp.arange(TILE)[None, :]
    bt = rel_bias.T.astype(F32)
    d0 = r - c
    d1 = TILE + r - c
    ok0 = d0 >= 0
    ok1 = jnp.ones_like(ok0)
    if window is not None:
        ok0 = ok0 & (d0 < window)
        ok1 = d1 < window
    t0 = jnp.where(ok0[None], bt[:, _t5_bucket(d0)], NEG)
    t1 = jnp.where(ok1[None], bt[:, _t5_bucket(d1)], NEG)
    far = bt[:, _t5_bucket(jnp.asarray(2 * TILE))]
    return t0, t1, far


def _flash_update(s, v, m_ref, l_ref, acc_ref):
    m_prev = m_ref[...]
    m_new = jnp.maximum(m_prev, jnp.max(s, axis=1, keepdims=True))
    alpha = jnp.exp(m_prev - m_new)
    p = jnp.exp(s - m_new)
    l_ref[...] = alpha * l_ref[...] + jnp.sum(p, axis=1, keepdims=True)
    acc_ref[...] = alpha * acc_ref[...] + jnp.dot(p.astype(BF16), v, preferred_element_type=F32)
    m_ref[...] = m_new


def _flash_reset(m_ref, l_ref, acc_ref):
    m_ref[...] = jnp.full(m_ref.shape, NEG, F32)
    l_ref[...] = jnp.zeros(l_ref.shape, F32)
    acc_ref[...] = jnp.zeros(acc_ref.shape, F32)


def _rows(ref, t):
    return ref[pl.ds(pl.multiple_of(t * TILE, TILE), TILE), :]


def _split3(x):
    hi = x.astype(BF16)
    r1 = x - hi.astype(F32)
    mid = r1.astype(BF16)
    lo = (r1 - mid.astype(F32)).astype(BF16)
    return hi, mid, lo


def _rank_before(val, lane, n):
    cnt = jnp.zeros(val.shape, jnp.int32)
    for m in range(n):
        vm = val[:, m:m + 1]
        beats = (vm > val) | ((vm == val) & (m < lane))
        cnt = cnt + beats.astype(jnp.int32)
    return cnt


def _moba_kernel(far_ref, q_ref, k_ref, v_ref, t0_ref, t1_ref, o_ref,
                 m_ref, l_ref, acc_ref, sel_ref, *, nblk):
    h = pl.program_id(1)
    i = pl.program_id(2)
    q = q_ref[...]

    kmean = jnp.concatenate(
        [jnp.mean(k_ref[n * TILE:(n + 1) * TILE, :].astype(F32), axis=0, keepdims=True)
         for n in range(nblk)]
        + [jnp.zeros((LANES - nblk, HEAD_DIM), F32)], axis=0)
    gate = sum(lax.dot_general(q, part, _NT, preferred_element_type=F32)
               for part in _split3(kmean))
    lane = lax.broadcasted_iota(jnp.int32, gate.shape, 1)
    past = lane < i
    gate = jnp.where(past, gate, -jnp.inf)
    cnt = _rank_before(gate, lane, nblk)
    sel_ref[...] = jnp.where(past & (cnt < MOBA_TOPK), 0.0, NEG).astype(F32)

    _flash_reset(m_ref, l_ref, acc_ref)
    s = lax.dot_general(q, _rows(k_ref, i), _NT, preferred_element_type=F32) + t0_ref[0]
    _flash_update(s, _rows(v_ref, i), m_ref, l_ref, acc_ref)

    for t in range(nblk - 1):
        @pl.when(t < i)
        def _(t=t):
            s = lax.dot_general(q, k_ref[t * TILE:(t + 1) * TILE, :], _NT, preferred_element_type=F32)
            bias = jnp.where(t == i - 1, t1_ref[0], far_ref[h])
            s = s + bias + sel_ref[:, t:t + 1]
            _flash_update(s, v_ref[t * TILE:(t + 1) * TILE, :], m_ref, l_ref, acc_ref)

    o_ref[...] = (acc_ref[...] / l_ref[...]).astype(o_ref.dtype)


def _moba_attention(qkv, rel_bias, b, s):
    assert MOBA_BLOCK == TILE and s % TILE == 0
    nq = s // TILE
    assert max(1, min(MOBA_TOPK, nq - 1)) == MOBA_TOPK
    hh = N_HEADS
    t0, t1, far = _bias_tiles(rel_bias)
    kern = functools.partial(_moba_kernel, nblk=nq)
    return pl.pallas_call(
        kern,
        out_shape=jax.ShapeDtypeStruct((b * s, hh * HEAD_DIM), BF16),
        grid=(b, hh, nq),
        in_specs=[pl.BlockSpec(memory_space=pltpu.SMEM),
                  pl.BlockSpec((TILE, HEAD_DIM), lambda bi, h, i: (bi * nq + i, h)),
                  pl.BlockSpec((s, HEAD_DIM), lambda bi, h, i: (bi, hh + h)),
                  pl.BlockSpec((s, HEAD_DIM), lambda bi, h, i: (bi, 2 * hh + h)),
                  pl.BlockSpec((1, TILE, TILE), lambda bi, h, i: (h, 0, 0)),
                  pl.BlockSpec((1, TILE, TILE), lambda bi, h, i: (h, 0, 0))],
        out_specs=pl.BlockSpec((TILE, HEAD_DIM), lambda bi, h, i: (bi * nq + i, h)),
        scratch_shapes=[pltpu.VMEM((TILE, 1), F32), pltpu.VMEM((TILE, 1), F32),
                        pltpu.VMEM((TILE, HEAD_DIM), F32), pltpu.VMEM((TILE, LANES), F32)],
        compiler_params=_cparams(("parallel", "parallel", "arbitrary")),
        name="moba_attention",
    )(far, qkv, qkv, qkv, t0, t1)


def _swa_kernel(sink_ref, q_ref, k_ref, v_ref, t0_ref, t1_ref, o_ref, m_ref, l_ref, acc_ref, *, group):
    hk = pl.program_id(0)
    i = pl.program_id(2)
    q = jnp.concatenate([q_ref[:, g * HEAD_DIM:(g + 1) * HEAD_DIM] for g in range(group)], axis=0)

    for g in range(group):
        m_ref[g * TILE:(g + 1) * TILE, :] = jnp.full((TILE, 1), sink_ref[hk * group + g], F32)
    l_ref[...] = jnp.ones(l_ref.shape, F32)
    acc_ref[...] = jnp.zeros(acc_ref.shape, F32)

    s = lax.dot_general(q, _rows(k_ref, i), _NT, preferred_element_type=F32)
    s = s + t0_ref[...].reshape(group * TILE, TILE)
    _flash_update(s, _rows(v_ref, i), m_ref, l_ref, acc_ref)

    @pl.when(i > 0)
    def _():
        s = lax.dot_general(q, _rows(k_ref, i - 1), _NT, preferred_element_type=F32)
        s = s + t1_ref[...].reshape(group * TILE, TILE)
        _flash_update(s, _rows(v_ref, i - 1), m_ref, l_ref, acc_ref)

    out = acc_ref[...] / l_ref[...]
    for g in range(group):
        o_ref[:, g * HEAD_DIM:(g + 1) * HEAD_DIM] = out[g * TILE:(g + 1) * TILE].astype(o_ref.dtype)


def _swa_attention(qkv, sinks, rel_bias, b, s):
    assert SWA_WINDOW <= TILE and s % TILE == 0
    nq = s // TILE
    hk_n = SWA_KV_HEADS
    group = N_HEADS // hk_n
    t0, t1, _ = _bias_tiles(rel_bias, window=SWA_WINDOW)
    kern = functools.partial(_swa_kernel, group=group)
    gw = group * HEAD_DIM
    return pl.pallas_call(
        kern,
        out_shape=jax.ShapeDtypeStruct((b * s, N_HEADS * HEAD_DIM), BF16),
        grid=(hk_n, b, nq),
        in_specs=[pl.BlockSpec(memory_space=pltpu.SMEM),
                  pl.BlockSpec((TILE, gw), lambda hk, bi, i: (bi * nq + i, hk)),
                  pl.BlockSpec((s, HEAD_DIM), lambda hk, bi, i: (bi, N_HEADS + hk)),
                  pl.BlockSpec((s, HEAD_DIM), lambda hk, bi, i: (bi, N_HEADS + hk_n + hk)),
                  pl.BlockSpec((group, TILE, TILE), lambda hk, bi, i: (hk, 0, 0)),
                  pl.BlockSpec((group, TILE, TILE), lambda hk, bi, i: (hk, 0, 0))],
        out_specs=pl.BlockSpec((TILE, gw), lambda hk, bi, i: (bi * nq + i, hk)),
        scratch_shapes=[pltpu.VMEM((group * TILE, 1), F32), pltpu.VMEM((group * TILE, 1), F32),
                        pltpu.VMEM((group * TILE, HEAD_DIM), F32)],
        compiler_params=_cparams(("parallel", "parallel", "arbitrary")),
        name="swa_attention",
    )(sinks.astype(F32), qkv, qkv, qkv, t0, t1)


def _gelu_tanh(x):
    return 0.5 * x * (1.0 + jnp.tanh(math.sqrt(2.0 / math.pi) * (x + 0.044715 * (x * x * x))))


def _nsa_compress_kernel(t_ref, pe_ref, w1_ref, w2_ref, o_ref, tf_ref, *, ncmp_pad):
    stride = NSA_CMP_STRIDE
    half = stride * HEAD_DIM
    pe = pe_ref[0]
    tf_ref[...] = t_ref[...].astype(F32)
    xa, xb = [], []
    for r in range(stride):
        x = tf_ref[pl.ds(r, ncmp_pad, stride=stride), :]
        xa.append((x + pe[r:r + 1, :]).astype(BF16))
        xb.append((x + pe[stride + r:stride + r + 1, :]).astype(BF16))
    a = jnp.dot(jnp.concatenate(xa, axis=1), w1_ref[0, :half, :], preferred_element_type=F32)
    bm = jnp.dot(jnp.concatenate(xb, axis=1), w1_ref[0, half:, :], preferred_element_type=F32)
    hid = _gelu_tanh(a + pltpu.roll(bm, ncmp_pad - 1, 0))
    o_ref[0, 0, 0] = jnp.dot(hid.astype(BF16), w2_ref[0], preferred_element_type=F32).astype(o_ref.dtype)


def _nsa_compress(proj, pe, w1, w2, b, s):
    assert NSA_CMP_LEN == 2 * NSA_CMP_STRIDE and s % NSA_CMP_STRIDE == 0
    ncp = s // NSA_CMP_STRIDE
    hk_n = NSA_KV_HEADS
    kern = functools.partial(_nsa_compress_kernel, ncmp_pad=ncp)
    return pl.pallas_call(
        kern,
        out_shape=jax.ShapeDtypeStruct((2, b, hk_n, ncp, HEAD_DIM), BF16),
        grid=(2, b, hk_n),
        in_specs=[pl.BlockSpec((s, HEAD_DIM), lambda kv, bi, hk: (bi, N_HEADS + kv * hk_n + hk)),
                  pl.BlockSpec((1, NSA_CMP_LEN, HEAD_DIM), lambda kv, bi, hk: (kv, 0, 0)),
                  pl.BlockSpec((1, NSA_CMP_LEN * HEAD_DIM, NSA_CMP_HIDDEN), lambda kv, bi, hk: (kv, 0, 0)),
                  pl.BlockSpec((1, NSA_CMP_HIDDEN, HEAD_DIM), lambda kv, bi, hk: (kv, 0, 0))],
        out_specs=pl.BlockSpec((1, 1, 1, ncp, HEAD_DIM), lambda kv, bi, hk: (kv, bi, hk, 0, 0)),
        scratch_shapes=[pltpu.VMEM((s, HEAD_DIM), F32)],
        compiler_params=_cparams(("parallel", "parallel", "arbitrary")),
        name="nsa_compress",
    )(proj, pe.astype(F32), w1, w2)


def _nsa_kernel(far_ref, q_ref, kc_ref, vc_ref, cb_ref, ksl_ref, vsl_ref, kw_ref, vw_ref, gate_ref,
                t0_ref, t1_ref, tw_ref, ov_ref, ex_ref, o_ref,
                m_ref, l_ref, acc_ref, sel_ref, oslc_ref, *, group, nq):
    hk = pl.program_id(1)
    i = pl.program_id(2)
    rows = group * TILE
    q = jnp.concatenate([q_ref[:, g * HEAD_DIM:(g + 1) * HEAD_DIM] for g in range(group)], axis=0)

    cb = cb_ref[...].reshape(rows, LANES)
    sc = lax.dot_general(q, kc_ref[0, 0, 0], _NT, preferred_element_type=F32) + cb
    valid = cb > 0.5 * NEG
    mc = jnp.max(sc, axis=1, keepdims=True)
    pc = jnp.where(valid, jnp.exp(sc - mc), 0.0)
    pc = pc / jnp.maximum(jnp.sum(pc, axis=1, keepdims=True), 1e-30)
    o_cmp = jnp.dot(pc.astype(BF16), vc_ref[0, 0, 0], preferred_element_type=F32)

    psum = pc[0:TILE]
    for g in range(1, group):
        psum = psum + pc[g * TILE:(g + 1) * TILE]
    imp = sum(jnp.dot(part, ov_ref[...], preferred_element_type=F32) for part in _split3(psum))
    lane = lax.broadcasted_iota(jnp.int32, imp.shape, 1)
    qblk = (i * TILE + lax.broadcasted_iota(jnp.int32, imp.shape, 0)) // NSA_SEL_BLOCK
    forced = (lane == 0) | (lane == qblk) | (lane == qblk - 1)
    imp = jnp.where(lane > qblk, -jnp.inf, jnp.where(forced, jnp.inf, imp))
    cnt = _rank_before(imp, lane, nq * (TILE // NSA_SEL_BLOCK))
    chosen = jnp.where((lane <= qblk) & (cnt < NSA_TOPN), 1.0, 0.0).astype(BF16)
    for t in range(nq):
        @pl.when(t <= i)
        def _(t=t):
            hit = jnp.dot(chosen, ex_ref[:, t * TILE:(t + 1) * TILE], preferred_element_type=F32)
            sel_ref[t] = (hit - 1.0) * (-NEG)

    far = jnp.concatenate([jnp.full((TILE, 1), far_ref[hk * group + g], F32) for g in range(group)], axis=0)
    t0 = t0_ref[...].reshape(rows, TILE)

    def masked(s, t):
        return (s.reshape(group, TILE, TILE) + sel_ref[t][None]).reshape(rows, TILE)

    _flash_reset(m_ref, l_ref, acc_ref)
    s = lax.dot_general(q, _rows(ksl_ref, i), _NT, preferred_element_type=F32) + t0
    _flash_update(masked(s, i), _rows(vsl_ref, i), m_ref, l_ref, acc_ref)

    @pl.when(i > 0)
    def _():
        s = lax.dot_general(q, _rows(ksl_ref, i - 1), _NT, preferred_element_type=F32)
        s = s + t1_ref[...].reshape(rows, TILE)
        _flash_update(masked(s, i - 1), _rows(vsl_ref, i - 1), m_ref, l_ref, acc_ref)

    def far_tile(t, carry):
        s = lax.dot_general(q, _rows(ksl_ref, t), _NT, preferred_element_type=F32) + far
        _flash_update(masked(s, t), _rows(vsl_ref, t), m_ref, l_ref, acc_ref)
        return carry

    lax.fori_loop(0, jnp.maximum(i - 1, 0), far_tile, 0)
    oslc_ref[...] = acc_ref[...] / l_ref[...]

    _flash_reset(m_ref, l_ref, acc_ref)
    s = lax.dot_general(q, _rows(kw_ref, i), _NT, preferred_element_type=F32) + t0
    _flash_update(s, _rows(vw_ref, i), m_ref, l_ref, acc_ref)

    @pl.when(i > 0)
    def _():
        s = lax.dot_general(q, _rows(kw_ref, i - 1), _NT, preferred_element_type=F32)
        s = s + t1_ref[...].reshape(rows, TILE)
        _flash_update(s, _rows(vw_ref, i - 1), m_ref, l_ref, acc_ref)

    @pl.when(i > 1)
    def _():
        s = lax.dot_general(q, _rows(kw_ref, i - 2), _NT, preferred_element_type=F32)
        s = s + tw_ref[...].reshape(rows, TILE)
        _flash_update(s, _rows(vw_ref, i - 2), m_ref, l_ref, acc_ref)

    o_win = acc_ref[...] / l_ref[...]
    o_slc = oslc_ref[...]
    gates = jax.nn.sigmoid(gate_ref[...])
    for g in range(group):
        sl = slice(g * TILE, (g + 1) * TILE)
        o = (gates[:, 3 * g:3 * g + 1] * o_cmp[sl] + gates[:, 3 * g + 1:3 * g + 2] * o_slc[sl]
             + gates[:, 3 * g + 2:3 * g + 3] * o_win[sl])
        o_ref[:, g * HEAD_DIM:(g + 1) * HEAD_DIM] = o.astype(o_ref.dtype)


def _nsa_attention(proj, gates, cmp_kv, rel_bias, b, s):
    assert s % TILE == 0 and NSA_WINDOW == 2 * TILE and TILE % NSA_SEL_BLOCK == 0
    nq = s // TILE
    hk_n = NSA_KV_HEADS
    group = N_HEADS // hk_n
    gw = group * HEAD_DIM
    ncp = s // NSA_CMP_STRIDE
    n_cmp = (s - NSA_CMP_LEN) // NSA_CMP_STRIDE + 1
    n_sel = s // NSA_SEL_BLOCK
    assert ncp == LANES and n_sel <= LANES and min(NSA_TOPN, n_sel) == NSA_TOPN
    bt = rel_bias.T.astype(F32)
    t0, t1, far = _bias_tiles(rel_bias)
    r = jnp.arange(TILE)[:, None]
    c = jnp.arange(TILE)[None, :]
    tw = jnp.where((c > r)[None], far[:, None, None], NEG)
    pos = jnp.arange(s)[:, None]
    cidx = jnp.arange(ncp)[None, :]
    cdist = pos - (cidx * NSA_CMP_STRIDE + NSA_CMP_LEN - 1)
    cmp_bias = jnp.where(((cdist >= 0) & (cidx < n_cmp))[None], bt[:, _t5_bucket(cdist)], NEG)
    ci = jnp.arange(ncp)[:, None] * NSA_CMP_STRIDE
    sj = jnp.arange(LANES)[None, :] * NSA_SEL_BLOCK
    overlap = ((ci < sj + NSA_SEL_BLOCK) & (ci + NSA_CMP_LEN > sj)
               & (jnp.arange(ncp)[:, None] < n_cmp) & (jnp.arange(LANES)[None, :] < n_sel)).astype(BF16)
    expand = (jnp.arange(LANES)[:, None] == (jnp.arange(s)[None, :] // NSA_SEL_BLOCK)).astype(BF16)

    kern = functools.partial(_nsa_kernel, group=group, nq=nq)
    kv_spec = lambda off: pl.BlockSpec((s, HEAD_DIM), lambda bi, hk, i: (bi, N_HEADS + off * hk_n + hk))
    cmp_spec = lambda kv: pl.BlockSpec((1, 1, 1, ncp, HEAD_DIM), lambda bi, hk, i: (kv, bi, hk, 0, 0))
    tile_spec = pl.BlockSpec((group, TILE, TILE), lambda bi, hk, i: (hk, 0, 0))
    return pl.pallas_call(
        kern,
        out_shape=jax.ShapeDtypeStruct((b * s, N_HEADS * HEAD_DIM), BF16),
        grid=(b, hk_n, nq),
        in_specs=[pl.BlockSpec(memory_space=pltpu.SMEM),
                  pl.BlockSpec((TILE, gw), lambda bi, hk, i: (bi * nq + i, hk)),
                  cmp_spec(0), cmp_spec(1),
                  pl.BlockSpec((group, TILE, ncp), lambda bi, hk, i: (hk, i, 0)),
                  kv_spec(2), kv_spec(3), kv_spec(4), kv_spec(5),
                  pl.BlockSpec((TILE, LANES), lambda bi, hk, i: (bi * nq + i, hk)),
                  tile_spec, tile_spec, tile_spec,
                  pl.BlockSpec((ncp, LANES), lambda bi, hk, i: (0, 0)),
                  pl.BlockSpec((LANES, s), lambda bi, hk, i: (0, 0))],
        out_specs=pl.BlockSpec((TILE, gw), lambda bi, hk, i: (bi * nq + i, hk)),
        scratch_shapes=[pltpu.VMEM((group * TILE, 1), F32), pltpu.VMEM((group * TILE, 1), F32),
                        pltpu.VMEM((group * TILE, HEAD_DIM), F32),
                        pltpu.VMEM((nq, TILE, TILE), F32),
                        pltpu.VMEM((group * TILE, HEAD_DIM), F32)],
        compiler_params=_cparams(("parallel", "parallel", "arbitrary")),
        name="nsa_attention",
    )(far, proj, cmp_kv, cmp_kv, cmp_bias, proj, proj, proj, proj, gates, t0, t1, tw, overlap, expand)


def _rms(x, g):
    return x * lax.rsqrt(jnp.mean(x * x, axis=-1, keepdims=True) + RMS_EPS) * g


def _mla_prep_kernel(c_ref, qg_ref, kg_ref, tab_ref, cq_ref, ckv_ref, kr_ref):
    c = c_ref[...]
    cq_ref[...] = _rms(c[:, :MLA_Q_LORA], qg_ref[...]).astype(BF16)
    ckv_ref[...] = _rms(c[:, MLA_Q_LORA:MLA_Q_LORA + MLA_KV_LORA], kg_ref[...]).astype(BF16)
    y = c[:, MLA_Q_LORA + MLA_KV_LORA:] * tab_ref[...]
    y = y + pltpu.roll(y, MLA_ROPE_DIM, 1)
    lane = lax.broadcasted_iota(jnp.int32, y.shape, 1)
    kr_ref[0] = jnp.where(lane < MLA_ROPE_DIM, y, 0.0).astype(BF16)
    kr_ref[1] = jnp.where(lane >= MLA_ROPE_DIM, y, 0.0).astype(BF16)


def _mla_prep(c, q_norm, kv_norm, ktab, s, *, tm=512):
    m, w = c.shape
    tm = min(tm, s)
    assert 2 * MLA_ROPE_DIM == LANES and w == MLA_Q_LORA + MLA_KV_LORA + LANES and s % tm == 0
    ns = s // tm
    return pl.pallas_call(
        _mla_prep_kernel,
        out_shape=(jax.ShapeDtypeStruct((m, MLA_Q_LORA), BF16),
                   jax.ShapeDtypeStruct((m, MLA_KV_LORA), BF16),
                   jax.ShapeDtypeStruct((2, m, LANES), BF16)),
        grid=(m // tm,),
        in_specs=[pl.BlockSpec((tm, w), lambda i: (i, 0)),
                  pl.BlockSpec((1, MLA_Q_LORA), lambda i: (0, 0)),
                  pl.BlockSpec((1, MLA_KV_LORA), lambda i: (0, 0)),
                  pl.BlockSpec((tm, LANES), lambda i: (i % ns, 0))],
        out_specs=(pl.BlockSpec((tm, MLA_Q_LORA), lambda i: (i, 0)),
                   pl.BlockSpec((tm, MLA_KV_LORA), lambda i: (i, 0)),
                   pl.BlockSpec((2, tm, LANES), lambda i: (0, i, 0))),
        compiler_params=_cparams(("parallel",)),
        name="mla_prep",
    )(c, q_norm.reshape(1, -1).astype(F32), kv_norm.reshape(1, -1).astype(F32), ktab)


def _mla_kernel(qn_ref, qr_ref, qs_ref, ct_ref, st_ref, kn_ref, kr_ref, v_ref, o_ref,
                m_ref, l_ref, acc_ref, kcat_ref):
    i = pl.program_id(2)

    @pl.when(i == 0)
    def _():
        kcat_ref[:, :HEAD_DIM] = kn_ref[...]
        kcat_ref[:, HEAD_DIM:] = kr_ref[0]

    qr = qr_ref[...].astype(F32) * ct_ref[...] + qs_ref[...].astype(F32) * st_ref[...]
    q = jnp.concatenate([qn_ref[...], qr.astype(BF16)], axis=1)

    _flash_reset(m_ref, l_ref, acc_ref)
    s = lax.dot_general(q, _rows(kcat_ref, i), _NT, preferred_element_type=F32)
    r = lax.broadcasted_iota(jnp.int32, s.shape, 0)
    c = lax.broadcasted_iota(jnp.int32, s.shape, 1)
    _flash_update(jnp.where(r >= c, s, NEG), _rows(v_ref, i), m_ref, l_ref, acc_ref)

    def past_tile(t, carry):
        s = lax.dot_general(q, _rows(kcat_ref, t), _NT, preferred_element_type=F32)
        _flash_update(s, _rows(v_ref, t), m_ref, l_ref, acc_ref)
        return carry

    lax.fori_loop(0, i, past_tile, 0)
    o_ref[...] = (acc_ref[...] / l_ref[...]).astype(o_ref.dtype)


def _mla_attention(qx, kv, kr, qc_tab, qs_tab, b, s):
    nq = s // TILE
    hh = N_HEADS
    return pl.pallas_call(
        _mla_kernel,
        out_shape=jax.ShapeDtypeStruct((b * s, hh * MLA_V_DIM), BF16),
        grid=(b, hh, nq),
        in_specs=[pl.BlockSpec((TILE, LANES), lambda bi, h, i: (bi * nq + i, h)),
                  pl.BlockSpec((TILE, LANES), lambda bi, h, i: (bi * nq + i, hh + h // 2)),
                  pl.BlockSpec((TILE, LANES), lambda bi, h, i: (bi * nq + i, hh + hh // 2 + h // 2)),
                  pl.BlockSpec((TILE, LANES), lambda bi, h, i: (i, 0)),
                  pl.BlockSpec((TILE, LANES), lambda bi, h, i: (i, 0)),
                  pl.BlockSpec((s, LANES), lambda bi, h, i: (bi, h)),
                  pl.BlockSpec((1, s, LANES), lambda bi, h, i: (h % 2, bi, 0)),
                  pl.BlockSpec((s, LANES), lambda bi, h, i: (bi, hh + h))],
        out_specs=pl.BlockSpec((TILE, LANES), lambda bi, h, i: (bi * nq + i, h)),
        scratch_shapes=[pltpu.VMEM((TILE, 1), F32), pltpu.VMEM((TILE, 1), F32),
                        pltpu.VMEM((TILE, MLA_V_DIM), F32), pltpu.VMEM((s, 2 * LANES), BF16)],
        compiler_params=_cparams(("parallel", "parallel", "arbitrary")),
        name="mla_attention",
    )(qx, qx, qx, qc_tab, qs_tab, kv, kr, kv)


def _moba_mixer(xb, w_qkv, rel_bias, b, s):
    hd = N_HEADS * HEAD_DIM
    qkv = _matmul(xb, w_qkv.astype(BF16), BF16, tn=512, scale_cols=hd, scale=HEAD_DIM ** -0.5)
    return _moba_attention(qkv, rel_bias, b, s)


def _swa_mixer(xb, w_qkv, sinks, rel_bias, b, s):
    hd = N_HEADS * HEAD_DIM
    qkv = _matmul(xb, w_qkv.astype(BF16), BF16, tn=512, scale_cols=hd, scale=HEAD_DIM ** -0.5)
    return _swa_attention(qkv, sinks, rel_bias, b, s)


def _nsa_mixer(xb, w_in, cmp_pos, cmp_w1, cmp_w2, rel_bias, b, s):
    hd = N_HEADS * HEAD_DIM
    hk_n = NSA_KV_HEADS
    group = N_HEADS // hk_n
    main = hd + 6 * hk_n * HEAD_DIM
    proj = _matmul(xb, w_in[:, :main].astype(BF16), BF16, tn=512, scale_cols=hd, scale=HEAD_DIM ** -0.5)
    wg = w_in[:, main:].reshape(-1, hk_n, 3 * group)
    wg = jnp.pad(wg, ((0, 0), (0, 0), (0, LANES - 3 * group))).reshape(-1, hk_n * LANES)
    gates = _matmul(xb, wg.astype(BF16), F32, tn=hk_n * LANES)
    cmp_kv = _nsa_compress(proj, cmp_pos, cmp_w1.astype(BF16), cmp_w2.astype(BF16), b, s)
    return _nsa_attention(proj, gates, cmp_kv, rel_bias, b, s)


def _mla_mixer(xb, w_down, q_norm, kv_norm, w_uq, w_ukv, b, s):
    hh = N_HEADS
    half = MLA_ROPE_DIM // 2
    lat = MLA_Q_LORA + MLA_KV_LORA
    swap = lambda t: jnp.concatenate([t[..., half:], t[..., :half]], axis=-1)
    w_down_x = jnp.concatenate([w_down, swap(w_down[:, lat:])], axis=1).astype(BF16)
    c = _matmul(xb, w_down_x, F32, tn=w_down_x.shape[1])

    inv = ROPE_BASE ** (-jnp.arange(0, MLA_ROPE_DIM, 2, dtype=F32) / MLA_ROPE_DIM)
    ang = jnp.arange(s, dtype=F32)[:, None] * inv[None, :]
    cos, sin = jnp.cos(ang), jnp.sin(ang)
    ktab = jnp.concatenate([cos, cos, -sin, sin], axis=1)
    qc_tab = jnp.concatenate([cos, cos, cos, cos], axis=1)
    qs_tab = jnp.concatenate([-sin, sin, -sin, sin], axis=1)
    cq, ckv, kr = _mla_prep(c, q_norm, kv_norm, ktab, s)

    wq = w_uq.reshape(MLA_Q_LORA, hh, MLA_NOPE_DIM + MLA_ROPE_DIM)
    wq_rope = wq[:, :, MLA_NOPE_DIM:]
    wq_x = jnp.concatenate([wq[:, :, :MLA_NOPE_DIM].reshape(MLA_Q_LORA, -1),
                            wq_rope.reshape(MLA_Q_LORA, -1),
                            swap(wq_rope).reshape(MLA_Q_LORA, -1)], axis=1).astype(BF16)
    scale = (MLA_NOPE_DIM + MLA_ROPE_DIM) ** -0.5
    qx = _matmul(cq, wq_x, BF16, tn=512, scale_cols=wq_x.shape[1], scale=scale)
    wkv = w_ukv.reshape(MLA_KV_LORA, hh, MLA_NOPE_DIM + MLA_V_DIM)
    wkv_x = jnp.concatenate([wkv[:, :, :MLA_NOPE_DIM].reshape(MLA_KV_LORA, -1),
                             wkv[:, :, MLA_NOPE_DIM:].reshape(MLA_KV_LORA, -1)], axis=1).astype(BF16)
    kv = _matmul(ckv, wkv_x, BF16, tn=512)
    return _mla_attention(qx, kv, kr, qc_tab, qs_tab, b, s)


def kernel(x, rel_bias, moba_w_qkv, moba_w_o, swa_w_qkv, swa_sinks, swa_w_o, nsa_w_in, nsa_cmp_pos,
           nsa_cmp_w1, nsa_cmp_w2, nsa_w_o, mla_w_down, mla_q_norm, mla_kv_norm, mla_w_uq, mla_w_ukv,
           mla_w_o, ln1_g, ln1_b, ffn_w_gate, ffn_w_up, ffn_w_down, ln2_g, ln2_b):
    b, s, d = x.shape
    depth = ln1_g.shape[0]
    n_mixers = 4
    xf = x.reshape(b * s, d).astype(F32)
    xb = xf.astype(BF16)
    for i in range(depth):
        kind, j = i % n_mixers, i // n_mixers
        if kind == 0:
            o = _moba_mixer(xb, moba_w_qkv[j], rel_bias, b, s)
            w_o = moba_w_o[j]
        elif kind == 1:
            o = _swa_mixer(xb, swa_w_qkv[j], swa_sinks[j], rel_bias, b, s)
            w_o = swa_w_o[j]
        elif kind == 2:
            o = _nsa_mixer(xb, nsa_w_in[j], nsa_cmp_pos[j], nsa_cmp_w1[j], nsa_cmp_w2[j], rel_bias, b, s)
            w_o = nsa_w_o[j]
        else:
            o = _mla_mixer(xb, mla_w_down[j], mla_q_norm[j], mla_kv_norm[j], mla_w_uq[j], mla_w_ukv[j], b, s)
            w_o = mla_w_o[j]
        xf, xb = _proj_ln(o, w_o.astype(BF16), xf, ln1_g[i], ln1_b[i])
        hmid = _ffn_up(xb, ffn_w_gate[i].astype(BF16), ffn_w_up[i].astype(BF16))
        xf, xb = _proj_ln(hmid, ffn_w_down[i].astype(BF16), xf, ln2_g[i], ln2_b[i])
    return xf.reshape(b, s, d).astype(x.dtype)
```

```python
import functools
import math

import jax
import jax.numpy as jnp
import numpy as np
from jax import lax
from jax.experimental import pallas as pl
from jax.experimental.pallas import tpu as pltpu

F32 = jnp.float32
BF16 = jnp.bfloat16

N_HEADS = 16
HEAD_DIM = 128
REL_BUCKETS = 32
REL_MAX_DIST = 128
MOBA_BLOCK = 256
MOBA_TOPK = 3
SWA_WINDOW = 128
SWA_KV_HEADS = 2
NSA_KV_HEADS = 4
NSA_CMP_LEN = 32
NSA_CMP_STRIDE = 16
NSA_CMP_HIDDEN = 256
NSA_SEL_BLOCK = 64
NSA_TOPN = 16
NSA_WINDOW = 512
MLA_Q_LORA = 512
MLA_KV_LORA = 512
MLA_NOPE_DIM = 128
MLA_ROPE_DIM = 64
MLA_V_DIM = 128
ROPE_BASE = 10000.0
DEPTH = 4
ALPHA = (2 * DEPTH) ** 0.25
LN_EPS = 1e-5
RMS_EPS = 1e-6

LANES = 128
TILE = 256
NEG = -1e30
VMEM_LIMIT = 56 * 1024 * 1024

_NT = (((1,), (1,)), ((), ()))


def _cparams(sem):
    return pltpu.CompilerParams(dimension_semantics=sem, vmem_limit_bytes=VMEM_LIMIT)


def _matmul_kernel(x_ref, w_ref, o_ref, *, scale_tiles, scale):
    acc = jnp.dot(x_ref[...], w_ref[...], preferred_element_type=F32)
    if scale_tiles:
        acc = acc * jnp.where(pl.program_id(1) < scale_tiles, scale, 1.0).astype(F32)
    o_ref[...] = acc.astype(o_ref.dtype)


def _matmul(x, w, out_dtype, *, tn, tm=1024, scale_cols=0, scale=1.0):
    m, k = x.shape
    n = w.shape[1]
    tm = min(tm, m)
    assert m % tm == 0 and n % tn == 0 and scale_cols % tn == 0
    kern = functools.partial(_matmul_kernel, scale_tiles=scale_cols // tn, scale=scale)
    return pl.pallas_call(
        kern,
        out_shape=jax.ShapeDtypeStruct((m, n), out_dtype),
        grid=(m // tm, n // tn),
        in_specs=[pl.BlockSpec((tm, k), lambda i, j: (i, 0)),
                  pl.BlockSpec((k, tn), lambda i, j: (0, j))],
        out_specs=pl.BlockSpec((tm, tn), lambda i, j: (i, j)),
        compiler_params=_cparams(("parallel", "arbitrary")),
        name="proj_matmul",
    )(x, w)


def _proj_ln_kernel(a_ref, w_ref, x_ref, g_ref, b_ref, o_ref, ob_ref, acc_ref):
    kk = pl.program_id(1)
    part = jnp.dot(a_ref[...], w_ref[...], preferred_element_type=F32)

    @pl.when(kk == 0)
    def _():
        acc_ref[...] = part

    @pl.when(kk > 0)
    def _():
        acc_ref[...] += part

    @pl.when(kk == pl.num_programs(1) - 1)
    def _():
        y = ALPHA * x_ref[...] + acc_ref[...]
        mu = jnp.mean(y, axis=-1, keepdims=True)
        yc = y - mu
        var = jnp.mean(yc * yc, axis=-1, keepdims=True)
        out = yc * lax.rsqrt(var + LN_EPS) * g_ref[...] + b_ref[...]
        o_ref[...] = out
        ob_ref[...] = out.astype(BF16)


MAX_TK = 2048


def _proj_ln(a, w, x, g, b, *, tm=512, tk=None):
    m, k = a.shape
    d = w.shape[1]
    tm = min(tm, m)
    if tk is None:
        tk = max(t for t in range(LANES, min(k, MAX_TK) + 1, LANES) if k % t == 0)
    assert m % tm == 0 and k % tk == 0
    return pl.pallas_call(
        _proj_ln_kernel,
        out_shape=(jax.ShapeDtypeStruct((m, d), F32), jax.ShapeDtypeStruct((m, d), BF16)),
        grid=(m // tm, k // tk),
        in_specs=[pl.BlockSpec((tm, tk), lambda i, j: (i, j)),
                  pl.BlockSpec((tk, d), lambda i, j: (j, 0)),
                  pl.BlockSpec((tm, d), lambda i, j: (i, 0)),
                  pl.BlockSpec((1, d), lambda i, j: (0, 0)),
                  pl.BlockSpec((1, d), lambda i, j: (0, 0))],
        out_specs=(pl.BlockSpec((tm, d), lambda i, j: (i, 0)),
                   pl.BlockSpec((tm, d), lambda i, j: (i, 0))),
        scratch_shapes=[pltpu.VMEM((tm, d), F32)],
        compiler_params=_cparams(("parallel", "arbitrary")),
        name="proj_deepnorm",
    )(a, w, x, g.reshape(1, d), b.reshape(1, d))


def _ffn_up_kernel(x_ref, wg_ref, wu_ref, o_ref):
    x = x_ref[...]
    gate = jnp.dot(x, wg_ref[...], preferred_element_type=F32)
    up = jnp.dot(x, wu_ref[...], preferred_element_type=F32)
    o_ref[...] = (gate * jax.nn.sigmoid(gate) * up).astype(o_ref.dtype)


def _ffn_up(x, wg, wu, *, tm=1024, tn=512):
    m, k = x.shape
    f = wg.shape[1]
    tm = min(tm, m)
    tn = min(tn, f)
    assert m % tm == 0 and f % tn == 0
    return pl.pallas_call(
        _ffn_up_kernel,
        out_shape=jax.ShapeDtypeStruct((m, f), BF16),
        grid=(m // tm, f // tn),
        in_specs=[pl.BlockSpec((tm, k), lambda i, j: (i, 0)),
                  pl.BlockSpec((k, tn), lambda i, j: (0, j)),
                  pl.BlockSpec((k, tn), lambda i, j: (0, j))],
        out_specs=pl.BlockSpec((tm, tn), lambda i, j: (i, j)),
        compiler_params=_cparams(("parallel", "arbitrary")),
        name="ffn_gate_up",
    )(x, wg, wu)


def _t5_bucket(dist):
    n = np.maximum(dist, 0)
    max_exact = REL_BUCKETS // 2
    nf = np.maximum(n, max_exact).astype(np.float32)
    large = max_exact + (np.log(nf / max_exact) / math.log(REL_MAX_DIST / max_exact)
                         * (REL_BUCKETS - max_exact)).astype(np.int32)
    return np.where(n < max_exact, n, np.minimum(large, REL_BUCKETS - 1)).astype(np.int32)


def _bias_lookup(rel_bias, dist, visible):
    bucket = _t5_bucket(dist).reshape(-1)
    onehot = (jnp.arange(REL_BUCKETS, dtype=jnp.int32)[:, None] == jnp.asarray(bucket)[None, :]).astype(F32)
    tab = jnp.dot(rel_bias.T.astype(F32), onehot, precision=lax.Precision.HIGHEST)
    tab = tab.reshape((rel_bias.shape[1],) + dist.shape)
    return jnp.where(jnp.asarray(visible)[None], tab, NEG)


def _bias_tiles(rel_bias, window=None):
    r = np.arange(TILE)[:, None]
    c = np.arange(TILE)[None, :]
    d0 = r - c
    d1 = TILE + r - c
    ok0 = d0 >= 0
    ok1 = np.ones_like(ok0)
    if window is not None:
        ok0 = ok0 & (d0 < window)
        ok1 = d1 < window
    t0 = _bias_lookup(rel_bias, d0, ok0)
    t1 = _bias_lookup(rel_bias, d1, ok1)
    far = rel_bias[int(_t5_bucket(np.asarray(2 * TILE)))].astype(F32)
    return t0, t1, far


def _flash_update(s, v, m_ref, l_ref, acc_ref):
    m_prev = m_ref[...]
    m_new = jnp.maximum(m_prev, jnp.max(s, axis=1, keepdims=True))
    alpha = jnp.exp(m_prev - m_new)
    p = jnp.exp(s - jnp.concatenate([m_new] * (s.shape[1] // LANES), axis=1))
    l_ref[...] = alpha * l_ref[...] + jnp.sum(p, axis=1, keepdims=True)
    acc_ref[...] = alpha * acc_ref[...] + jnp.dot(p.astype(BF16), v, preferred_element_type=F32)
    m_ref[...] = m_new


def _flash_reset(m_ref, l_ref, acc_ref):
    m_ref[...] = jnp.full(m_ref.shape, NEG, F32)
    l_ref[...] = jnp.zeros(l_ref.shape, F32)
    acc_ref[...] = jnp.zeros(acc_ref.shape, F32)


def _rows(ref, t):
    return ref[pl.ds(pl.multiple_of(t * TILE, TILE), TILE), :]


def _split3(x):
    hi = x.astype(BF16)
    r1 = x - hi.astype(F32)
    mid = r1.astype(BF16)
    lo = (r1 - mid.astype(F32)).astype(BF16)
    return hi, mid, lo


def _rank_before(val, lane, n):
    cnt = jnp.zeros(val.shape, jnp.int32)
    for m in range(n):
        vm = val[:, m:m + 1]
        beats = (vm > val) | ((vm == val) & (m < lane))
        cnt = cnt + beats.astype(jnp.int32)
    return cnt


def _moba_kernel(far_ref, q_ref, k_ref, v_ref, t0_ref, t1_ref, o_ref,
                 m_ref, l_ref, acc_ref, sel_ref, *, nblk):
    h = pl.program_id(1)
    i = pl.program_id(2)
    q = q_ref[...]

    kmean = jnp.concatenate(
        [jnp.mean(k_ref[n * TILE:(n + 1) * TILE, :].astype(F32), axis=0, keepdims=True)
         for n in range(nblk)]
        + [jnp.zeros((LANES - nblk, HEAD_DIM), F32)], axis=0)
    gate = sum(lax.dot_general(q, part, _NT, preferred_element_type=F32)
               for part in _split3(kmean))
    lane = lax.broadcasted_iota(jnp.int32, gate.shape, 1)
    past = lane < i
    gate = jnp.where(past, gate, -jnp.inf)
    cnt = _rank_before(gate, lane, nblk)
    sel_ref[...] = jnp.where(past & (cnt < MOBA_TOPK), 0.0, NEG).astype(F32)

    _flash_reset(m_ref, l_ref, acc_ref)
    s = lax.dot_general(q, _rows(k_ref, i), _NT, preferred_element_type=F32) + t0_ref[0]
    _flash_update(s, _rows(v_ref, i), m_ref, l_ref, acc_ref)

    for t in range(nblk - 1):
        @pl.when(t < i)
        def _(t=t):
            s = lax.dot_general(q, k_ref[t * TILE:(t + 1) * TILE, :], _NT, preferred_element_type=F32)
            bias = jnp.where(t == i - 1, t1_ref[0], far_ref[h])
            s = s + bias + sel_ref[:, t:t + 1]
            _flash_update(s, v_ref[t * TILE:(t + 1) * TILE, :], m_ref, l_ref, acc_ref)

    o_ref[...] = (acc_ref[...] / l_ref[...]).astype(o_ref.dtype)


def _moba_attention(qkv, rel_bias, b, s):
    assert MOBA_BLOCK == TILE and s % TILE == 0
    nq = s // TILE
    assert max(1, min(MOBA_TOPK, nq - 1)) == MOBA_TOPK
    hh = N_HEADS
    t0, t1, far = _bias_tiles(rel_bias)
    kern = functools.partial(_moba_kernel, nblk=nq)
    return pl.pallas_call(
        kern,
        out_shape=jax.ShapeDtypeStruct((b * s, hh * HEAD_DIM), BF16),
        grid=(b, hh, nq),
        in_specs=[pl.BlockSpec(memory_space=pltpu.SMEM),
                  pl.BlockSpec((TILE, HEAD_DIM), lambda bi, h, i: (bi * nq + i, h)),
                  pl.BlockSpec((s, HEAD_DIM), lambda bi, h, i: (bi, hh + h)),
                  pl.BlockSpec((s, HEAD_DIM), lambda bi, h, i: (bi, 2 * hh + h)),
                  pl.BlockSpec((1, TILE, TILE), lambda bi, h, i: (h, 0, 0)),
                  pl.BlockSpec((1, TILE, TILE), lambda bi, h, i: (h, 0, 0))],
        out_specs=pl.BlockSpec((TILE, HEAD_DIM), lambda bi, h, i: (bi * nq + i, h)),
        scratch_shapes=[pltpu.VMEM((TILE, LANES), F32), pltpu.VMEM((TILE, LANES), F32),
                        pltpu.VMEM((TILE, HEAD_DIM), F32), pltpu.VMEM((TILE, LANES), F32)],
        compiler_params=_cparams(("parallel", "parallel", "arbitrary")),
        name="moba_attention",
    )(far, qkv, qkv, qkv, t0, t1)


def _swa_kernel(sink_ref, q_ref, k_ref, v_ref, t0_ref, t1_ref, o_ref, m_ref, l_ref, acc_ref, *, group):
    hk = pl.program_id(0)
    i = pl.program_id(2)
    q = jnp.concatenate([q_ref[:, g * HEAD_DIM:(g + 1) * HEAD_DIM] for g in range(group)], axis=0)

    for g in range(group):
        m_ref[g * TILE:(g + 1) * TILE, :] = jnp.full((TILE, LANES), sink_ref[hk * group + g], F32)
    l_ref[...] = jnp.ones(l_ref.shape, F32)
    acc_ref[...] = jnp.zeros(acc_ref.shape, F32)

    s = lax.dot_general(q, _rows(k_ref, i), _NT, preferred_element_type=F32)
    s = s + t0_ref[...].reshape(group * TILE, TILE)
    _flash_update(s, _rows(v_ref, i), m_ref, l_ref, acc_ref)

    @pl.when(i > 0)
    def _():
        s = lax.dot_general(q, _rows(k_ref, i - 1), _NT, preferred_element_type=F32)
        s = s + t1_ref[...].reshape(group * TILE, TILE)
        _flash_update(s, _rows(v_ref, i - 1), m_ref, l_ref, acc_ref)

    out = acc_ref[...] / l_ref[...]
    for g in range(group):
        o_ref[:, g * HEAD_DIM:(g + 1) * HEAD_DIM] = out[g * TILE:(g + 1) * TILE].astype(o_ref.dtype)


def _swa_attention(qkv, sinks, rel_bias, b, s):
    assert SWA_WINDOW <= TILE and s % TILE == 0
    nq = s // TILE
    hk_n = SWA_KV_HEADS
    group = N_HEADS // hk_n
    t0, t1, _ = _bias_tiles(rel_bias, window=SWA_WINDOW)
    kern = functools.partial(_swa_kernel, group=group)
    gw = group * HEAD_DIM
    return pl.pallas_call(
        kern,
        out_shape=jax.ShapeDtypeStruct((b * s, N_HEADS * HEAD_DIM), BF16),
        grid=(hk_n, b, nq),
        in_specs=[pl.BlockSpec(memory_space=pltpu.SMEM),
                  pl.BlockSpec((TILE, gw), lambda hk, bi, i: (bi * nq + i, hk)),
                  pl.BlockSpec((s, HEAD_DIM), lambda hk, bi, i: (bi, N_HEADS + hk)),
                  pl.BlockSpec((s, HEAD_DIM), lambda hk, bi, i: (bi, N_HEADS + hk_n + hk)),
                  pl.BlockSpec((group, TILE, TILE), lambda hk, bi, i: (hk, 0, 0)),
                  pl.BlockSpec((group, TILE, TILE), lambda hk, bi, i: (hk, 0, 0))],
        out_specs=pl.BlockSpec((TILE, gw), lambda hk, bi, i: (bi * nq + i, hk)),
        scratch_shapes=[pltpu.VMEM((group * TILE, LANES), F32), pltpu.VMEM((group * TILE, LANES), F32),
                        pltpu.VMEM((group * TILE, HEAD_DIM), F32)],
        compiler_params=_cparams(("parallel", "parallel", "arbitrary")),
        name="swa_attention",
    )(sinks.astype(F32), qkv, qkv, qkv, t0, t1)


def _gelu_tanh(x):
    return 0.5 * x * (1.0 + jnp.tanh(math.sqrt(2.0 / math.pi) * (x + 0.044715 * (x * x * x))))


def _nsa_compress_kernel(t_ref, pe_ref, w1_ref, w2_ref, o_ref, tf_ref, *, ncmp_pad):
    stride = NSA_CMP_STRIDE
    half = stride * HEAD_DIM
    pe = pe_ref[0]
    tf_ref[...] = t_ref[...].astype(F32)
    xa, xb = [], []
    for r in range(stride):
        x = tf_ref[pl.ds(r, ncmp_pad, stride=stride), :]
        xa.append((x + pe[r:r + 1, :]).astype(BF16))
        xb.append((x + pe[stride + r:stride + r + 1, :]).astype(BF16))
    a = jnp.dot(jnp.concatenate(xa, axis=1), w1_ref[0, :half, :], preferred_element_type=F32)
    bm = jnp.dot(jnp.concatenate(xb, axis=1), w1_ref[0, half:, :], preferred_element_type=F32)
    hid = _gelu_tanh(a + pltpu.roll(bm, ncmp_pad - 1, 0))
    o_ref[0, 0, 0] = jnp.dot(hid.astype(BF16), w2_ref[0], preferred_element_type=F32).astype(o_ref.dtype)


def _nsa_compress(proj, pe, w1, w2, b, s):
    assert NSA_CMP_LEN == 2 * NSA_CMP_STRIDE and s % NSA_CMP_STRIDE == 0
    ncp = s // NSA_CMP_STRIDE
    hk_n = NSA_KV_HEADS
    kern = functools.partial(_nsa_compress_kernel, ncmp_pad=ncp)
    return pl.pallas_call(
        kern,
        out_shape=jax.ShapeDtypeStruct((2, b, hk_n, ncp, HEAD_DIM), BF16),
        grid=(2, b, hk_n),
        in_specs=[pl.BlockSpec((s, HEAD_DIM), lambda kv, bi, hk: (bi, N_HEADS + kv * hk_n + hk)),
                  pl.BlockSpec((1, NSA_CMP_LEN, HEAD_DIM), lambda kv, bi, hk: (kv, 0, 0)),
                  pl.BlockSpec((1, NSA_CMP_LEN * HEAD_DIM, NSA_CMP_HIDDEN), lambda kv, bi, hk: (kv, 0, 0)),
                  pl.BlockSpec((1, NSA_CMP_HIDDEN, HEAD_DIM), lambda kv, bi, hk: (kv, 0, 0))],
        out_specs=pl.BlockSpec((1, 1, 1, ncp, HEAD_DIM), lambda kv, bi, hk: (kv, bi, hk, 0, 0)),
        scratch_shapes=[pltpu.VMEM((s, HEAD_DIM), F32)],
        compiler_params=_cparams(("parallel", "parallel", "arbitrary")),
        name="nsa_compress",
    )(proj, pe.astype(F32), w1, w2)


def _nsa_kernel(far_ref, q_ref, kc_ref, vc_ref, cb_ref, ksl_ref, vsl_ref, kw_ref, vw_ref, gate_ref,
                t0_ref, t1_ref, tw_ref, ov_ref, ex_ref, o_ref,
                m_ref, l_ref, acc_ref, sel_ref, oslc_ref, *, group, nq):
    hk = pl.program_id(1)
    i = pl.program_id(2)
    rows = group * TILE
    q = jnp.concatenate([q_ref[:, g * HEAD_DIM:(g + 1) * HEAD_DIM] for g in range(group)], axis=0)

    cb = cb_ref[...].reshape(rows, LANES)
    sc = lax.dot_general(q, kc_ref[0, 0, 0], _NT, preferred_element_type=F32) + cb
    valid = cb > 0.5 * NEG
    mc = jnp.max(sc, axis=1, keepdims=True)
    pc = jnp.where(valid, jnp.exp(sc - mc), 0.0)
    pc = pc / jnp.maximum(jnp.sum(pc, axis=1, keepdims=True), 1e-30)
    o_cmp = jnp.dot(pc.astype(BF16), vc_ref[0, 0, 0], preferred_element_type=F32)

    psum = pc[0:TILE]
    for g in range(1, group):
        psum = psum + pc[g * TILE:(g + 1) * TILE]
    imp = sum(jnp.dot(part, ov_ref[...], preferred_element_type=F32) for part in _split3(psum))
    lane = lax.broadcasted_iota(jnp.int32, imp.shape, 1)
    qblk = (i * TILE + lax.broadcasted_iota(jnp.int32, imp.shape, 0)) // NSA_SEL_BLOCK
    forced = (lane == 0) | (lane == qblk) | (lane == qblk - 1)
    imp = jnp.where(lane > qblk, -jnp.inf, jnp.where(forced, jnp.inf, imp))
    cnt = _rank_before(imp, lane, nq * (TILE // NSA_SEL_BLOCK))
    chosen = jnp.where((lane <= qblk) & (cnt < NSA_TOPN), 1.0, 0.0).astype(BF16)
    for t in range(nq):
        @pl.when(t <= i)
        def _(t=t):
            hit = jnp.dot(chosen, ex_ref[:, t * TILE:(t + 1) * TILE], preferred_element_type=F32)
            sel_ref[t] = (hit - 1.0) * (-NEG)

    t0 = t0_ref[...].reshape(rows, TILE)

    def masked(s, t, with_far=False):
        sel = sel_ref[t]
        parts = []
        for g in range(group):
            add = sel + far_ref[hk * group + g] if with_far else sel
            parts.append(s[g * TILE:(g + 1) * TILE] + add)
        return jnp.concatenate(parts, axis=0)

    _flash_reset(m_ref, l_ref, acc_ref)
    s = lax.dot_general(q, _rows(ksl_ref, i), _NT, preferred_element_type=F32) + t0
    _flash_update(masked(s, i), _rows(vsl_ref, i), m_ref, l_ref, acc_ref)

    @pl.when(i > 0)
    def _():
        s = lax.dot_general(q, _rows(ksl_ref, i - 1), _NT, preferred_element_type=F32)
        s = s + t1_ref[...].reshape(rows, TILE)
        _flash_update(masked(s, i - 1), _rows(vsl_ref, i - 1), m_ref, l_ref, acc_ref)

    def far_tile(t, carry):
        s = lax.dot_general(q, _rows(ksl_ref, t), _NT, preferred_element_type=F32)
        _flash_update(masked(s, t, with_far=True), _rows(vsl_ref, t), m_ref, l_ref, acc_ref)
        return carry

    lax.fori_loop(0, jnp.maximum(i - 1, 0), far_tile, 0)
    oslc_ref[...] = acc_ref[...] / l_ref[...]

    _flash_reset(m_ref, l_ref, acc_ref)
    s = lax.dot_general(q, _rows(kw_ref, i), _NT, preferred_element_type=F32) + t0
    _flash_update(s, _rows(vw_ref, i), m_ref, l_ref, acc_ref)

    @pl.when(i > 0)
    def _():
        s = lax.dot_general(q, _rows(kw_ref, i - 1), _NT, preferred_element_type=F32)
        s = s + t1_ref[...].reshape(rows, TILE)
        _flash_update(s, _rows(vw_ref, i - 1), m_ref, l_ref, acc_ref)

    @pl.when(i > 1)
    def _():
        s = lax.dot_general(q, _rows(kw_ref, i - 2), _NT, preferred_element_type=F32)
        s = s + tw_ref[...].reshape(rows, TILE)
        _flash_update(s, _rows(vw_ref, i - 2), m_ref, l_ref, acc_ref)

    o_win = acc_ref[...] / l_ref[...]
    o_slc = oslc_ref[...]
    gates = jax.nn.sigmoid(gate_ref[...])
    for g in range(group):
        sl = slice(g * TILE, (g + 1) * TILE)
        o = (gates[:, 3 * g:3 * g + 1] * o_cmp[sl] + gates[:, 3 * g + 1:3 * g + 2] * o_slc[sl]
             + gates[:, 3 * g + 2:3 * g + 3] * o_win[sl])
        o_ref[:, g * HEAD_DIM:(g + 1) * HEAD_DIM] = o.astype(o_ref.dtype)


def _nsa_attention(proj, gates, cmp_kv, rel_bias, b, s):
    assert s % TILE == 0 and NSA_WINDOW == 2 * TILE and TILE % NSA_SEL_BLOCK == 0
    nq = s // TILE
    hk_n = NSA_KV_HEADS
    group = N_HEADS // hk_n
    gw = group * HEAD_DIM
    ncp = s // NSA_CMP_STRIDE
    n_cmp = (s - NSA_CMP_LEN) // NSA_CMP_STRIDE + 1
    n_sel = s // NSA_SEL_BLOCK
    assert ncp == LANES and n_sel <= LANES and min(NSA_TOPN, n_sel) == NSA_TOPN
    t0, t1, far = _bias_tiles(rel_bias)
    r = np.arange(TILE)[:, None]
    c = np.arange(TILE)[None, :]
    tw = jnp.where(jnp.asarray(c > r)[None], far[:, None, None], NEG)
    pos = np.arange(s)[:, None]
    cidx = np.arange(ncp)[None, :]
    cdist = pos - (cidx * NSA_CMP_STRIDE + NSA_CMP_LEN - 1)
    cmp_bias = _bias_lookup(rel_bias, cdist, (cdist >= 0) & (cidx < n_cmp))
    ci = np.arange(ncp)[:, None] * NSA_CMP_STRIDE
    sj = np.arange(LANES)[None, :] * NSA_SEL_BLOCK
    overlap = jnp.asarray((ci < sj + NSA_SEL_BLOCK) & (ci + NSA_CMP_LEN > sj)
                          & (np.arange(ncp)[:, None] < n_cmp) & (np.arange(LANES)[None, :] < n_sel), BF16)
    expand = jnp.asarray(np.arange(LANES)[:, None] == (np.arange(s)[None, :] // NSA_SEL_BLOCK), BF16)

    kern = functools.partial(_nsa_kernel, group=group, nq=nq)
    kv_spec = lambda off: pl.BlockSpec((s, HEAD_DIM), lambda bi, hk, i: (bi, N_HEADS + off * hk_n + hk))
    cmp_spec = lambda kv: pl.BlockSpec((1, 1, 1, ncp, HEAD_DIM), lambda bi, hk, i: (kv, bi, hk, 0, 0))
    tile_spec = pl.BlockSpec((group, TILE, TILE), lambda bi, hk, i: (hk, 0, 0))
    return pl.pallas_call(
        kern,
        out_shape=jax.ShapeDtypeStruct((b * s, N_HEADS * HEAD_DIM), BF16),
        grid=(b, hk_n, nq),
        in_specs=[pl.BlockSpec(memory_space=pltpu.SMEM),
                  pl.BlockSpec((TILE, gw), lambda bi, hk, i: (bi * nq + i, hk)),
                  cmp_spec(0), cmp_spec(1),
                  pl.BlockSpec((group, TILE, ncp), lambda bi, hk, i: (hk, i, 0)),
                  kv_spec(2), kv_spec(3), kv_spec(4), kv_spec(5),
                  pl.BlockSpec((TILE, LANES), lambda bi, hk, i: (bi * nq + i, hk)),
                  tile_spec, tile_spec, tile_spec,
                  pl.BlockSpec((ncp, LANES), lambda bi, hk, i: (0, 0)),
                  pl.BlockSpec((LANES, s), lambda bi, hk, i: (0, 0))],
        out_specs=pl.BlockSpec((TILE, gw), lambda bi, hk, i: (bi * nq + i, hk)),
        scratch_shapes=[pltpu.VMEM((group * TILE, LANES), F32), pltpu.VMEM((group * TILE, LANES), F32),
                        pltpu.VMEM((group * TILE, HEAD_DIM), F32),
                        pltpu.VMEM((nq, TILE, TILE), F32),
                        pltpu.VMEM((group * TILE, HEAD_DIM), F32)],
        compiler_params=_cparams(("parallel", "parallel", "arbitrary")),
        name="nsa_attention",
    )(far, proj, cmp_kv, cmp_kv, cmp_bias, proj, proj, proj, proj, gates, t0, t1, tw, overlap, expand)


def _rms(x, g):
    return x * lax.rsqrt(jnp.mean(x * x, axis=-1, keepdims=True) + RMS_EPS) * g


def _mla_prep_kernel(c_ref, qg_ref, kg_ref, tab_ref, cq_ref, ckv_ref, kr_ref):
    c = c_ref[...]
    cq_ref[...] = _rms(c[:, :MLA_Q_LORA], qg_ref[...]).astype(BF16)
    ckv_ref[...] = _rms(c[:, MLA_Q_LORA:MLA_Q_LORA + MLA_KV_LORA], kg_ref[...]).astype(BF16)
    y = c[:, MLA_Q_LORA + MLA_KV_LORA:] * tab_ref[...]
    y = y + pltpu.roll(y, MLA_ROPE_DIM, 1)
    lane = lax.broadcasted_iota(jnp.int32, y.shape, 1)
    kr_ref[0] = jnp.where(lane < MLA_ROPE_DIM, y, 0.0).astype(BF16)
    kr_ref[1] = jnp.where(lane >= MLA_ROPE_DIM, y, 0.0).astype(BF16)


def _mla_prep(c, q_norm, kv_norm, ktab, s, *, tm=512):
    m, w = c.shape
    tm = min(tm, s)
    assert 2 * MLA_ROPE_DIM == LANES and w == MLA_Q_LORA + MLA_KV_LORA + LANES and s % tm == 0
    ns = s // tm
    return pl.pallas_call(
        _mla_prep_kernel,
        out_shape=(jax.ShapeDtypeStruct((m, MLA_Q_LORA), BF16),
                   jax.ShapeDtypeStruct((m, MLA_KV_LORA), BF16),
                   jax.ShapeDtypeStruct((2, m, LANES), BF16)),
        grid=(m // tm,),
        in_specs=[pl.BlockSpec((tm, w), lambda i: (i, 0)),
                  pl.BlockSpec((1, MLA_Q_LORA), lambda i: (0, 0)),
                  pl.BlockSpec((1, MLA_KV_LORA), lambda i: (0, 0)),
                  pl.BlockSpec((tm, LANES), lambda i: (i % ns, 0))],
        out_specs=(pl.BlockSpec((tm, MLA_Q_LORA), lambda i: (i, 0)),
                   pl.BlockSpec((tm, MLA_KV_LORA), lambda i: (i, 0)),
                   pl.BlockSpec((2, tm, LANES), lambda i: (0, i, 0))),
        compiler_params=_cparams(("parallel",)),
        name="mla_prep",
    )(c, q_norm.reshape(1, -1).astype(F32), kv_norm.reshape(1, -1).astype(F32), ktab)


def _mla_kernel(qn_ref, qr_ref, qs_ref, ct_ref, st_ref, kn_ref, kr_ref, v_ref, o_ref,
                m_ref, l_ref, acc_ref, kcat_ref):
    i = pl.program_id(2)

    @pl.when(i == 0)
    def _():
        kcat_ref[:, :HEAD_DIM] = kn_ref[...]
        kcat_ref[:, HEAD_DIM:] = kr_ref[0]

    qr = qr_ref[...].astype(F32) * ct_ref[...] + qs_ref[...].astype(F32) * st_ref[...]
    q = jnp.concatenate([qn_ref[...], qr.astype(BF16)], axis=1)

    _flash_reset(m_ref, l_ref, acc_ref)
    s = lax.dot_general(q, _rows(kcat_ref, i), _NT, preferred_element_type=F32)
    r = lax.broadcasted_iota(jnp.int32, s.shape, 0)
    c = lax.broadcasted_iota(jnp.int32, s.shape, 1)
    _flash_update(jnp.where(r >= c, s, NEG), _rows(v_ref, i), m_ref, l_ref, acc_ref)

    def past_tile(t, carry):
        s = lax.dot_general(q, _rows(kcat_ref, t), _NT, preferred_element_type=F32)
        _flash_update(s, _rows(v_ref, t), m_ref, l_ref, acc_ref)
        return carry

    lax.fori_loop(0, i, past_tile, 0)
    o_ref[...] = (acc_ref[...] / l_ref[...]).astype(o_ref.dtype)


def _mla_attention(qx, kv, kr, qc_tab, qs_tab, b, s):
    nq = s // TILE
    hh = N_HEADS
    return pl.pallas_call(
        _mla_kernel,
        out_shape=jax.ShapeDtypeStruct((b * s, hh * MLA_V_DIM), BF16),
        grid=(b, hh, nq),
        in_specs=[pl.BlockSpec((TILE, LANES), lambda bi, h, i: (bi * nq + i, h)),
                  pl.BlockSpec((TILE, LANES), lambda bi, h, i: (bi * nq + i, hh + h // 2)),
                  pl.BlockSpec((TILE, LANES), lambda bi, h, i: (bi * nq + i, hh + hh // 2 + h // 2)),
                  pl.BlockSpec((TILE, LANES), lambda bi, h, i: (i, 0)),
                  pl.BlockSpec((TILE, LANES), lambda bi, h, i: (i, 0)),
                  pl.BlockSpec((s, LANES), lambda bi, h, i: (bi, h)),
                  pl.BlockSpec((1, s, LANES), lambda bi, h, i: (h % 2, bi, 0)),
                  pl.BlockSpec((s, LANES), lambda bi, h, i: (bi, hh + h))],
        out_specs=pl.BlockSpec((TILE, LANES), lambda bi, h, i: (bi * nq + i, h)),
        scratch_shapes=[pltpu.VMEM((TILE, LANES), F32), pltpu.VMEM((TILE, LANES), F32),
                        pltpu.VMEM((TILE, MLA_V_DIM), F32), pltpu.VMEM((s, 2 * LANES), BF16)],
        compiler_params=_cparams(("parallel", "parallel", "arbitrary")),
        name="mla_attention",
    )(qx, qx, qx, qc_tab, qs_tab, kv, kr, kv)


def _moba_mixer(xb, w_qkv, rel_bias, b, s):
    hd = N_HEADS * HEAD_DIM
    qkv = _matmul(xb, w_qkv.astype(BF16), BF16, tn=512, scale_cols=hd, scale=HEAD_DIM ** -0.5)
    return _moba_attention(qkv, rel_bias, b, s)


def _swa_mixer(xb, w_qkv, sinks, rel_bias, b, s):
    hd = N_HEADS * HEAD_DIM
    qkv = _matmul(xb, w_qkv.astype(BF16), BF16, tn=512, scale_cols=hd, scale=HEAD_DIM ** -0.5)
    return _swa_attention(qkv, sinks, rel_bias, b, s)


def _nsa_mixer(xb, w_in, cmp_pos, cmp_w1, cmp_w2, rel_bias, b, s):
    hd = N_HEADS * HEAD_DIM
    hk_n = NSA_KV_HEADS
    group = N_HEADS // hk_n
    main = hd + 6 * hk_n * HEAD_DIM
    proj = _matmul(xb, w_in[:, :main].astype(BF16), BF16, tn=512, scale_cols=hd, scale=HEAD_DIM ** -0.5)
    wg = w_in[:, main:].reshape(-1, hk_n, 3 * group)
    wg = jnp.pad(wg, ((0, 0), (0, 0), (0, LANES - 3 * group))).reshape(-1, hk_n * LANES)
    gates = _matmul(xb, wg.astype(BF16), F32, tn=hk_n * LANES)
    cmp_kv = _nsa_compress(proj, cmp_pos, cmp_w1.astype(BF16), cmp_w2.astype(BF16), b, s)
    return _nsa_attention(proj, gates, cmp_kv, rel_bias, b, s)


def _mla_mixer(xb, w_down, q_norm, kv_norm, w_uq, w_ukv, b, s):
    hh = N_HEADS
    half = MLA_ROPE_DIM // 2
    lat = MLA_Q_LORA + MLA_KV_LORA
    swap = lambda t: jnp.concatenate([t[..., half:], t[..., :half]], axis=-1)
    w_down_x = jnp.concatenate([w_down, swap(w_down[:, lat:])], axis=1).astype(BF16)
    c = _matmul(xb, w_down_x, F32, tn=w_down_x.shape[1])

    inv = ROPE_BASE ** (-jnp.arange(0, MLA_ROPE_DIM, 2, dtype=F32) / MLA_ROPE_DIM)
    ang = jnp.arange(s, dtype=F32)[:, None] * inv[None, :]
    cos, sin = jnp.cos(ang), jnp.sin(ang)
    ktab = jnp.concatenate([cos, cos, -sin, sin], axis=1)
    qc_tab = jnp.concatenate([cos, cos, cos, cos], axis=1)
    qs_tab = jnp.concatenate([-sin, sin, -sin, sin], axis=1)
    cq, ckv, kr = _mla_prep(c, q_norm, kv_norm, ktab, s)

    wq = w_uq.reshape(MLA_Q_LORA, hh, MLA_NOPE_DIM + MLA_ROPE_DIM)
    wq_rope = wq[:, :, MLA_NOPE_DIM:]
    wq_x = jnp.concatenate([wq[:, :, :MLA_NOPE_DIM].reshape(MLA_Q_LORA, -1),
                            wq_rope.reshape(MLA_Q_LORA, -1),
                            swap(wq_rope).reshape(MLA_Q_LORA, -1)], axis=1).astype(BF16)
    scale = (MLA_NOPE_DIM + MLA_ROPE_DIM) ** -0.5
    qx = _matmul(cq, wq_x, BF16, tn=512, scale_cols=wq_x.shape[1], scale=scale)
    wkv = w_ukv.reshape(MLA_KV_LORA, hh, MLA_NOPE_DIM + MLA_V_DIM)
    wkv_x = jnp.concatenate([wkv[:, :, :MLA_NOPE_DIM].reshape(MLA_KV_LORA, -1),
                             wkv[:, :, MLA_NOPE_DIM:].reshape(MLA_KV_LORA, -1)], axis=1).astype(BF16)
    kv = _matmul(ckv, wkv_x, BF16, tn=512)
    return _mla_attention(qx, kv, kr, qc_tab, qs_tab, b, s)


def kernel(x, rel_bias, moba_w_qkv, moba_w_o, swa_w_qkv, swa_sinks, swa_w_o, nsa_w_in, nsa_cmp_pos,
           nsa_cmp_w1, nsa_cmp_w2, nsa_w_o, mla_w_down, mla_q_norm, mla_kv_norm, mla_w_uq, mla_w_ukv,
           mla_w_o, ln1_g, ln1_b, ffn_w_gate, ffn_w_up, ffn_w_down, ln2_g, ln2_b):
    b, s, d = x.shape
    depth = ln1_g.shape[0]
    n_mixers = 4
    xf = x.reshape(b * s, d).astype(F32)
    xb = xf.astype(BF16)
    for i in range(depth):
        kind, j = i % n_mixers, i // n_mixers
        if kind == 0:
            o = _moba_mixer(xb, moba_w_qkv[j], rel_bias, b, s)
            w_o = moba_w_o[j]
        elif kind == 1:
            o = _swa_mixer(xb, swa_w_qkv[j], swa_sinks[j], rel_bias, b, s)
            w_o = swa_w_o[j]
        elif kind == 2:
            o = _nsa_mixer(xb, nsa_w_in[j], nsa_cmp_pos[j], nsa_cmp_w1[j], nsa_cmp_w2[j], rel_bias, b, s)
            w_o = nsa_w_o[j]
        else:
            o = _mla_mixer(xb, mla_w_down[j], mla_q_norm[j], mla_kv_norm[j], mla_w_uq[j], mla_w_ukv[j], b, s)
            w_o = mla_w_o[j]
        xf, xb = _proj_ln(o, w_o.astype(BF16), xf, ln1_g[i], ln1_b[i])
        hmid = _ffn_up(xb, ffn_w_gate[i].astype(BF16), ffn_w_up[i].astype(BF16))
        xf, xb = _proj_ln(hmid, ffn_w_down[i].astype(BF16), xf, ln2_g[i], ln2_b[i])
    return xf.reshape(b, s, d).astype(x.dtype)
```

```python
import functools
import math

import jax
import jax.numpy as jnp
import numpy as np
from jax import lax
from jax.experimental import pallas as pl
from jax.experimental.pallas import tpu as pltpu

F32 = jnp.float32
BF16 = jnp.bfloat16

N_HEADS = 16
HEAD_DIM = 128
REL_BUCKETS = 32
REL_MAX_DIST = 128
MOBA_BLOCK = 256
MOBA_TOPK = 3
SWA_WINDOW = 128
SWA_KV_HEADS = 2
NSA_KV_HEADS = 4
NSA_CMP_LEN = 32
NSA_CMP_STRIDE = 16
NSA_CMP_HIDDEN = 256
NSA_SEL_BLOCK = 64
NSA_TOPN = 16
NSA_WINDOW = 512
MLA_Q_LORA = 512
MLA_KV_LORA = 512
MLA_NOPE_DIM = 128
MLA_ROPE_DIM = 64
MLA_V_DIM = 128
ROPE_BASE = 10000.0
DEPTH = 4
ALPHA = (2 * DEPTH) ** 0.25
LN_EPS = 1e-5
RMS_EPS = 1e-6

LANES = 128
TILE = 256
BIG = 2 * TILE
NEG = -1e30
VMEM_LIMIT = 56 * 1024 * 1024

_NT = (((1,), (1,)), ((), ()))


def _cparams(sem):
    return pltpu.CompilerParams(dimension_semantics=sem, vmem_limit_bytes=VMEM_LIMIT)


def _matmul_kernel(x_ref, w_ref, o_ref, *, scale_tiles, scale):
    acc = jnp.dot(x_ref[...], w_ref[...], preferred_element_type=F32)
    if scale_tiles:
        acc = acc * jnp.where(pl.program_id(1) < scale_tiles, scale, 1.0).astype(F32)
    o_ref[...] = acc.astype(o_ref.dtype)


def _matmul(x, w, out_dtype, *, tn, tm=1024, scale_cols=0, scale=1.0):
    m, k = x.shape
    n = w.shape[1]
    tm = min(tm, m)
    assert m % tm == 0 and n % tn == 0 and scale_cols % tn == 0
    kern = functools.partial(_matmul_kernel, scale_tiles=scale_cols // tn, scale=scale)
    return pl.pallas_call(
        kern,
        out_shape=jax.ShapeDtypeStruct((m, n), out_dtype),
        grid=(m // tm, n // tn),
        in_specs=[pl.BlockSpec((tm, k), lambda i, j: (i, 0)),
                  pl.BlockSpec((k, tn), lambda i, j: (0, j))],
        out_specs=pl.BlockSpec((tm, tn), lambda i, j: (i, j)),
        compiler_params=_cparams(("parallel", "arbitrary")),
        name="proj_matmul",
    )(x, w)


def _proj_ln_kernel(a_ref, w_ref, x_ref, g_ref, b_ref, o_ref, ob_ref, acc_ref):
    kk = pl.program_id(1)
    part = jnp.dot(a_ref[...], w_ref[...], preferred_element_type=F32)

    @pl.when(kk == 0)
    def _():
        acc_ref[...] = part

    @pl.when(kk > 0)
    def _():
        acc_ref[...] += part

    @pl.when(kk == pl.num_programs(1) - 1)
    def _():
        y = ALPHA * x_ref[...] + acc_ref[...]
        mu = jnp.mean(y, axis=-1, keepdims=True)
        yc = y - mu
        var = jnp.mean(yc * yc, axis=-1, keepdims=True)
        out = yc * lax.rsqrt(var + LN_EPS) * g_ref[...] + b_ref[...]
        o_ref[...] = out
        ob_ref[...] = out.astype(BF16)


MAX_TK = 2048


def _proj_ln(a, w, x, g, b, *, tm=512, tk=None):
    m, k = a.shape
    d = w.shape[1]
    tm = min(tm, m)
    if tk is None:
        tk = max(t for t in range(LANES, min(k, MAX_TK) + 1, LANES) if k % t == 0)
    assert m % tm == 0 and k % tk == 0
    return pl.pallas_call(
        _proj_ln_kernel,
        out_shape=(jax.ShapeDtypeStruct((m, d), F32), jax.ShapeDtypeStruct((m, d), BF16)),
        grid=(m // tm, k // tk),
        in_specs=[pl.BlockSpec((tm, tk), lambda i, j: (i, j)),
                  pl.BlockSpec((tk, d), lambda i, j: (j, 0)),
                  pl.BlockSpec((tm, d), lambda i, j: (i, 0)),
                  pl.BlockSpec((1, d), lambda i, j: (0, 0)),
                  pl.BlockSpec((1, d), lambda i, j: (0, 0))],
        out_specs=(pl.BlockSpec((tm, d), lambda i, j: (i, 0)),
                   pl.BlockSpec((tm, d), lambda i, j: (i, 0))),
        scratch_shapes=[pltpu.VMEM((tm, d), F32)],
        compiler_params=_cparams(("parallel", "arbitrary")),
        name="proj_deepnorm",
    )(a, w, x, g.reshape(1, d), b.reshape(1, d))


def _ffn_up_kernel(x_ref, wg_ref, wu_ref, o_ref):
    x = x_ref[...]
    gate = jnp.dot(x, wg_ref[...], preferred_element_type=F32)
    up = jnp.dot(x, wu_ref[...], preferred_element_type=F32)
    o_ref[...] = (gate * jax.nn.sigmoid(gate) * up).astype(o_ref.dtype)


def _ffn_up(x, wg, wu, *, tm=1024, tn=512):
    m, k = x.shape
    f = wg.shape[1]
    tm = min(tm, m)
    tn = min(tn, f)
    assert m % tm == 0 and f % tn == 0
    return pl.pallas_call(
        _ffn_up_kernel,
        out_shape=jax.ShapeDtypeStruct((m, f), BF16),
        grid=(m // tm, f // tn),
        in_specs=[pl.BlockSpec((tm, k), lambda i, j: (i, 0)),
                  pl.BlockSpec((k, tn), lambda i, j: (0, j)),
                  pl.BlockSpec((k, tn), lambda i, j: (0, j))],
        out_specs=pl.BlockSpec((tm, tn), lambda i, j: (i, j)),
        compiler_params=_cparams(("parallel", "arbitrary")),
        name="ffn_gate_up",
    )(x, wg, wu)


def _t5_bucket(dist):
    n = np.maximum(dist, 0)
    max_exact = REL_BUCKETS // 2
    nf = np.maximum(n, max_exact).astype(np.float32)
    large = max_exact + (np.log(nf / max_exact) / math.log(REL_MAX_DIST / max_exact)
                         * (REL_BUCKETS - max_exact)).astype(np.int32)
    return np.where(n < max_exact, n, np.minimum(large, REL_BUCKETS - 1)).astype(np.int32)


def _bias_lookup(rel_bias, dist, visible):
    bucket = _t5_bucket(dist).reshape(-1)
    onehot = (jnp.arange(REL_BUCKETS, dtype=jnp.int32)[:, None] == jnp.asarray(bucket)[None, :]).astype(F32)
    tab = jnp.dot(rel_bias.T.astype(F32), onehot, precision=lax.Precision.HIGHEST)
    tab = tab.reshape((rel_bias.shape[1],) + dist.shape)
    return jnp.where(jnp.asarray(visible)[None], tab, NEG)


def _bias_tiles(rel_bias, window=None):
    r = np.arange(TILE)[:, None]
    c = np.arange(TILE)[None, :]
    d0 = r - c
    d1 = TILE + r - c
    ok0 = d0 >= 0
    ok1 = np.ones_like(ok0)
    if window is not None:
        ok0 = ok0 & (d0 < window)
        ok1 = d1 < window
    t0 = _bias_lookup(rel_bias, d0, ok0)
    t1 = _bias_lookup(rel_bias, d1, ok1)
    far = rel_bias[int(_t5_bucket(np.asarray(2 * TILE)))].astype(F32)
    return t0, t1, far


def _flash_update(s, v, m_ref, l_ref, acc_ref):
    m_prev = m_ref[...]
    m_new = jnp.maximum(m_prev, jnp.max(s, axis=1, keepdims=True))
    alpha = jnp.exp(m_prev - m_new)
    p = jnp.exp(s - jnp.concatenate([m_new] * (s.shape[1] // LANES), axis=1))
    l_ref[...] = alpha * l_ref[...] + jnp.sum(p, axis=1, keepdims=True)
    acc_ref[...] = alpha * acc_ref[...] + jnp.dot(p.astype(BF16), v, preferred_element_type=F32)
    m_ref[...] = m_new


def _flash_reset(m_ref, l_ref, acc_ref):
    m_ref[...] = jnp.full(m_ref.shape, NEG, F32)
    l_ref[...] = jnp.zeros(l_ref.shape, F32)
    acc_ref[...] = jnp.zeros(acc_ref.shape, F32)


def _rows(ref, t, tile=TILE):
    return ref[pl.ds(pl.multiple_of(t * tile, tile), tile), :]


def _split3(x):
    hi = x.astype(BF16)
    r1 = x - hi.astype(F32)
    mid = r1.astype(BF16)
    lo = (r1 - mid.astype(F32)).astype(BF16)
    return hi, mid, lo


def _rank_before(val, lane, n):
    cnt = jnp.zeros(val.shape, jnp.int32)
    for m in range(n):
        vm = val[:, m:m + 1]
        beats = (vm > val) | ((vm == val) & (m < lane))
        cnt = cnt + beats.astype(jnp.int32)
    return cnt


def _moba_kernel(q_ref, k_ref, v_ref, tab_ref, ex_ref, o_ref, m_ref, l_ref, acc_ref, *, nblk):
    ci = pl.program_id(2)
    q = q_ref[...]

    kmean = jnp.concatenate(
        [jnp.mean(k_ref[n * TILE:(n + 1) * TILE, :].astype(F32), axis=0, keepdims=True)
         for n in range(nblk)]
        + [jnp.zeros((LANES - nblk, HEAD_DIM), F32)], axis=0)
    gate = sum(lax.dot_general(q, part, _NT, preferred_element_type=F32)
               for part in _split3(kmean))
    lane = lax.broadcasted_iota(jnp.int32, gate.shape, 1)
    qblk = 2 * ci + lax.broadcasted_iota(jnp.int32, gate.shape, 0) // TILE
    past = lane < qblk
    gate = jnp.where(past, gate, -jnp.inf)
    cnt = _rank_before(gate, lane, nblk)
    chosen = jnp.where((past & (cnt < MOBA_TOPK)) | (lane == qblk), 1.0, 0.0).astype(BF16)

    def chunk(c, carry):
        rows = pl.ds(pl.multiple_of(c * BIG, BIG), BIG)
        s = lax.dot_general(q, k_ref[rows, :], _NT, preferred_element_type=F32)
        hit = jnp.dot(chosen, ex_ref[c], preferred_element_type=F32)
        s = s + tab_ref[0, jnp.minimum(ci - c, 2)] + (hit - 1.0) * (-NEG)
        _flash_update(s, v_ref[rows, :], m_ref, l_ref, acc_ref)
        return carry

    _flash_reset(m_ref, l_ref, acc_ref)
    chunk(ci, 0)
    lax.fori_loop(0, ci, chunk, 0)
    o_ref[...] = (acc_ref[...] / l_ref[...]).astype(o_ref.dtype)


def _moba_attention(qkv, rel_bias, b, s):
    assert MOBA_BLOCK == TILE and s % BIG == 0
    nblk = s // TILE
    nq = s // BIG
    assert max(1, min(MOBA_TOPK, nblk - 1)) == MOBA_TOPK and nblk <= LANES
    hh = N_HEADS
    t0, t1, far = _bias_tiles(rel_bias)
    ff = jnp.broadcast_to(far[:, None, None], t0.shape)
    zz = jnp.zeros_like(t0)
    blk = lambda a, b_, c, d: jnp.concatenate([jnp.concatenate([a, b_], axis=2),
                                               jnp.concatenate([c, d], axis=2)], axis=1)
    tab = jnp.stack([blk(t0, zz, t1, t0), blk(ff, t1, ff, ff), blk(ff, ff, ff, ff)], axis=1)
    expand = jnp.asarray(np.arange(LANES)[None, :, None]
                         == (2 * np.arange(nq)[:, None, None] + np.arange(BIG)[None, None, :] // TILE), BF16)
    kern = functools.partial(_moba_kernel, nblk=nblk)
    return pl.pallas_call(
        kern,
        out_shape=jax.ShapeDtypeStruct((b * s, hh * HEAD_DIM), BF16),
        grid=(b, hh, nq),
        in_specs=[pl.BlockSpec((BIG, HEAD_DIM), lambda bi, h, i: (bi * nq + i, h)),
                  pl.BlockSpec((s, HEAD_DIM), lambda bi, h, i: (bi, hh + h)),
                  pl.BlockSpec((s, HEAD_DIM), lambda bi, h, i: (bi, 2 * hh + h)),
                  pl.BlockSpec((1, 3, BIG, BIG), lambda bi, h, i: (h, 0, 0, 0)),
                  pl.BlockSpec((nq, LANES, BIG), lambda bi, h, i: (0, 0, 0))],
        out_specs=pl.BlockSpec((BIG, HEAD_DIM), lambda bi, h, i: (bi * nq + i, h)),
        scratch_shapes=[pltpu.VMEM((BIG, LANES), F32), pltpu.VMEM((BIG, LANES), F32),
                        pltpu.VMEM((BIG, HEAD_DIM), F32)],
        compiler_params=_cparams(("parallel", "parallel", "arbitrary")),
        name="moba_attention",
    )(qkv, qkv, qkv, tab, expand)


def _swa_kernel(sink_ref, q_ref, k_ref, v_ref, t0_ref, t1_ref, o_ref, m_ref, l_ref, acc_ref, *, group):
    hk = pl.program_id(0)
    i = pl.program_id(2)
    q = jnp.concatenate([q_ref[:, g * HEAD_DIM:(g + 1) * HEAD_DIM] for g in range(group)], axis=0)

    for g in range(group):
        m_ref[g * TILE:(g + 1) * TILE, :] = jnp.full((TILE, LANES), sink_ref[hk * group + g], F32)
    l_ref[...] = jnp.ones(l_ref.shape, F32)
    acc_ref[...] = jnp.zeros(acc_ref.shape, F32)

    s = lax.dot_general(q, _rows(k_ref, i), _NT, preferred_element_type=F32)
    s = s + t0_ref[...].reshape(group * TILE, TILE)
    _flash_update(s, _rows(v_ref, i), m_ref, l_ref, acc_ref)

    @pl.when(i > 0)
    def _():
        s = lax.dot_general(q, _rows(k_ref, i - 1), _NT, preferred_element_type=F32)
        s = s + t1_ref[...].reshape(group * TILE, TILE)
        _flash_update(s, _rows(v_ref, i - 1), m_ref, l_ref, acc_ref)

    out = acc_ref[...] / l_ref[...]
    for g in range(group):
        o_ref[:, g * HEAD_DIM:(g + 1) * HEAD_DIM] = out[g * TILE:(g + 1) * TILE].astype(o_ref.dtype)


def _swa_attention(qkv, sinks, rel_bias, b, s):
    assert SWA_WINDOW <= TILE and s % TILE == 0
    nq = s // TILE
    hk_n = SWA_KV_HEADS
    group = N_HEADS // hk_n
    t0, t1, _ = _bias_tiles(rel_bias, window=SWA_WINDOW)
    kern = functools.partial(_swa_kernel, group=group)
    gw = group * HEAD_DIM
    return pl.pallas_call(
        kern,
        out_shape=jax.ShapeDtypeStruct((b * s, N_HEADS * HEAD_DIM), BF16),
        grid=(hk_n, b, nq),
        in_specs=[pl.BlockSpec(memory_space=pltpu.SMEM),
                  pl.BlockSpec((TILE, gw), lambda hk, bi, i: (bi * nq + i, hk)),
                  pl.BlockSpec((s, HEAD_DIM), lambda hk, bi, i: (bi, N_HEADS + hk)),
                  pl.BlockSpec((s, HEAD_DIM), lambda hk, bi, i: (bi, N_HEADS + hk_n + hk)),
                  pl.BlockSpec((group, TILE, TILE), lambda hk, bi, i: (hk, 0, 0)),
                  pl.BlockSpec((group, TILE, TILE), lambda hk, bi, i: (hk, 0, 0))],
        out_specs=pl.BlockSpec((TILE, gw), lambda hk, bi, i: (bi * nq + i, hk)),
        scratch_shapes=[pltpu.VMEM((group * TILE, LANES), F32), pltpu.VMEM((group * TILE, LANES), F32),
                        pltpu.VMEM((group * TILE, HEAD_DIM), F32)],
        compiler_params=_cparams(("parallel", "parallel", "arbitrary")),
        name="swa_attention",
    )(sinks.astype(F32), qkv, qkv, qkv, t0, t1)


def _gelu_tanh(x):
    return 0.5 * x * (1.0 + jnp.tanh(math.sqrt(2.0 / math.pi) * (x + 0.044715 * (x * x * x))))


def _nsa_compress_kernel(t_ref, pe_ref, w1_ref, w2_ref, o_ref, tf_ref, *, ncmp_pad):
    stride = NSA_CMP_STRIDE
    half = stride * HEAD_DIM
    pe = pe_ref[0]
    tf_ref[...] = t_ref[...].astype(F32)
    xa, xb = [], []
    for r in range(stride):
        x = tf_ref[pl.ds(r, ncmp_pad, stride=stride), :]
        xa.append((x + pe[r:r + 1, :]).astype(BF16))
        xb.append((x + pe[stride + r:stride + r + 1, :]).astype(BF16))
    a = jnp.dot(jnp.concatenate(xa, axis=1), w1_ref[0, :half, :], preferred_element_type=F32)
    bm = jnp.dot(jnp.concatenate(xb, axis=1), w1_ref[0, half:, :], preferred_element_type=F32)
    hid = _gelu_tanh(a + pltpu.roll(bm, ncmp_pad - 1, 0))
    o_ref[0, 0, 0] = jnp.dot(hid.astype(BF16), w2_ref[0], preferred_element_type=F32).astype(o_ref.dtype)


def _nsa_compress(proj, pe, w1, w2, b, s):
    assert NSA_CMP_LEN == 2 * NSA_CMP_STRIDE and s % NSA_CMP_STRIDE == 0
    ncp = s // NSA_CMP_STRIDE
    hk_n = NSA_KV_HEADS
    kern = functools.partial(_nsa_compress_kernel, ncmp_pad=ncp)
    return pl.pallas_call(
        kern,
        out_shape=jax.ShapeDtypeStruct((2, b, hk_n, ncp, HEAD_DIM), BF16),
        grid=(2, b, hk_n),
        in_specs=[pl.BlockSpec((s, HEAD_DIM), lambda kv, bi, hk: (bi, N_HEADS + kv * hk_n + hk)),
                  pl.BlockSpec((1, NSA_CMP_LEN, HEAD_DIM), lambda kv, bi, hk: (kv, 0, 0)),
                  pl.BlockSpec((1, NSA_CMP_LEN * HEAD_DIM, NSA_CMP_HIDDEN), lambda kv, bi, hk: (kv, 0, 0)),
                  pl.BlockSpec((1, NSA_CMP_HIDDEN, HEAD_DIM), lambda kv, bi, hk: (kv, 0, 0))],
        out_specs=pl.BlockSpec((1, 1, 1, ncp, HEAD_DIM), lambda kv, bi, hk: (kv, bi, hk, 0, 0)),
        scratch_shapes=[pltpu.VMEM((s, HEAD_DIM), F32)],
        compiler_params=_cparams(("parallel", "parallel", "arbitrary")),
        name="nsa_compress",
    )(proj, pe.astype(F32), w1, w2)


def _nsa_kernel(far_ref, q_ref, kc_ref, vc_ref, cb_ref, ksl_ref, vsl_ref, kw_ref, vw_ref, gate_ref,
                t0_ref, t1_ref, tw_ref, ov_ref, ex_ref, o_ref,
                m_ref, l_ref, acc_ref, sel_ref, oslc_ref, *, group, nq):
    hk = pl.program_id(1)
    i = pl.program_id(2)
    rows = group * TILE
    q = jnp.concatenate([q_ref[:, g * HEAD_DIM:(g + 1) * HEAD_DIM] for g in range(group)], axis=0)

    cb = cb_ref[...].reshape(rows, LANES)
    sc = lax.dot_general(q, kc_ref[0, 0, 0], _NT, preferred_element_type=F32) + cb
    valid = cb > 0.5 * NEG
    mc = jnp.max(sc, axis=1, keepdims=True)
    pc = jnp.where(valid, jnp.exp(sc - mc), 0.0)
    pc = pc / jnp.maximum(jnp.sum(pc, axis=1, keepdims=True), 1e-30)
    o_cmp = jnp.dot(pc.astype(BF16), vc_ref[0, 0, 0], preferred_element_type=F32)

    psum = pc[0:TILE]
    for g in range(1, group):
        psum = psum + pc[g * TILE:(g + 1) * TILE]
    imp = sum(jnp.dot(part, ov_ref[...], preferred_element_type=F32) for part in _split3(psum))
    lane = lax.broadcasted_iota(jnp.int32, imp.shape, 1)
    qblk = (i * TILE + lax.broadcasted_iota(jnp.int32, imp.shape, 0)) // NSA_SEL_BLOCK
    forced = (lane == 0) | (lane == qblk) | (lane == qblk - 1)
    imp = jnp.where(lane > qblk, -jnp.inf, jnp.where(forced, jnp.inf, imp))
    cnt = _rank_before(imp, lane, nq * (TILE // NSA_SEL_BLOCK))
    chosen = jnp.where((lane <= qblk) & (cnt < NSA_TOPN), 1.0, 0.0).astype(BF16)
    for t in range(nq):
        @pl.when(t <= i)
        def _(t=t):
            hit = jnp.dot(chosen, ex_ref[:, t * TILE:(t + 1) * TILE], preferred_element_type=F32)
            sel_ref[t] = (hit - 1.0) * (-NEG)

    t0 = t0_ref[...].reshape(rows, TILE)

    def masked(s, t, with_far=False):
        sel = sel_ref[t]
        parts = []
        for g in range(group):
            add = sel + far_ref[hk * group + g] if with_far else sel
            parts.append(s[g * TILE:(g + 1) * TILE] + add)
        return jnp.concatenate(parts, axis=0)

    _flash_reset(m_ref, l_ref, acc_ref)
    s = lax.dot_general(q, _rows(ksl_ref, i), _NT, preferred_element_type=F32) + t0
    _flash_update(masked(s, i), _rows(vsl_ref, i), m_ref, l_ref, acc_ref)

    @pl.when(i > 0)
    def _():
        s = lax.dot_general(q, _rows(ksl_ref, i - 1), _NT, preferred_element_type=F32)
        s = s + t1_ref[...].reshape(rows, TILE)
        _flash_update(masked(s, i - 1), _rows(vsl_ref, i - 1), m_ref, l_ref, acc_ref)

    def far_tile(t, carry):
        s = lax.dot_general(q, _rows(ksl_ref, t), _NT, preferred_element_type=F32)
        _flash_update(masked(s, t, with_far=True), _rows(vsl_ref, t), m_ref, l_ref, acc_ref)
        return carry

    lax.fori_loop(0, jnp.maximum(i - 1, 0), far_tile, 0)
    oslc_ref[...] = acc_ref[...] / l_ref[...]

    _flash_reset(m_ref, l_ref, acc_ref)
    s = lax.dot_general(q, _rows(kw_ref, i), _NT, preferred_element_type=F32) + t0
    _flash_update(s, _rows(vw_ref, i), m_ref, l_ref, acc_ref)

    @pl.when(i > 0)
    def _():
        s = lax.dot_general(q, _rows(kw_ref, i - 1), _NT, preferred_element_type=F32)
        s = s + t1_ref[...].reshape(rows, TILE)
        _flash_update(s, _rows(vw_ref, i - 1), m_ref, l_ref, acc_ref)

    @pl.when(i > 1)
    def _():
        s = lax.dot_general(q, _rows(kw_ref, i - 2), _NT, preferred_element_type=F32)
        s = s + tw_ref[...].reshape(rows, TILE)
        _flash_update(s, _rows(vw_ref, i - 2), m_ref, l_ref, acc_ref)

    o_win = acc_ref[...] / l_ref[...]
    o_slc = oslc_ref[...]
    gates = jax.nn.sigmoid(gate_ref[...])
    for g in range(group):
        sl = slice(g * TILE, (g + 1) * TILE)
        o = (gates[:, 3 * g:3 * g + 1] * o_cmp[sl] + gates[:, 3 * g + 1:3 * g + 2] * o_slc[sl]
             + gates[:, 3 * g + 2:3 * g + 3] * o_win[sl])
        o_ref[:, g * HEAD_DIM:(g + 1) * HEAD_DIM] = o.astype(o_ref.dtype)


def _nsa_attention(proj, gates, cmp_kv, rel_bias, b, s):
    assert s % TILE == 0 and NSA_WINDOW == 2 * TILE and TILE % NSA_SEL_BLOCK == 0
    nq = s // TILE
    hk_n = NSA_KV_HEADS
    group = N_HEADS // hk_n
    gw = group * HEAD_DIM
    ncp = s // NSA_CMP_STRIDE
    n_cmp = (s - NSA_CMP_LEN) // NSA_CMP_STRIDE + 1
    n_sel = s // NSA_SEL_BLOCK
    assert ncp == LANES and n_sel <= LANES and min(NSA_TOPN, n_sel) == NSA_TOPN
    t0, t1, far = _bias_tiles(rel_bias)
    r = np.arange(TILE)[:, None]
    c = np.arange(TILE)[None, :]
    tw = jnp.where(jnp.asarray(c > r)[None], far[:, None, None], NEG)
    pos = np.arange(s)[:, None]
    cidx = np.arange(ncp)[None, :]
    cdist = pos - (cidx * NSA_CMP_STRIDE + NSA_CMP_LEN - 1)
    cmp_bias = _bias_lookup(rel_bias, cdist, (cdist >= 0) & (cidx < n_cmp))
    ci = np.arange(ncp)[:, None] * NSA_CMP_STRIDE
    sj = np.arange(LANES)[None, :] * NSA_SEL_BLOCK
    overlap = jnp.asarray((ci < sj + NSA_SEL_BLOCK) & (ci + NSA_CMP_LEN > sj)
                          & (np.arange(ncp)[:, None] < n_cmp) & (np.arange(LANES)[None, :] < n_sel), BF16)
    expand = jnp.asarray(np.arange(LANES)[:, None] == (np.arange(s)[None, :] // NSA_SEL_BLOCK), BF16)

    kern = functools.partial(_nsa_kernel, group=group, nq=nq)
    kv_spec = lambda off: pl.BlockSpec((s, HEAD_DIM), lambda bi, hk, i: (bi, N_HEADS + off * hk_n + hk))
    cmp_spec = lambda kv: pl.BlockSpec((1, 1, 1, ncp, HEAD_DIM), lambda bi, hk, i: (kv, bi, hk, 0, 0))
    tile_spec = pl.BlockSpec((group, TILE, TILE), lambda bi, hk, i: (hk, 0, 0))
    return pl.pallas_call(
        kern,
        out_shape=jax.ShapeDtypeStruct((b * s, N_HEADS * HEAD_DIM), BF16),
        grid=(b, hk_n, nq),
        in_specs=[pl.BlockSpec(memory_space=pltpu.SMEM),
                  pl.BlockSpec((TILE, gw), lambda bi, hk, i: (bi * nq + i, hk)),
                  cmp_spec(0), cmp_spec(1),
                  pl.BlockSpec((group, TILE, ncp), lambda bi, hk, i: (hk, i, 0)),
                  kv_spec(2), kv_spec(3), kv_spec(4), kv_spec(5),
                  pl.BlockSpec((TILE, LANES), lambda bi, hk, i: (bi * nq + i, hk)),
                  tile_spec, tile_spec, tile_spec,
                  pl.BlockSpec((ncp, LANES), lambda bi, hk, i: (0, 0)),
                  pl.BlockSpec((LANES, s), lambda bi, hk, i: (0, 0))],
        out_specs=pl.BlockSpec((TILE, gw), lambda bi, hk, i: (bi * nq + i, hk)),
        scratch_shapes=[pltpu.VMEM((group * TILE, LANES), F32), pltpu.VMEM((group * TILE, LANES), F32),
                        pltpu.VMEM((group * TILE, HEAD_DIM), F32),
                        pltpu.VMEM((nq, TILE, TILE), F32),
                        pltpu.VMEM((group * TILE, HEAD_DIM), F32)],
        compiler_params=_cparams(("parallel", "parallel", "arbitrary")),
        name="nsa_attention",
    )(far, proj, cmp_kv, cmp_kv, cmp_bias, proj, proj, proj, proj, gates, t0, t1, tw, overlap, expand)


def _rms(x, g):
    return x * lax.rsqrt(jnp.mean(x * x, axis=-1, keepdims=True) + RMS_EPS) * g


def _mla_prep_kernel(c_ref, qg_ref, kg_ref, tab_ref, cq_ref, ckv_ref, kr_ref):
    c = c_ref[...]
    cq_ref[...] = _rms(c[:, :MLA_Q_LORA], qg_ref[...]).astype(BF16)
    ckv_ref[...] = _rms(c[:, MLA_Q_LORA:MLA_Q_LORA + MLA_KV_LORA], kg_ref[...]).astype(BF16)
    y = c[:, MLA_Q_LORA + MLA_KV_LORA:] * tab_ref[...]
    y = y + pltpu.roll(y, MLA_ROPE_DIM, 1)
    lane = lax.broadcasted_iota(jnp.int32, y.shape, 1)
    kr_ref[0] = jnp.where(lane < MLA_ROPE_DIM, y, 0.0).astype(BF16)
    kr_ref[1] = jnp.where(lane >= MLA_ROPE_DIM, y, 0.0).astype(BF16)


def _mla_prep(c, q_norm, kv_norm, ktab, s, *, tm=512):
    m, w = c.shape
    tm = min(tm, s)
    assert 2 * MLA_ROPE_DIM == LANES and w == MLA_Q_LORA + MLA_KV_LORA + LANES and s % tm == 0
    ns = s // tm
    return pl.pallas_call(
        _mla_prep_kernel,
        out_shape=(jax.ShapeDtypeStruct((m, MLA_Q_LORA), BF16),
                   jax.ShapeDtypeStruct((m, MLA_KV_LORA), BF16),
                   jax.ShapeDtypeStruct((2, m, LANES), BF16)),
        grid=(m // tm,),
        in_specs=[pl.BlockSpec((tm, w), lambda i: (i, 0)),
                  pl.BlockSpec((1, MLA_Q_LORA), lambda i: (0, 0)),
                  pl.BlockSpec((1, MLA_KV_LORA), lambda i: (0, 0)),
                  pl.BlockSpec((tm, LANES), lambda i: (i % ns, 0))],
        out_specs=(pl.BlockSpec((tm, MLA_Q_LORA), lambda i: (i, 0)),
                   pl.BlockSpec((tm, MLA_KV_LORA), lambda i: (i, 0)),
                   pl.BlockSpec((2, tm, LANES), lambda i: (0, i, 0))),
        compiler_params=_cparams(("parallel",)),
        name="mla_prep",
    )(c, q_norm.reshape(1, -1).astype(F32), kv_norm.reshape(1, -1).astype(F32), ktab)


def _mla_kernel(qn_ref, qr_ref, qs_ref, ct_ref, st_ref, kn_ref, kr_ref, v_ref, o_ref,
                m_ref, l_ref, acc_ref, kcat_ref):
    i = pl.program_id(2)

    @pl.when(i == 0)
    def _():
        kcat_ref[:, :HEAD_DIM] = kn_ref[...]
        kcat_ref[:, HEAD_DIM:] = kr_ref[0]

    qr = qr_ref[...].astype(F32) * ct_ref[...] + qs_ref[...].astype(F32) * st_ref[...]
    q = jnp.concatenate([qn_ref[...], qr.astype(BF16)], axis=1)

    _flash_reset(m_ref, l_ref, acc_ref)
    s = lax.dot_general(q, _rows(kcat_ref, i, BIG), _NT, preferred_element_type=F32)
    r = lax.broadcasted_iota(jnp.int32, s.shape, 0)
    c = lax.broadcasted_iota(jnp.int32, s.shape, 1)
    _flash_update(jnp.where(r >= c, s, NEG), _rows(v_ref, i, BIG), m_ref, l_ref, acc_ref)

    def past_tile(t, carry):
        s = lax.dot_general(q, _rows(kcat_ref, t, BIG), _NT, preferred_element_type=F32)
        _flash_update(s, _rows(v_ref, t, BIG), m_ref, l_ref, acc_ref)
        return carry

    lax.fori_loop(0, i, past_tile, 0)
    o_ref[...] = (acc_ref[...] / l_ref[...]).astype(o_ref.dtype)


def _mla_attention(qx, kv, kr, qc_tab, qs_tab, b, s):
    assert s % BIG == 0
    nq = s // BIG
    hh = N_HEADS
    return pl.pallas_call(
        _mla_kernel,
        out_shape=jax.ShapeDtypeStruct((b * s, hh * MLA_V_DIM), BF16),
        grid=(b, hh, nq),
        in_specs=[pl.BlockSpec((BIG, LANES), lambda bi, h, i: (bi * nq + i, h)),
                  pl.BlockSpec((BIG, LANES), lambda bi, h, i: (bi * nq + i, hh + h // 2)),
                  pl.BlockSpec((BIG, LANES), lambda bi, h, i: (bi * nq + i, hh + hh // 2 + h // 2)),
                  pl.BlockSpec((BIG, LANES), lambda bi, h, i: (i, 0)),
                  pl.BlockSpec((BIG, LANES), lambda bi, h, i: (i, 0)),
                  pl.BlockSpec((s, LANES), lambda bi, h, i: (bi, h)),
                  pl.BlockSpec((1, s, LANES), lambda bi, h, i: (h % 2, bi, 0)),
                  pl.BlockSpec((s, LANES), lambda bi, h, i: (bi, hh + h))],
        out_specs=pl.BlockSpec((BIG, LANES), lambda bi, h, i: (bi * nq + i, h)),
        scratch_shapes=[pltpu.VMEM((BIG, LANES), F32), pltpu.VMEM((BIG, LANES), F32),
                        pltpu.VMEM((BIG, MLA_V_DIM), F32), pltpu.VMEM((s, 2 * LANES), BF16)],
        compiler_params=_cparams(("parallel", "parallel", "arbitrary")),
        name="mla_attention",
    )(qx, qx, qx, qc_tab, qs_tab, kv, kr, kv)


def _moba_mixer(xb, w_qkv, rel_bias, b, s):
    hd = N_HEADS * HEAD_DIM
    qkv = _matmul(xb, w_qkv.astype(BF16), BF16, tn=512, scale_cols=hd, scale=HEAD_DIM ** -0.5)
    return _moba_attention(qkv, rel_bias, b, s)


def _swa_mixer(xb, w_qkv, sinks, rel_bias, b, s):
    hd = N_HEADS * HEAD_DIM
    qkv = _matmul(xb, w_qkv.astype(BF16), BF16, tn=512, scale_cols=hd, scale=HEAD_DIM ** -0.5)
    return _swa_attention(qkv, sinks, rel_bias, b, s)


def _nsa_mixer(xb, w_in, cmp_pos, cmp_w1, cmp_w2, rel_bias, b, s):
    hd = N_HEADS * HEAD_DIM
    hk_n = NSA_KV_HEADS
    group = N_HEADS // hk_n
    main = hd + 6 * hk_n * HEAD_DIM
    proj = _matmul(xb, w_in[:, :main].astype(BF16), BF16, tn=512, scale_cols=hd, scale=HEAD_DIM ** -0.5)
    wg = w_in[:, main:].reshape(-1, hk_n, 3 * group)
    wg = jnp.pad(wg, ((0, 0), (0, 0), (0, LANES - 3 * group))).reshape(-1, hk_n * LANES)
    gates = _matmul(xb, wg.astype(BF16), F32, tn=hk_n * LANES)
    cmp_kv = _nsa_compress(proj, cmp_pos, cmp_w1.astype(BF16), cmp_w2.astype(BF16), b, s)
    return _nsa_attention(proj, gates, cmp_kv, rel_bias, b, s)


def _mla_mixer(xb, w_down, q_norm, kv_norm, w_uq, w_ukv, b, s):
    hh = N_HEADS
    half = MLA_ROPE_DIM // 2
    lat = MLA_Q_LORA + MLA_KV_LORA
    swap = lambda t: jnp.concatenate([t[..., half:], t[..., :half]], axis=-1)
    w_down_x = jnp.concatenate([w_down, swap(w_down[:, lat:])], axis=1).astype(BF16)
    c = _matmul(xb, w_down_x, F32, tn=w_down_x.shape[1])

    inv = ROPE_BASE ** (-jnp.arange(0, MLA_ROPE_DIM, 2, dtype=F32) / MLA_ROPE_DIM)
    ang = jnp.arange(s, dtype=F32)[:, None] * inv[None, :]
    cos, sin = jnp.cos(ang), jnp.sin(ang)
    ktab = jnp.concatenate([cos, cos, -sin, sin], axis=1)
    qc_tab = jnp.concatenate([cos, cos, cos, cos], axis=1)
    qs_tab = jnp.concatenate([-sin, sin, -sin, sin], axis=1)
    cq, ckv, kr = _mla_prep(c, q_norm, kv_norm, ktab, s)

    wq = w_uq.reshape(MLA_Q_LORA, hh, MLA_NOPE_DIM + MLA_ROPE_DIM)
    wq_rope = wq[:, :, MLA_NOPE_DIM:]
    wq_x = jnp.concatenate([wq[:, :, :MLA_NOPE_DIM].reshape(MLA_Q_LORA, -1),
                            wq_rope.reshape(MLA_Q_LORA, -1),
                            swap(wq_rope).reshape(MLA_Q_LORA, -1)], axis=1).astype(BF16)
    scale = (MLA_NOPE_DIM + MLA_ROPE_DIM) ** -0.5
    qx = _matmul(cq, wq_x, BF16, tn=512, scale_cols=wq_x.shape[1], scale=scale)
    wkv = w_ukv.reshape(MLA_KV_LORA, hh, MLA_NOPE_DIM + MLA_V_DIM)
    wkv_x = jnp.concatenate([wkv[:, :, :MLA_NOPE_DIM].reshape(MLA_KV_LORA, -1),
                             wkv[:, :, MLA_NOPE_DIM:].reshape(MLA_KV_LORA, -1)], axis=1).astype(BF16)
    kv = _matmul(ckv, wkv_x, BF16, tn=512)
    return _mla_attention(qx, kv, kr, qc_tab, qs_tab, b, s)


def kernel(x, rel_bias, moba_w_qkv, moba_w_o, swa_w_qkv, swa_sinks, swa_w_o, nsa_w_in, nsa_cmp_pos,
           nsa_cmp_w1, nsa_cmp_w2, nsa_w_o, mla_w_down, mla_q_norm, mla_kv_norm, mla_w_uq, mla_w_ukv,
           mla_w_o, ln1_g, ln1_b, ffn_w_gate, ffn_w_up, ffn_w_down, ln2_g, ln2_b):
    b, s, d = x.shape
    depth = ln1_g.shape[0]
    n_mixers = 4
    xf = x.reshape(b * s, d).astype(F32)
    xb = xf.astype(BF16)
    for i in range(depth):
        kind, j = i % n_mixers, i // n_mixers
        if kind == 0:
            o = _moba_mixer(xb, moba_w_qkv[j], rel_bias, b, s)
            w_o = moba_w_o[j]
        elif kind == 1:
            o = _swa_mixer(xb, swa_w_qkv[j], swa_sinks[j], rel_bias, b, s)
            w_o = swa_w_o[j]
        elif kind == 2:
            o = _nsa_mixer(xb, nsa_w_in[j], nsa_cmp_pos[j], nsa_cmp_w1[j], nsa_cmp_w2[j], rel_bias, b, s)
            w_o = nsa_w_o[j]
        else:
            o = _mla_mixer(xb, mla_w_down[j], mla_q_norm[j], mla_kv_norm[j], mla_w_uq[j], mla_w_ukv[j], b, s)
            w_o = mla_w_o[j]
        xf, xb = _proj_ln(o, w_o.astype(BF16), xf, ln1_g[i], ln1_b[i])
        hmid = _ffn_up(xb, ffn_w_gate[i].astype(BF16), ffn_w_up[i].astype(BF16))
        xf, xb = _proj_ln(hmid, ffn_w_down[i].astype(BF16), xf, ln2_g[i], ln2_b[i])
    return xf.reshape(b, s, d).astype(x.dtype)
```

```python
import functools
import math

import jax
import jax.numpy as jnp
import numpy as np
from jax import lax
from jax.experimental import pallas as pl
from jax.experimental.pallas import tpu as pltpu

F32 = jnp.float32
BF16 = jnp.bfloat16

N_HEADS = 16
HEAD_DIM = 128
REL_BUCKETS = 32
REL_MAX_DIST = 128
MOBA_BLOCK = 256
MOBA_TOPK = 3
SWA_WINDOW = 128
SWA_KV_HEADS = 2
NSA_KV_HEADS = 4
NSA_CMP_LEN = 32
NSA_CMP_STRIDE = 16
NSA_CMP_HIDDEN = 256
NSA_SEL_BLOCK = 64
NSA_TOPN = 16
NSA_WINDOW = 512
MLA_Q_LORA = 512
MLA_KV_LORA = 512
MLA_NOPE_DIM = 128
MLA_ROPE_DIM = 64
MLA_V_DIM = 128
ROPE_BASE = 10000.0
DEPTH = 4
ALPHA = (2 * DEPTH) ** 0.25
LN_EPS = 1e-5
RMS_EPS = 1e-6

LANES = 128
TILE = 256
BIG = 2 * TILE
NEG = -1e30
VMEM_LIMIT = 56 * 1024 * 1024

_NT = (((1,), (1,)), ((), ()))


def _cparams(sem):
    return pltpu.CompilerParams(dimension_semantics=sem, vmem_limit_bytes=VMEM_LIMIT)


def _matmul_kernel(x_ref, w_ref, o_ref, *, scale_tiles, scale):
    acc = jnp.dot(x_ref[...], w_ref[...], preferred_element_type=F32)
    if scale_tiles:
        acc = acc * jnp.where(pl.program_id(1) < scale_tiles, scale, 1.0).astype(F32)
    o_ref[...] = acc.astype(o_ref.dtype)


CAST_BLOCK_BYTES = 6 * 1024 * 1024


def _cast_kernel(x_ref, o_ref):
    o_ref[...] = x_ref[...].astype(o_ref.dtype)


def _to_bf16(w):
    w2 = w.reshape(-1, w.shape[-1])
    rows, cols = w2.shape
    tr = rows
    while tr % 2 == 0 and tr > 16 and tr * cols * 4 > CAST_BLOCK_BYTES:
        tr //= 2
    out = pl.pallas_call(
        _cast_kernel,
        out_shape=jax.ShapeDtypeStruct((rows, cols), BF16),
        grid=(rows // tr,),
        in_specs=[pl.BlockSpec((tr, cols), lambda i: (i, 0))],
        out_specs=pl.BlockSpec((tr, cols), lambda i: (i, 0)),
        compiler_params=_cparams(("parallel",)),
        name="cast_bf16",
    )(w2)
    return out.reshape(w.shape)


def _wspec(w, rows, cols, index):
    if isinstance(w, tuple):
        layer = w[1]
        return w[0], pl.BlockSpec((None, rows, cols), lambda i, j: (layer,) + index(i, j))
    return w, pl.BlockSpec((rows, cols), index)


def _wshape(w):
    return w[0].shape[1:] if isinstance(w, tuple) else w.shape


def _matmul(x, w, out_dtype, *, tn, tm=1024, scale_cols=0, scale=1.0, n_out=None):
    m, k = x.shape
    n = _wshape(w)[1] if n_out is None else n_out
    w, w_spec = _wspec(w, k, tn, lambda i, j: (0, j))
    tm = min(tm, m)
    assert m % tm == 0 and n % tn == 0 and scale_cols % tn == 0
    kern = functools.partial(_matmul_kernel, scale_tiles=scale_cols // tn, scale=scale)
    return pl.pallas_call(
        kern,
        out_shape=jax.ShapeDtypeStruct((m, n), out_dtype),
        grid=(m // tm, n // tn),
        in_specs=[pl.BlockSpec((tm, k), lambda i, j: (i, 0)), w_spec],
        out_specs=pl.BlockSpec((tm, tn), lambda i, j: (i, j)),
        compiler_params=_cparams(("parallel", "arbitrary")),
        name="proj_matmul",
    )(x, w)


def _proj_ln_kernel(a_ref, w_ref, x_ref, g_ref, b_ref, o_ref, ob_ref, acc_ref):
    kk = pl.program_id(1)
    part = jnp.dot(a_ref[...], w_ref[...], preferred_element_type=F32)

    @pl.when(kk == 0)
    def _():
        acc_ref[...] = part

    @pl.when(kk > 0)
    def _():
        acc_ref[...] += part

    @pl.when(kk == pl.num_programs(1) - 1)
    def _():
        y = ALPHA * x_ref[...] + acc_ref[...]
        mu = jnp.mean(y, axis=-1, keepdims=True)
        yc = y - mu
        var = jnp.mean(yc * yc, axis=-1, keepdims=True)
        out = yc * lax.rsqrt(var + LN_EPS) * g_ref[...] + b_ref[...]
        o_ref[...] = out
        ob_ref[...] = out.astype(BF16)


MAX_TK = 2048


def _proj_ln(a, w, x, g, b, *, tm=512, tk=None):
    m, k = a.shape
    d = _wshape(w)[1]
    tm = min(tm, m)
    if tk is None:
        tk = max(t for t in range(LANES, min(k, MAX_TK) + 1, LANES) if k % t == 0)
    assert m % tm == 0 and k % tk == 0
    w, w_spec = _wspec(w, tk, d, lambda i, j: (j, 0))
    return pl.pallas_call(
        _proj_ln_kernel,
        out_shape=(jax.ShapeDtypeStruct((m, d), F32), jax.ShapeDtypeStruct((m, d), BF16)),
        grid=(m // tm, k // tk),
        in_specs=[pl.BlockSpec((tm, tk), lambda i, j: (i, j)),
                  w_spec,
                  pl.BlockSpec((tm, d), lambda i, j: (i, 0)),
                  pl.BlockSpec((1, d), lambda i, j: (0, 0)),
                  pl.BlockSpec((1, d), lambda i, j: (0, 0))],
        out_specs=(pl.BlockSpec((tm, d), lambda i, j: (i, 0)),
                   pl.BlockSpec((tm, d), lambda i, j: (i, 0))),
        scratch_shapes=[pltpu.VMEM((tm, d), F32)],
        compiler_params=_cparams(("parallel", "arbitrary")),
        name="proj_deepnorm",
    )(a, w, x, g.reshape(1, d), b.reshape(1, d))


def _ffn_up_kernel(x_ref, wg_ref, wu_ref, o_ref):
    x = x_ref[...]
    gate = jnp.dot(x, wg_ref[...], preferred_element_type=F32)
    up = jnp.dot(x, wu_ref[...], preferred_element_type=F32)
    o_ref[...] = (gate * jax.nn.sigmoid(gate) * up).astype(o_ref.dtype)


def _ffn_up(x, wg, wu, *, tm=1024, tn=512):
    m, k = x.shape
    f = _wshape(wg)[1]
    tm = min(tm, m)
    tn = min(tn, f)
    assert m % tm == 0 and f % tn == 0
    wg, wg_spec = _wspec(wg, k, tn, lambda i, j: (0, j))
    wu, wu_spec = _wspec(wu, k, tn, lambda i, j: (0, j))
    return pl.pallas_call(
        _ffn_up_kernel,
        out_shape=jax.ShapeDtypeStruct((m, f), BF16),
        grid=(m // tm, f // tn),
        in_specs=[pl.BlockSpec((tm, k), lambda i, j: (i, 0)), wg_spec, wu_spec],
        out_specs=pl.BlockSpec((tm, tn), lambda i, j: (i, j)),
        compiler_params=_cparams(("parallel", "arbitrary")),
        name="ffn_gate_up",
    )(x, wg, wu)


def _t5_bucket(dist):
    n = np.maximum(dist, 0)
    max_exact = REL_BUCKETS // 2
    nf = np.maximum(n, max_exact).astype(np.float32)
    large = max_exact + (np.log(nf / max_exact) / math.log(REL_MAX_DIST / max_exact)
                         * (REL_BUCKETS - max_exact)).astype(np.int32)
    return np.where(n < max_exact, n, np.minimum(large, REL_BUCKETS - 1)).astype(np.int32)


def _bias_lookup(rel_bias, dist, visible):
    bucket = _t5_bucket(dist).reshape(-1)
    onehot = (jnp.arange(REL_BUCKETS, dtype=jnp.int32)[:, None] == jnp.asarray(bucket)[None, :]).astype(F32)
    tab = jnp.dot(rel_bias.T.astype(F32), onehot, precision=lax.Precision.HIGHEST)
    tab = tab.reshape((rel_bias.shape[1],) + dist.shape)
    return jnp.where(jnp.asarray(visible)[None], tab, NEG)


def _bias_tiles(rel_bias, window=None):
    r = np.arange(TILE)[:, None]
    c = np.arange(TILE)[None, :]
    d0 = r - c
    d1 = TILE + r - c
    ok0 = d0 >= 0
    ok1 = np.ones_like(ok0)
    if window is not None:
        ok0 = ok0 & (d0 < window)
        ok1 = d1 < window
    t0 = _bias_lookup(rel_bias, d0, ok0)
    t1 = _bias_lookup(rel_bias, d1, ok1)
    far = rel_bias[int(_t5_bucket(np.asarray(2 * TILE)))].astype(F32)
    return t0, t1, far


def _flash_update(s, v, m_ref, l_ref, acc_ref):
    m_prev = m_ref[...]
    m_new = jnp.maximum(m_prev, jnp.max(s, axis=1, keepdims=True))
    alpha = jnp.exp(m_prev - m_new)
    p = jnp.exp(s - jnp.concatenate([m_new] * (s.shape[1] // LANES), axis=1))
    l_ref[...] = alpha * l_ref[...] + jnp.sum(p, axis=1, keepdims=True)
    acc_ref[...] = alpha * acc_ref[...] + jnp.dot(p.astype(BF16), v, preferred_element_type=F32)
    m_ref[...] = m_new


def _flash_reset(m_ref, l_ref, acc_ref):
    m_ref[...] = jnp.full(m_ref.shape, NEG, F32)
    l_ref[...] = jnp.zeros(l_ref.shape, F32)
    acc_ref[...] = jnp.zeros(acc_ref.shape, F32)


def _rows(ref, t, tile=TILE):
    return ref[pl.ds(pl.multiple_of(t * tile, tile), tile), :]


def _split3(x):
    hi = x.astype(BF16)
    r1 = x - hi.astype(F32)
    mid = r1.astype(BF16)
    lo = (r1 - mid.astype(F32)).astype(BF16)
    return hi, mid, lo


def _rank_before(val, idx, n):
    cnt = jnp.zeros(val.shape, jnp.int32)
    for m in range(n):
        vm = val[m:m + 1, :]
        beats = (vm > val) | ((vm == val) & (m < idx))
        cnt = cnt + beats.astype(jnp.int32)
    return cnt


def _chosen_rows(chosen_t):
    n, q = chosen_t.shape
    padded = jnp.concatenate([chosen_t, jnp.zeros((LANES - n, q), F32)], axis=0)
    return padded.T.astype(BF16)


def _moba_kernel(q_ref, k_ref, v_ref, tab_ref, ex_ref, o_ref, m_ref, l_ref, acc_ref, km_ref, *, nblk):
    ci = pl.program_id(2)
    q = q_ref[...]

    @pl.when(ci == 0)
    def _():
        kmean = jnp.concatenate(
            [jnp.mean(k_ref[n * TILE:(n + 1) * TILE, :].astype(F32), axis=0, keepdims=True)
             for n in range(nblk)]
            + [jnp.zeros((LANES - nblk, HEAD_DIM), F32)], axis=0)
        for j, part in enumerate(_split3(kmean)):
            km_ref[j] = part

    nrow = -(-nblk // 8) * 8
    gate = sum(lax.dot_general(km_ref[j], q, _NT, preferred_element_type=F32) for j in range(3))[:nrow]
    blk = lax.broadcasted_iota(jnp.int32, gate.shape, 0)
    qblk = 2 * ci + lax.broadcasted_iota(jnp.int32, gate.shape, 1) // TILE
    past = blk < qblk
    gate = jnp.where(past, gate, -jnp.inf)
    cnt = _rank_before(gate, blk, nblk)
    chosen = _chosen_rows(jnp.where((past & (cnt < MOBA_TOPK)) | (blk == qblk), 1.0, 0.0))

    def chunk(c, carry):
        rows = pl.ds(pl.multiple_of(c * BIG, BIG), BIG)
        s = lax.dot_general(q, k_ref[rows, :], _NT, preferred_element_type=F32)
        hit = jnp.dot(chosen, ex_ref[c], preferred_element_type=F32)
        s = s + tab_ref[0, jnp.minimum(ci - c, 2)] + (hit - 1.0) * (-NEG)
        _flash_update(s, v_ref[rows, :], m_ref, l_ref, acc_ref)
        return carry

    _flash_reset(m_ref, l_ref, acc_ref)
    chunk(ci, 0)
    lax.fori_loop(0, ci, chunk, 0)
    o_ref[...] = (acc_ref[...] / l_ref[...]).astype(o_ref.dtype)


def _moba_attention(qkv, rel_bias, b, s):
    assert MOBA_BLOCK == TILE and s % BIG == 0
    nblk = s // TILE
    nq = s // BIG
    assert max(1, min(MOBA_TOPK, nblk - 1)) == MOBA_TOPK and nblk <= LANES
    hh = N_HEADS
    t0, t1, far = _bias_tiles(rel_bias)
    ff = jnp.broadcast_to(far[:, None, None], t0.shape)
    zz = jnp.zeros_like(t0)
    blk = lambda a, b_, c, d: jnp.concatenate([jnp.concatenate([a, b_], axis=2),
                                               jnp.concatenate([c, d], axis=2)], axis=1)
    tab = jnp.stack([blk(t0, zz, t1, t0), blk(ff, t1, ff, ff), blk(ff, ff, ff, ff)], axis=1)
    expand = jnp.asarray(np.arange(LANES)[None, :, None]
                         == (2 * np.arange(nq)[:, None, None] + np.arange(BIG)[None, None, :] // TILE), BF16)
    kern = functools.partial(_moba_kernel, nblk=nblk)
    return pl.pallas_call(
        kern,
        out_shape=jax.ShapeDtypeStruct((b * s, hh * HEAD_DIM), BF16),
        grid=(b, hh, nq),
        in_specs=[pl.BlockSpec((BIG, HEAD_DIM), lambda bi, h, i: (bi * nq + i, h)),
                  pl.BlockSpec((s, HEAD_DIM), lambda bi, h, i: (bi, hh + h)),
                  pl.BlockSpec((s, HEAD_DIM), lambda bi, h, i: (bi, 2 * hh + h)),
                  pl.BlockSpec((1, 3, BIG, BIG), lambda bi, h, i: (h, 0, 0, 0)),
                  pl.BlockSpec((nq, LANES, BIG), lambda bi, h, i: (0, 0, 0))],
        out_specs=pl.BlockSpec((BIG, HEAD_DIM), lambda bi, h, i: (bi * nq + i, h)),
        scratch_shapes=[pltpu.VMEM((BIG, LANES), F32), pltpu.VMEM((BIG, LANES), F32),
                        pltpu.VMEM((BIG, HEAD_DIM), F32), pltpu.VMEM((3, LANES, HEAD_DIM), BF16)],
        compiler_params=_cparams(("parallel", "parallel", "arbitrary")),
        name="moba_attention",
    )(qkv, qkv, qkv, tab, expand)


def _swa_kernel(sink_ref, q_ref, k_ref, v_ref, t0_ref, t1_ref, o_ref, m_ref, l_ref, acc_ref, *, group):
    hk = pl.program_id(0)
    i = pl.program_id(2)
    q = jnp.concatenate([q_ref[:, g * HEAD_DIM:(g + 1) * HEAD_DIM] for g in range(group)], axis=0)

    for g in range(group):
        m_ref[g * TILE:(g + 1) * TILE, :] = jnp.full((TILE, LANES), sink_ref[hk * group + g], F32)
    l_ref[...] = jnp.ones(l_ref.shape, F32)
    acc_ref[...] = jnp.zeros(acc_ref.shape, F32)

    s = lax.dot_general(q, _rows(k_ref, i), _NT, preferred_element_type=F32)
    s = s + t0_ref[...].reshape(group * TILE, TILE)
    _flash_update(s, _rows(v_ref, i), m_ref, l_ref, acc_ref)

    @pl.when(i > 0)
    def _():
        s = lax.dot_general(q, _rows(k_ref, i - 1), _NT, preferred_element_type=F32)
        s = s + t1_ref[...].reshape(group * TILE, TILE)
        _flash_update(s, _rows(v_ref, i - 1), m_ref, l_ref, acc_ref)

    out = acc_ref[...] / l_ref[...]
    for g in range(group):
        o_ref[:, g * HEAD_DIM:(g + 1) * HEAD_DIM] = out[g * TILE:(g + 1) * TILE].astype(o_ref.dtype)


def _swa_attention(qkv, sinks, rel_bias, b, s):
    assert SWA_WINDOW <= TILE and s % TILE == 0
    nq = s // TILE
    hk_n = SWA_KV_HEADS
    group = N_HEADS // hk_n
    t0, t1, _ = _bias_tiles(rel_bias, window=SWA_WINDOW)
    kern = functools.partial(_swa_kernel, group=group)
    gw = group * HEAD_DIM
    return pl.pallas_call(
        kern,
        out_shape=jax.ShapeDtypeStruct((b * s, N_HEADS * HEAD_DIM), BF16),
        grid=(hk_n, b, nq),
        in_specs=[pl.BlockSpec(memory_space=pltpu.SMEM),
                  pl.BlockSpec((TILE, gw), lambda hk, bi, i: (bi * nq + i, hk)),
                  pl.BlockSpec((s, HEAD_DIM), lambda hk, bi, i: (bi, N_HEADS + hk)),
                  pl.BlockSpec((s, HEAD_DIM), lambda hk, bi, i: (bi, N_HEADS + hk_n + hk)),
                  pl.BlockSpec((group, TILE, TILE), lambda hk, bi, i: (hk, 0, 0)),
                  pl.BlockSpec((group, TILE, TILE), lambda hk, bi, i: (hk, 0, 0))],
        out_specs=pl.BlockSpec((TILE, gw), lambda hk, bi, i: (bi * nq + i, hk)),
        scratch_shapes=[pltpu.VMEM((group * TILE, LANES), F32), pltpu.VMEM((group * TILE, LANES), F32),
                        pltpu.VMEM((group * TILE, HEAD_DIM), F32)],
        compiler_params=_cparams(("parallel", "parallel", "arbitrary")),
        name="swa_attention",
    )(sinks.astype(F32), qkv, qkv, qkv, t0, t1)


def _gelu_tanh(x):
    return 0.5 * x * (1.0 + jnp.tanh(math.sqrt(2.0 / math.pi) * (x + 0.044715 * (x * x * x))))


def _nsa_compress_kernel(t_ref, pe_ref, w1_ref, w2_ref, o_ref, tf_ref, *, ncmp_pad):
    stride = NSA_CMP_STRIDE
    half = stride * HEAD_DIM
    pe = pe_ref[0]
    tf_ref[...] = t_ref[...].astype(F32)
    xa, xb = [], []
    for r in range(stride):
        x = tf_ref[pl.ds(r, ncmp_pad, stride=stride), :]
        xa.append((x + pe[r:r + 1, :]).astype(BF16))
        xb.append((x + pe[stride + r:stride + r + 1, :]).astype(BF16))
    a = jnp.dot(jnp.concatenate(xa, axis=1), w1_ref[0, :half, :], preferred_element_type=F32)
    bm = jnp.dot(jnp.concatenate(xb, axis=1), w1_ref[0, half:, :], preferred_element_type=F32)
    hid = _gelu_tanh(a + pltpu.roll(bm, ncmp_pad - 1, 0))
    o_ref[0, 0, 0] = jnp.dot(hid.astype(BF16), w2_ref[0], preferred_element_type=F32).astype(o_ref.dtype)


def _nsa_compress(proj, pe, w1, w2, b, s):
    assert NSA_CMP_LEN == 2 * NSA_CMP_STRIDE and s % NSA_CMP_STRIDE == 0
    ncp = s // NSA_CMP_STRIDE
    hk_n = NSA_KV_HEADS
    kern = functools.partial(_nsa_compress_kernel, ncmp_pad=ncp)
    return pl.pallas_call(
        kern,
        out_shape=jax.ShapeDtypeStruct((2, b, hk_n, ncp, HEAD_DIM), BF16),
        grid=(2, b, hk_n),
        in_specs=[pl.BlockSpec((s, HEAD_DIM), lambda kv, bi, hk: (bi, N_HEADS + kv * hk_n + hk)),
                  pl.BlockSpec((1, NSA_CMP_LEN, HEAD_DIM), lambda kv, bi, hk: (kv, 0, 0)),
                  pl.BlockSpec((1, NSA_CMP_LEN * HEAD_DIM, NSA_CMP_HIDDEN), lambda kv, bi, hk: (kv, 0, 0)),
                  pl.BlockSpec((1, NSA_CMP_HIDDEN, HEAD_DIM), lambda kv, bi, hk: (kv, 0, 0))],
        out_specs=pl.BlockSpec((1, 1, 1, ncp, HEAD_DIM), lambda kv, bi, hk: (kv, bi, hk, 0, 0)),
        scratch_shapes=[pltpu.VMEM((s, HEAD_DIM), F32)],
        compiler_params=_cparams(("parallel", "parallel", "arbitrary")),
        name="nsa_compress",
    )(proj, pe.astype(F32), w1, w2)


def _nsa_kernel(far_ref, q_ref, kc_ref, vc_ref, cb_ref, ksl_ref, vsl_ref, kw_ref, vw_ref, gate_ref,
                t0_ref, t1_ref, tw_ref, ov_ref, ex_ref, o_ref,
                m_ref, l_ref, acc_ref, sel_ref, oslc_ref, *, group, nq):
    hk = pl.program_id(1)
    i = pl.program_id(2)
    rows = group * TILE
    q = jnp.concatenate([q_ref[:, g * HEAD_DIM:(g + 1) * HEAD_DIM] for g in range(group)], axis=0)

    cb = cb_ref[...].reshape(rows, LANES)
    sc = lax.dot_general(q, kc_ref[0, 0, 0], _NT, preferred_element_type=F32) + cb
    valid = cb > 0.5 * NEG
    mc = jnp.max(sc, axis=1, keepdims=True)
    pc = jnp.where(valid, jnp.exp(sc - mc), 0.0)
    pc = pc / jnp.maximum(jnp.sum(pc, axis=1, keepdims=True), 1e-30)
    o_cmp = jnp.dot(pc.astype(BF16), vc_ref[0, 0, 0], preferred_element_type=F32)

    psum = pc[0:TILE]
    for g in range(1, group):
        psum = psum + pc[g * TILE:(g + 1) * TILE]
    n_sel = nq * (TILE // NSA_SEL_BLOCK)
    imp = sum(lax.dot_general(ov_ref[...], part, _NT, preferred_element_type=F32)
              for part in _split3(psum))[:n_sel]
    blk = lax.broadcasted_iota(jnp.int32, imp.shape, 0)
    qblk = (i * TILE + lax.broadcasted_iota(jnp.int32, imp.shape, 1)) // NSA_SEL_BLOCK
    forced = (blk == 0) | (blk == qblk) | (blk == qblk - 1)
    imp = jnp.where(blk > qblk, -jnp.inf, jnp.where(forced, jnp.inf, imp))
    cnt = _rank_before(imp, blk, n_sel)
    chosen = _chosen_rows(jnp.where((blk <= qblk) & (cnt < NSA_TOPN), 1.0, 0.0))
    for t in range(nq):
        @pl.when(t <= i)
        def _(t=t):
            hit = jnp.dot(chosen, ex_ref[:, t * TILE:(t + 1) * TILE], preferred_element_type=F32)
            sel_ref[t] = (hit - 1.0) * (-NEG)

    t0 = t0_ref[...].reshape(rows, TILE)

    def masked(s, t, with_far=False):
        sel = sel_ref[t]
        parts = []
        for g in range(group):
            add = sel + far_ref[hk * group + g] if with_far else sel
            parts.append(s[g * TILE:(g + 1) * TILE] + add)
        return jnp.concatenate(parts, axis=0)

    _flash_reset(m_ref, l_ref, acc_ref)
    s = lax.dot_general(q, _rows(ksl_ref, i), _NT, preferred_element_type=F32) + t0
    _flash_update(masked(s, i), _rows(vsl_ref, i), m_ref, l_ref, acc_ref)

    @pl.when(i > 0)
    def _():
        s = lax.dot_general(q, _rows(ksl_ref, i - 1), _NT, preferred_element_type=F32)
        s = s + t1_ref[...].reshape(rows, TILE)
        _flash_update(masked(s, i - 1), _rows(vsl_ref, i - 1), m_ref, l_ref, acc_ref)

    def far_tile(t, carry):
        s = lax.dot_general(q, _rows(ksl_ref, t), _NT, preferred_element_type=F32)
        _flash_update(masked(s, t, with_far=True), _rows(vsl_ref, t), m_ref, l_ref, acc_ref)
        return carry

    lax.fori_loop(0, jnp.maximum(i - 1, 0), far_tile, 0)
    oslc_ref[...] = acc_ref[...] / l_ref[...]

    _flash_reset(m_ref, l_ref, acc_ref)
    s = lax.dot_general(q, _rows(kw_ref, i), _NT, preferred_element_type=F32) + t0
    _flash_update(s, _rows(vw_ref, i), m_ref, l_ref, acc_ref)

    @pl.when(i > 0)
    def _():
        s = lax.dot_general(q, _rows(kw_ref, i - 1), _NT, preferred_element_type=F32)
        s = s + t1_ref[...].reshape(rows, TILE)
        _flash_update(s, _rows(vw_ref, i - 1), m_ref, l_ref, acc_ref)

    @pl.when(i > 1)
    def _():
        s = lax.dot_general(q, _rows(kw_ref, i - 2), _NT, preferred_element_type=F32)
        s = s + tw_ref[...].reshape(rows, TILE)
        _flash_update(s, _rows(vw_ref, i - 2), m_ref, l_ref, acc_ref)

    o_win = acc_ref[...] / l_ref[...]
    o_slc = oslc_ref[...]
    gates = jax.nn.sigmoid(gate_ref[...])
    for g in range(group):
        sl = slice(g * TILE, (g + 1) * TILE)
        o = (gates[:, 3 * g:3 * g + 1] * o_cmp[sl] + gates[:, 3 * g + 1:3 * g + 2] * o_slc[sl]
             + gates[:, 3 * g + 2:3 * g + 3] * o_win[sl])
        o_ref[:, g * HEAD_DIM:(g + 1) * HEAD_DIM] = o.astype(o_ref.dtype)


def _nsa_attention(proj, gates, cmp_kv, rel_bias, b, s):
    assert s % TILE == 0 and NSA_WINDOW == 2 * TILE and TILE % NSA_SEL_BLOCK == 0
    nq = s // TILE
    hk_n = NSA_KV_HEADS
    group = N_HEADS // hk_n
    gw = group * HEAD_DIM
    ncp = s // NSA_CMP_STRIDE
    n_cmp = (s - NSA_CMP_LEN) // NSA_CMP_STRIDE + 1
    n_sel = s // NSA_SEL_BLOCK
    assert ncp == LANES and n_sel <= LANES and min(NSA_TOPN, n_sel) == NSA_TOPN
    t0, t1, far = _bias_tiles(rel_bias)
    r = np.arange(TILE)[:, None]
    c = np.arange(TILE)[None, :]
    tw = jnp.where(jnp.asarray(c > r)[None], far[:, None, None], NEG)
    pos = np.arange(s)[:, None]
    cidx = np.arange(ncp)[None, :]
    cdist = pos - (cidx * NSA_CMP_STRIDE + NSA_CMP_LEN - 1)
    cmp_bias = _bias_lookup(rel_bias, cdist, (cdist >= 0) & (cidx < n_cmp))
    ci = np.arange(ncp)[:, None] * NSA_CMP_STRIDE
    sj = np.arange(LANES)[None, :] * NSA_SEL_BLOCK
    overlap = jnp.asarray(((ci < sj + NSA_SEL_BLOCK) & (ci + NSA_CMP_LEN > sj)
                           & (np.arange(ncp)[:, None] < n_cmp) & (np.arange(LANES)[None, :] < n_sel)).T, BF16)
    expand = jnp.asarray(np.arange(LANES)[:, None] == (np.arange(s)[None, :] // NSA_SEL_BLOCK), BF16)

    kern = functools.partial(_nsa_kernel, group=group, nq=nq)
    kv_spec = lambda off: pl.BlockSpec((s, HEAD_DIM), lambda bi, hk, i: (bi, N_HEADS + off * hk_n + hk))
    cmp_spec = lambda kv: pl.BlockSpec((1, 1, 1, ncp, HEAD_DIM), lambda bi, hk, i: (kv, bi, hk, 0, 0))
    tile_spec = pl.BlockSpec((group, TILE, TILE), lambda bi, hk, i: (hk, 0, 0))
    return pl.pallas_call(
        kern,
        out_shape=jax.ShapeDtypeStruct((b * s, N_HEADS * HEAD_DIM), BF16),
        grid=(b, hk_n, nq),
        in_specs=[pl.BlockSpec(memory_space=pltpu.SMEM),
                  pl.BlockSpec((TILE, gw), lambda bi, hk, i: (bi * nq + i, hk)),
                  cmp_spec(0), cmp_spec(1),
                  pl.BlockSpec((group, TILE, ncp), lambda bi, hk, i: (hk, i, 0)),
                  kv_spec(2), kv_spec(3), kv_spec(4), kv_spec(5),
                  pl.BlockSpec((TILE, LANES), lambda bi, hk, i: (bi * nq + i, hk)),
                  tile_spec, tile_spec, tile_spec,
                  pl.BlockSpec((ncp, LANES), lambda bi, hk, i: (0, 0)),
                  pl.BlockSpec((LANES, s), lambda bi, hk, i: (0, 0))],
        out_specs=pl.BlockSpec((TILE, gw), lambda bi, hk, i: (bi * nq + i, hk)),
        scratch_shapes=[pltpu.VMEM((group * TILE, LANES), F32), pltpu.VMEM((group * TILE, LANES), F32),
                        pltpu.VMEM((group * TILE, HEAD_DIM), F32),
                        pltpu.VMEM((nq, TILE, TILE), F32),
                        pltpu.VMEM((group * TILE, HEAD_DIM), F32)],
        compiler_params=_cparams(("parallel", "parallel", "arbitrary")),
        name="nsa_attention",
    )(far, proj, cmp_kv, cmp_kv, cmp_bias, proj, proj, proj, proj, gates, t0, t1, tw, overlap, expand)


def _rms(x, g):
    return x * lax.rsqrt(jnp.mean(x * x, axis=-1, keepdims=True) + RMS_EPS) * g


def _mla_prep_kernel(c_ref, qg_ref, kg_ref, tab_ref, cq_ref, ckv_ref, kr_ref):
    c = c_ref[...]
    cq_ref[...] = _rms(c[:, :MLA_Q_LORA], qg_ref[...]).astype(BF16)
    ckv_ref[...] = _rms(c[:, MLA_Q_LORA:MLA_Q_LORA + MLA_KV_LORA], kg_ref[...]).astype(BF16)
    y = c[:, MLA_Q_LORA + MLA_KV_LORA:] * tab_ref[...]
    y = y + pltpu.roll(y, MLA_ROPE_DIM, 1)
    lane = lax.broadcasted_iota(jnp.int32, y.shape, 1)
    kr_ref[0] = jnp.where(lane < MLA_ROPE_DIM, y, 0.0).astype(BF16)
    kr_ref[1] = jnp.where(lane >= MLA_ROPE_DIM, y, 0.0).astype(BF16)


def _mla_prep(c, q_norm, kv_norm, ktab, s, *, tm=512):
    m, w = c.shape
    tm = min(tm, s)
    assert 2 * MLA_ROPE_DIM == LANES and w == MLA_Q_LORA + MLA_KV_LORA + LANES and s % tm == 0
    ns = s // tm
    return pl.pallas_call(
        _mla_prep_kernel,
        out_shape=(jax.ShapeDtypeStruct((m, MLA_Q_LORA), BF16),
                   jax.ShapeDtypeStruct((m, MLA_KV_LORA), BF16),
                   jax.ShapeDtypeStruct((2, m, LANES), BF16)),
        grid=(m // tm,),
        in_specs=[pl.BlockSpec((tm, w), lambda i: (i, 0)),
                  pl.BlockSpec((1, MLA_Q_LORA), lambda i: (0, 0)),
                  pl.BlockSpec((1, MLA_KV_LORA), lambda i: (0, 0)),
                  pl.BlockSpec((tm, LANES), lambda i: (i % ns, 0))],
        out_specs=(pl.BlockSpec((tm, MLA_Q_LORA), lambda i: (i, 0)),
                   pl.BlockSpec((tm, MLA_KV_LORA), lambda i: (i, 0)),
                   pl.BlockSpec((2, tm, LANES), lambda i: (0, i, 0))),
        compiler_params=_cparams(("parallel",)),
        name="mla_prep",
    )(c, q_norm.reshape(1, -1).astype(F32), kv_norm.reshape(1, -1).astype(F32), ktab)


def _mla_kernel(qn_ref, qr_ref, qs_ref, ct_ref, st_ref, kn_ref, kr_ref, v_ref, o_ref,
                m_ref, l_ref, acc_ref, kcat_ref):
    i = pl.program_id(2)

    @pl.when(i == 0)
    def _():
        kcat_ref[:, :HEAD_DIM] = kn_ref[...]
        kcat_ref[:, HEAD_DIM:] = kr_ref[0]

    qr = qr_ref[...].astype(F32) * ct_ref[...] + qs_ref[...].astype(F32) * st_ref[...]
    q = jnp.concatenate([qn_ref[...], qr.astype(BF16)], axis=1)

    _flash_reset(m_ref, l_ref, acc_ref)
    s = lax.dot_general(q, _rows(kcat_ref, i, BIG), _NT, preferred_element_type=F32)
    r = lax.broadcasted_iota(jnp.int32, s.shape, 0)
    c = lax.broadcasted_iota(jnp.int32, s.shape, 1)
    _flash_update(jnp.where(r >= c, s, NEG), _rows(v_ref, i, BIG), m_ref, l_ref, acc_ref)

    def past_tile(t, carry):
        s = lax.dot_general(q, _rows(kcat_ref, t, BIG), _NT, preferred_element_type=F32)
        _flash_update(s, _rows(v_ref, t, BIG), m_ref, l_ref, acc_ref)
        return carry

    lax.fori_loop(0, i, past_tile, 0)
    o_ref[...] = (acc_ref[...] / l_ref[...]).astype(o_ref.dtype)


def _mla_attention(qx, kv, kr, qc_tab, qs_tab, b, s):
    assert s % BIG == 0
    nq = s // BIG
    hh = N_HEADS
    return pl.pallas_call(
        _mla_kernel,
        out_shape=jax.ShapeDtypeStruct((b * s, hh * MLA_V_DIM), BF16),
        grid=(b, hh, nq),
        in_specs=[pl.BlockSpec((BIG, LANES), lambda bi, h, i: (bi * nq + i, h)),
                  pl.BlockSpec((BIG, LANES), lambda bi, h, i: (bi * nq + i, hh + h // 2)),
                  pl.BlockSpec((BIG, LANES), lambda bi, h, i: (bi * nq + i, hh + hh // 2 + h // 2)),
                  pl.BlockSpec((BIG, LANES), lambda bi, h, i: (i, 0)),
                  pl.BlockSpec((BIG, LANES), lambda bi, h, i: (i, 0)),
                  pl.BlockSpec((s, LANES), lambda bi, h, i: (bi, h)),
                  pl.BlockSpec((1, s, LANES), lambda bi, h, i: (h % 2, bi, 0)),
                  pl.BlockSpec((s, LANES), lambda bi, h, i: (bi, hh + h))],
        out_specs=pl.BlockSpec((BIG, LANES), lambda bi, h, i: (bi * nq + i, h)),
        scratch_shapes=[pltpu.VMEM((BIG, LANES), F32), pltpu.VMEM((BIG, LANES), F32),
                        pltpu.VMEM((BIG, MLA_V_DIM), F32), pltpu.VMEM((s, 2 * LANES), BF16)],
        compiler_params=_cparams(("parallel", "parallel", "arbitrary")),
        name="mla_attention",
    )(qx, qx, qx, qc_tab, qs_tab, kv, kr, kv)


def _bf(w):
    return w if isinstance(w, tuple) or w.dtype == BF16 else w.astype(BF16)


def _moba_mixer(xb, w_qkv, rel_bias, b, s):
    hd = N_HEADS * HEAD_DIM
    qkv = _matmul(xb, _bf(w_qkv), BF16, tn=512, scale_cols=hd, scale=HEAD_DIM ** -0.5)
    return _moba_attention(qkv, rel_bias, b, s)


def _swa_mixer(xb, w_qkv, sinks, rel_bias, b, s):
    hd = N_HEADS * HEAD_DIM
    qkv = _matmul(xb, _bf(w_qkv), BF16, tn=512, scale_cols=hd, scale=HEAD_DIM ** -0.5)
    return _swa_attention(qkv, sinks, rel_bias, b, s)


def _nsa_mixer(xb, w_in, w_gate, cmp_pos, cmp_w1, cmp_w2, rel_bias, b, s):
    hd = N_HEADS * HEAD_DIM
    hk_n = NSA_KV_HEADS
    group = N_HEADS // hk_n
    main = hd + 6 * hk_n * HEAD_DIM
    proj = _matmul(xb, _bf(w_in), BF16, tn=512, scale_cols=hd, scale=HEAD_DIM ** -0.5, n_out=main)
    wg = w_gate.reshape(-1, hk_n, 3 * group)
    wg = jnp.pad(wg, ((0, 0), (0, 0), (0, LANES - 3 * group))).reshape(-1, hk_n * LANES)
    gates = _matmul(xb, wg.astype(BF16), F32, tn=hk_n * LANES)
    cmp_kv = _nsa_compress(proj, cmp_pos, _bf(cmp_w1), _bf(cmp_w2), b, s)
    return _nsa_attention(proj, gates, cmp_kv, rel_bias, b, s)


def _mla_mixer(xb, w_down, q_norm, kv_norm, w_uq, w_ukv, b, s):
    hh = N_HEADS
    half = MLA_ROPE_DIM // 2
    lat = MLA_Q_LORA + MLA_KV_LORA
    swap = lambda t: jnp.concatenate([t[..., half:], t[..., :half]], axis=-1)
    w_down_x = jnp.concatenate([w_down, swap(w_down[:, lat:])], axis=1).astype(BF16)
    c = _matmul(xb, w_down_x, F32, tn=w_down_x.shape[1])

    inv = ROPE_BASE ** (-jnp.arange(0, MLA_ROPE_DIM, 2, dtype=F32) / MLA_ROPE_DIM)
    ang = jnp.arange(s, dtype=F32)[:, None] * inv[None, :]
    cos, sin = jnp.cos(ang), jnp.sin(ang)
    ktab = jnp.concatenate([cos, cos, -sin, sin], axis=1)
    qc_tab = jnp.concatenate([cos, cos, cos, cos], axis=1)
    qs_tab = jnp.concatenate([-sin, sin, -sin, sin], axis=1)
    cq, ckv, kr = _mla_prep(c, q_norm, kv_norm, ktab, s)

    wq = w_uq.reshape(MLA_Q_LORA, hh, MLA_NOPE_DIM + MLA_ROPE_DIM)
    wq_rope = wq[:, :, MLA_NOPE_DIM:]
    wq_x = jnp.concatenate([wq[:, :, :MLA_NOPE_DIM].reshape(MLA_Q_LORA, -1),
                            wq_rope.reshape(MLA_Q_LORA, -1),
                            swap(wq_rope).reshape(MLA_Q_LORA, -1)], axis=1).astype(BF16)
    scale = (MLA_NOPE_DIM + MLA_ROPE_DIM) ** -0.5
    qx = _matmul(cq, wq_x, BF16, tn=512, scale_cols=wq_x.shape[1], scale=scale)
    wkv = w_ukv.reshape(MLA_KV_LORA, hh, MLA_NOPE_DIM + MLA_V_DIM)
    wkv_x = jnp.concatenate([wkv[:, :, :MLA_NOPE_DIM].reshape(MLA_KV_LORA, -1),
                             wkv[:, :, MLA_NOPE_DIM:].reshape(MLA_KV_LORA, -1)], axis=1).astype(BF16)
    kv = _matmul(ckv, wkv_x, BF16, tn=512)
    return _mla_attention(qx, kv, kr, qc_tab, qs_tab, b, s)


def kernel(x, rel_bias, moba_w_qkv, moba_w_o, swa_w_qkv, swa_sinks, swa_w_o, nsa_w_in, nsa_cmp_pos,
           nsa_cmp_w1, nsa_cmp_w2, nsa_w_o, mla_w_down, mla_q_norm, mla_kv_norm, mla_w_uq, mla_w_ukv,
           mla_w_o, ln1_g, ln1_b, ffn_w_gate, ffn_w_up, ffn_w_down, ln2_g, ln2_b):
    b, s, d = x.shape
    depth = ln1_g.shape[0]
    n_mixers = 4
    xf = x.reshape(b * s, d).astype(F32)
    xb = xf.astype(BF16)
    moba_qkv_b, moba_o_b = _to_bf16(moba_w_qkv), _to_bf16(moba_w_o)
    swa_qkv_b, swa_o_b = _to_bf16(swa_w_qkv), _to_bf16(swa_w_o)
    nsa_in_b, nsa_o_b, nsa_w1_b = _to_bf16(nsa_w_in), _to_bf16(nsa_w_o), _to_bf16(nsa_cmp_w1)
    mla_o_b = _to_bf16(mla_w_o)
    gate_b, up_b, down_b = _to_bf16(ffn_w_gate), _to_bf16(ffn_w_up), _to_bf16(ffn_w_down)
    nsa_main = (N_HEADS + 6 * NSA_KV_HEADS) * HEAD_DIM
    for i in range(depth):
        kind, j = i % n_mixers, i // n_mixers
        if kind == 0:
            o = _moba_mixer(xb, (moba_qkv_b, j), rel_bias, b, s)
            w_o = (moba_o_b, j)
        elif kind == 1:
            o = _swa_mixer(xb, (swa_qkv_b, j), swa_sinks[j], rel_bias, b, s)
            w_o = (swa_o_b, j)
        elif kind == 2:
            o = _nsa_mixer(xb, (nsa_in_b, j), nsa_w_in[j][:, nsa_main:], nsa_cmp_pos[j], nsa_w1_b[j],
                           nsa_cmp_w2[j], rel_bias, b, s)
            w_o = (nsa_o_b, j)
        else:
            o = _mla_mixer(xb, mla_w_down[j], mla_q_norm[j], mla_kv_norm[j], mla_w_uq[j], mla_w_ukv[j], b, s)
            w_o = (mla_o_b, j)
        xf, xb = _proj_ln(o, w_o, xf, ln1_g[i], ln1_b[i])
        hmid = _ffn_up(xb, (gate_b, i), (up_b, i))
        xf, xb = _proj_ln(hmid, (down_b, i), xf, ln2_g[i], ln2_b[i])
    return xf.reshape(b, s, d).astype(x.dtype)
```

```python
import functools
import math

import jax
import jax.numpy as jnp
import numpy as np
from jax import lax
from jax.experimental import pallas as pl
from jax.experimental.pallas import tpu as pltpu

F32 = jnp.float32
BF16 = jnp.bfloat16

N_HEADS = 16
HEAD_DIM = 128
REL_BUCKETS = 32
REL_MAX_DIST = 128
MOBA_BLOCK = 256
MOBA_TOPK = 3
SWA_WINDOW = 128
SWA_KV_HEADS = 2
NSA_KV_HEADS = 4
NSA_CMP_LEN = 32
NSA_CMP_STRIDE = 16
NSA_CMP_HIDDEN = 256
NSA_SEL_BLOCK = 64
NSA_TOPN = 16
NSA_WINDOW = 512
MLA_Q_LORA = 512
MLA_KV_LORA = 512
MLA_NOPE_DIM = 128
MLA_ROPE_DIM = 64
MLA_V_DIM = 128
ROPE_BASE = 10000.0
DEPTH = 4
ALPHA = (2 * DEPTH) ** 0.25
LN_EPS = 1e-5
RMS_EPS = 1e-6

LANES = 128
TILE = 256
BIG = 2 * TILE
NEG = -1e30
VMEM_LIMIT = 56 * 1024 * 1024

_NT = (((1,), (1,)), ((), ()))


def _cparams(sem):
    return pltpu.CompilerParams(dimension_semantics=sem, vmem_limit_bytes=VMEM_LIMIT)


def _matmul_kernel(x_ref, w_ref, o_ref, *, scale_tiles, scale):
    acc = jnp.dot(x_ref[...], w_ref[...], preferred_element_type=F32)
    if scale_tiles:
        acc = acc * jnp.where(pl.program_id(1) < scale_tiles, scale, 1.0).astype(F32)
    o_ref[...] = acc.astype(o_ref.dtype)


CAST_BLOCK_BYTES = 6 * 1024 * 1024


def _cast_kernel(x_ref, o_ref):
    o_ref[...] = x_ref[...].astype(o_ref.dtype)


def _to_bf16(w):
    w2 = w.reshape(-1, w.shape[-1])
    rows, cols = w2.shape
    tr = rows
    while tr % 2 == 0 and tr > 16 and tr * cols * 4 > CAST_BLOCK_BYTES:
        tr //= 2
    out = pl.pallas_call(
        _cast_kernel,
        out_shape=jax.ShapeDtypeStruct((rows, cols), BF16),
        grid=(rows // tr,),
        in_specs=[pl.BlockSpec((tr, cols), lambda i: (i, 0))],
        out_specs=pl.BlockSpec((tr, cols), lambda i: (i, 0)),
        compiler_params=_cparams(("parallel",)),
        name="cast_bf16",
    )(w2)
    return out.reshape(w.shape)


def _wspec(w, rows, cols, index):
    if isinstance(w, tuple):
        layer = w[1]
        return w[0], pl.BlockSpec((None, rows, cols), lambda i, j: (layer,) + index(i, j))
    return w, pl.BlockSpec((rows, cols), index)


def _wshape(w):
    return w[0].shape[1:] if isinstance(w, tuple) else w.shape


def _matmul(x, w, out_dtype, *, tn, tm=1024, scale_cols=0, scale=1.0, n_out=None):
    m, k = x.shape
    n = _wshape(w)[1] if n_out is None else n_out
    w, w_spec = _wspec(w, k, tn, lambda i, j: (0, j))
    tm = min(tm, m)
    assert m % tm == 0 and n % tn == 0 and scale_cols % tn == 0
    kern = functools.partial(_matmul_kernel, scale_tiles=scale_cols // tn, scale=scale)
    return pl.pallas_call(
        kern,
        out_shape=jax.ShapeDtypeStruct((m, n), out_dtype),
        grid=(m // tm, n // tn),
        in_specs=[pl.BlockSpec((tm, k), lambda i, j: (i, 0)), w_spec],
        out_specs=pl.BlockSpec((tm, tn), lambda i, j: (i, j)),
        compiler_params=_cparams(("parallel", "arbitrary")),
        name="proj_matmul",
    )(x, w)


LN_SUB = 128


def _proj_ln_kernel(a_ref, w_ref, x_ref, g_ref, b_ref, o_ref, ob_ref, acc_ref, *, nk):
    kk = pl.program_id(1)
    tm = a_ref.shape[0]

    def part(rows):
        return jnp.dot(a_ref[rows, :], w_ref[...], preferred_element_type=F32)

    def finish(rows, y):
        mu = jnp.mean(y, axis=-1, keepdims=True)
        yc = y - mu
        var = jnp.mean(yc * yc, axis=-1, keepdims=True)
        out = yc * lax.rsqrt(var + LN_EPS) * g_ref[...] + b_ref[...]
        o_ref[rows, :] = out
        ob_ref[rows, :] = out.astype(BF16)

    subs = [pl.ds(r, LN_SUB) for r in range(0, tm, LN_SUB)]
    if nk == 1:
        for rows in subs:
            finish(rows, ALPHA * x_ref[rows, :] + part(rows))
        return

    @pl.when(kk == 0)
    def _():
        acc_ref[...] = part(slice(None))

    @pl.when((kk > 0) & (kk < nk - 1))
    def _():
        acc_ref[...] += part(slice(None))

    @pl.when(kk == nk - 1)
    def _():
        for rows in subs:
            finish(rows, ALPHA * x_ref[rows, :] + (acc_ref[rows, :] + part(rows)))


MAX_TK = 2048


def _proj_ln(a, w, x, g, b, *, tm=512, tk=None):
    m, k = a.shape
    d = _wshape(w)[1]
    tm = min(tm, m)
    if tk is None:
        tk = max(t for t in range(LANES, min(k, MAX_TK) + 1, LANES) if k % t == 0)
    assert m % tm == 0 and k % tk == 0 and tm % LN_SUB == 0
    w, w_spec = _wspec(w, tk, d, lambda i, j: (j, 0))
    return pl.pallas_call(
        functools.partial(_proj_ln_kernel, nk=k // tk),
        out_shape=(jax.ShapeDtypeStruct((m, d), F32), jax.ShapeDtypeStruct((m, d), BF16)),
        grid=(m // tm, k // tk),
        in_specs=[pl.BlockSpec((tm, tk), lambda i, j: (i, j)),
                  w_spec,
                  pl.BlockSpec((tm, d), lambda i, j: (i, 0)),
                  pl.BlockSpec((1, d), lambda i, j: (0, 0)),
                  pl.BlockSpec((1, d), lambda i, j: (0, 0))],
        out_specs=(pl.BlockSpec((tm, d), lambda i, j: (i, 0)),
                   pl.BlockSpec((tm, d), lambda i, j: (i, 0))),
        scratch_shapes=[pltpu.VMEM((tm, d), F32)],
        compiler_params=_cparams(("parallel", "arbitrary")),
        name="proj_deepnorm",
    )(a, w, x, g.reshape(1, d), b.reshape(1, d))


def _ffn_up_kernel(x_ref, wg_ref, wu_ref, o_ref):
    x = x_ref[...]
    gate = jnp.dot(x, wg_ref[...], preferred_element_type=F32)
    up = jnp.dot(x, wu_ref[...], preferred_element_type=F32)
    o_ref[...] = (gate * jax.nn.sigmoid(gate) * up).astype(o_ref.dtype)


def _ffn_up(x, wg, wu, *, tm=1024, tn=512):
    m, k = x.shape
    f = _wshape(wg)[1]
    tm = min(tm, m)
    tn = min(tn, f)
    assert m % tm == 0 and f % tn == 0
    wg, wg_spec = _wspec(wg, k, tn, lambda i, j: (0, j))
    wu, wu_spec = _wspec(wu, k, tn, lambda i, j: (0, j))
    return pl.pallas_call(
        _ffn_up_kernel,
        out_shape=jax.ShapeDtypeStruct((m, f), BF16),
        grid=(m // tm, f // tn),
        in_specs=[pl.BlockSpec((tm, k), lambda i, j: (i, 0)), wg_spec, wu_spec],
        out_specs=pl.BlockSpec((tm, tn), lambda i, j: (i, j)),
        compiler_params=_cparams(("parallel", "arbitrary")),
        name="ffn_gate_up",
    )(x, wg, wu)


def _t5_bucket(dist):
    n = np.maximum(dist, 0)
    max_exact = REL_BUCKETS // 2
    nf = np.maximum(n, max_exact).astype(np.float32)
    large = max_exact + (np.log(nf / max_exact) / math.log(REL_MAX_DIST / max_exact)
                         * (REL_BUCKETS - max_exact)).astype(np.int32)
    return np.where(n < max_exact, n, np.minimum(large, REL_BUCKETS - 1)).astype(np.int32)


def _bias_lookup(rel_bias, dist, visible):
    bucket = _t5_bucket(dist).reshape(-1)
    onehot = (jnp.arange(REL_BUCKETS, dtype=jnp.int32)[:, None] == jnp.asarray(bucket)[None, :]).astype(F32)
    tab = jnp.dot(rel_bias.T.astype(F32), onehot, precision=lax.Precision.HIGHEST)
    tab = tab.reshape((rel_bias.shape[1],) + dist.shape)
    return jnp.where(jnp.asarray(visible)[None], tab, NEG)


def _bias_tiles(rel_bias, window=None):
    r = np.arange(TILE)[:, None]
    c = np.arange(TILE)[None, :]
    d0 = r - c
    d1 = TILE + r - c
    ok0 = d0 >= 0
    ok1 = np.ones_like(ok0)
    if window is not None:
        ok0 = ok0 & (d0 < window)
        ok1 = d1 < window
    t0 = _bias_lookup(rel_bias, d0, ok0)
    t1 = _bias_lookup(rel_bias, d1, ok1)
    far = rel_bias[int(_t5_bucket(np.asarray(2 * TILE)))].astype(F32)
    return t0, t1, far


def _flash_update(s, v, m_ref, l_ref, acc_ref):
    m_prev = m_ref[...]
    m_new = jnp.maximum(m_prev, jnp.max(s, axis=1, keepdims=True))
    alpha = jnp.exp(m_prev - m_new)
    p = jnp.exp(s - jnp.concatenate([m_new] * (s.shape[1] // LANES), axis=1))
    l_ref[...] = alpha * l_ref[...] + jnp.sum(p, axis=1, keepdims=True)
    acc_ref[...] = alpha * acc_ref[...] + jnp.dot(p.astype(BF16), v, preferred_element_type=F32)
    m_ref[...] = m_new


def _flash_reset(m_ref, l_ref, acc_ref):
    m_ref[...] = jnp.full(m_ref.shape, NEG, F32)
    l_ref[...] = jnp.zeros(l_ref.shape, F32)
    acc_ref[...] = jnp.zeros(acc_ref.shape, F32)


def _rows(ref, t, tile=TILE):
    return ref[pl.ds(pl.multiple_of(t * tile, tile), tile), :]


def _split3(x):
    hi = x.astype(BF16)
    r1 = x - hi.astype(F32)
    mid = r1.astype(BF16)
    lo = (r1 - mid.astype(F32)).astype(BF16)
    return hi, mid, lo


def _rank_before(val, idx, n):
    cnt = jnp.zeros(val.shape, jnp.int32)
    for m in range(n):
        vm = val[m:m + 1, :]
        beats = (vm > val) | ((vm == val) & (m < idx))
        cnt = cnt + beats.astype(jnp.int32)
    return cnt


def _chosen_rows(chosen_t):
    n, q = chosen_t.shape
    padded = jnp.concatenate([chosen_t, jnp.zeros((LANES - n, q), F32)], axis=0)
    return padded.T.astype(BF16)


def _moba_kernel(q_ref, k_ref, v_ref, tab_ref, ex_ref, o_ref, m_ref, l_ref, acc_ref, km_ref, *, nblk):
    ci = pl.program_id(2)
    q = q_ref[...]

    @pl.when(ci == 0)
    def _():
        kmean = jnp.concatenate(
            [jnp.mean(k_ref[n * TILE:(n + 1) * TILE, :].astype(F32), axis=0, keepdims=True)
             for n in range(nblk)]
            + [jnp.zeros((LANES - nblk, HEAD_DIM), F32)], axis=0)
        for j, part in enumerate(_split3(kmean)):
            km_ref[j] = part

    nrow = -(-nblk // 8) * 8
    gate = sum(lax.dot_general(km_ref[j], q, _NT, preferred_element_type=F32) for j in range(3))[:nrow]
    blk = lax.broadcasted_iota(jnp.int32, gate.shape, 0)
    qblk = 2 * ci + lax.broadcasted_iota(jnp.int32, gate.shape, 1) // TILE
    past = blk < qblk
    gate = jnp.where(past, gate, -jnp.inf)
    cnt = _rank_before(gate, blk, nblk)
    chosen = _chosen_rows(jnp.where((past & (cnt < MOBA_TOPK)) | (blk == qblk), 1.0, 0.0))

    def chunk(c, carry):
        rows = pl.ds(pl.multiple_of(c * BIG, BIG), BIG)
        s = lax.dot_general(q, k_ref[rows, :], _NT, preferred_element_type=F32)
        hit = jnp.dot(chosen, ex_ref[c], preferred_element_type=F32)
        s = s + tab_ref[0, jnp.minimum(ci - c, 2)] + (hit - 1.0) * (-NEG)
        _flash_update(s, v_ref[rows, :], m_ref, l_ref, acc_ref)
        return carry

    _flash_reset(m_ref, l_ref, acc_ref)
    chunk(ci, 0)
    lax.fori_loop(0, ci, chunk, 0)
    o_ref[...] = (acc_ref[...] / l_ref[...]).astype(o_ref.dtype)


def _moba_attention(qkv, rel_bias, b, s):
    assert MOBA_BLOCK == TILE and s % BIG == 0
    nblk = s // TILE
    nq = s // BIG
    assert max(1, min(MOBA_TOPK, nblk - 1)) == MOBA_TOPK and nblk <= LANES
    hh = N_HEADS
    t0, t1, far = _bias_tiles(rel_bias)
    ff = jnp.broadcast_to(far[:, None, None], t0.shape)
    zz = jnp.zeros_like(t0)
    blk = lambda a, b_, c, d: jnp.concatenate([jnp.concatenate([a, b_], axis=2),
                                               jnp.concatenate([c, d], axis=2)], axis=1)
    tab = jnp.stack([blk(t0, zz, t1, t0), blk(ff, t1, ff, ff), blk(ff, ff, ff, ff)], axis=1)
    expand = jnp.asarray(np.arange(LANES)[None, :, None]
                         == (2 * np.arange(nq)[:, None, None] + np.arange(BIG)[None, None, :] // TILE), BF16)
    kern = functools.partial(_moba_kernel, nblk=nblk)
    return pl.pallas_call(
        kern,
        out_shape=jax.ShapeDtypeStruct((b * s, hh * HEAD_DIM), BF16),
        grid=(b, hh, nq),
        in_specs=[pl.BlockSpec((BIG, HEAD_DIM), lambda bi, h, i: (bi * nq + i, h)),
                  pl.BlockSpec((s, HEAD_DIM), lambda bi, h, i: (bi, hh + h)),
                  pl.BlockSpec((s, HEAD_DIM), lambda bi, h, i: (bi, 2 * hh + h)),
                  pl.BlockSpec((1, 3, BIG, BIG), lambda bi, h, i: (h, 0, 0, 0)),
                  pl.BlockSpec((nq, LANES, BIG), lambda bi, h, i: (0, 0, 0))],
        out_specs=pl.BlockSpec((BIG, HEAD_DIM), lambda bi, h, i: (bi * nq + i, h)),
        scratch_shapes=[pltpu.VMEM((BIG, LANES), F32), pltpu.VMEM((BIG, LANES), F32),
                        pltpu.VMEM((BIG, HEAD_DIM), F32), pltpu.VMEM((3, LANES, HEAD_DIM), BF16)],
        compiler_params=_cparams(("parallel", "parallel", "arbitrary")),
        name="moba_attention",
    )(qkv, qkv, qkv, tab, expand)


def _swa_kernel(sink_ref, q_ref, k_ref, v_ref, t0_ref, t1_ref, o_ref, m_ref, l_ref, acc_ref, *, group):
    hk = pl.program_id(0)
    i = pl.program_id(2)
    q = jnp.concatenate([q_ref[:, g * HEAD_DIM:(g + 1) * HEAD_DIM] for g in range(group)], axis=0)

    for g in range(group):
        m_ref[g * TILE:(g + 1) * TILE, :] = jnp.full((TILE, LANES), sink_ref[hk * group + g], F32)
    l_ref[...] = jnp.ones(l_ref.shape, F32)
    acc_ref[...] = jnp.zeros(acc_ref.shape, F32)

    s = lax.dot_general(q, _rows(k_ref, i), _NT, preferred_element_type=F32)
    s = s + t0_ref[...].reshape(group * TILE, TILE)
    _flash_update(s, _rows(v_ref, i), m_ref, l_ref, acc_ref)

    @pl.when(i > 0)
    def _():
        s = lax.dot_general(q, _rows(k_ref, i - 1), _NT, preferred_element_type=F32)
        s = s + t1_ref[...].reshape(group * TILE, TILE)
        _flash_update(s, _rows(v_ref, i - 1), m_ref, l_ref, acc_ref)

    out = acc_ref[...] / l_ref[...]
    for g in range(group):
        o_ref[:, g * HEAD_DIM:(g + 1) * HEAD_DIM] = out[g * TILE:(g + 1) * TILE].astype(o_ref.dtype)


def _swa_attention(qkv, sinks, rel_bias, b, s):
    assert SWA_WINDOW <= TILE and s % TILE == 0
    nq = s // TILE
    hk_n = SWA_KV_HEADS
    group = N_HEADS // hk_n
    t0, t1, _ = _bias_tiles(rel_bias, window=SWA_WINDOW)
    kern = functools.partial(_swa_kernel, group=group)
    gw = group * HEAD_DIM
    return pl.pallas_call(
        kern,
        out_shape=jax.ShapeDtypeStruct((b * s, N_HEADS * HEAD_DIM), BF16),
        grid=(hk_n, b, nq),
        in_specs=[pl.BlockSpec(memory_space=pltpu.SMEM),
                  pl.BlockSpec((TILE, gw), lambda hk, bi, i: (bi * nq + i, hk)),
                  pl.BlockSpec((s, HEAD_DIM), lambda hk, bi, i: (bi, N_HEADS + hk)),
                  pl.BlockSpec((s, HEAD_DIM), lambda hk, bi, i: (bi, N_HEADS + hk_n + hk)),
                  pl.BlockSpec((group, TILE, TILE), lambda hk, bi, i: (hk, 0, 0)),
                  pl.BlockSpec((group, TILE, TILE), lambda hk, bi, i: (hk, 0, 0))],
        out_specs=pl.BlockSpec((TILE, gw), lambda hk, bi, i: (bi * nq + i, hk)),
        scratch_shapes=[pltpu.VMEM((group * TILE, LANES), F32), pltpu.VMEM((group * TILE, LANES), F32),
                        pltpu.VMEM((group * TILE, HEAD_DIM), F32)],
        compiler_params=_cparams(("parallel", "parallel", "arbitrary")),
        name="swa_attention",
    )(sinks.astype(F32), qkv, qkv, qkv, t0, t1)


def _gelu_tanh(x):
    return 0.5 * x * (1.0 + jnp.tanh(math.sqrt(2.0 / math.pi) * (x + 0.044715 * (x * x * x))))


def _nsa_compress_kernel(t_ref, pe_ref, w1_ref, w2_ref, o_ref, tf_ref, *, ncmp_pad):
    stride = NSA_CMP_STRIDE
    half = stride * HEAD_DIM
    pe = pe_ref[0]
    tf_ref[...] = t_ref[...].astype(F32)
    xa, xb = [], []
    for r in range(stride):
        x = tf_ref[pl.ds(r, ncmp_pad, stride=stride), :]
        xa.append((x + pe[r:r + 1, :]).astype(BF16))
        xb.append((x + pe[stride + r:stride + r + 1, :]).astype(BF16))
    a = jnp.dot(jnp.concatenate(xa, axis=1), w1_ref[0, :half, :], preferred_element_type=F32)
    bm = jnp.dot(jnp.concatenate(xb, axis=1), w1_ref[0, half:, :], preferred_element_type=F32)
    hid = _gelu_tanh(a + pltpu.roll(bm, ncmp_pad - 1, 0))
    o_ref[0, 0, 0] = jnp.dot(hid.astype(BF16), w2_ref[0], preferred_element_type=F32).astype(o_ref.dtype)


def _nsa_compress(proj, pe, w1, w2, b, s):
    assert NSA_CMP_LEN == 2 * NSA_CMP_STRIDE and s % NSA_CMP_STRIDE == 0
    ncp = s // NSA_CMP_STRIDE
    hk_n = NSA_KV_HEADS
    kern = functools.partial(_nsa_compress_kernel, ncmp_pad=ncp)
    return pl.pallas_call(
        kern,
        out_shape=jax.ShapeDtypeStruct((2, b, hk_n, ncp, HEAD_DIM), BF16),
        grid=(2, b, hk_n),
        in_specs=[pl.BlockSpec((s, HEAD_DIM), lambda kv, bi, hk: (bi, N_HEADS + kv * hk_n + hk)),
                  pl.BlockSpec((1, NSA_CMP_LEN, HEAD_DIM), lambda kv, bi, hk: (kv, 0, 0)),
                  pl.BlockSpec((1, NSA_CMP_LEN * HEAD_DIM, NSA_CMP_HIDDEN), lambda kv, bi, hk: (kv, 0, 0)),
                  pl.BlockSpec((1, NSA_CMP_HIDDEN, HEAD_DIM), lambda kv, bi, hk: (kv, 0, 0))],
        out_specs=pl.BlockSpec((1, 1, 1, ncp, HEAD_DIM), lambda kv, bi, hk: (kv, bi, hk, 0, 0)),
        scratch_shapes=[pltpu.VMEM((s, HEAD_DIM), F32)],
        compiler_params=_cparams(("parallel", "parallel", "arbitrary")),
        name="nsa_compress",
    )(proj, pe.astype(F32), w1, w2)


def _nsa_kernel(far_ref, q_ref, kc_ref, vc_ref, cb_ref, ksl_ref, vsl_ref, kw_ref, vw_ref, gate_ref,
                t0_ref, t1_ref, tw_ref, ov_ref, ex_ref, o_ref,
                m_ref, l_ref, acc_ref, sel_ref, mw_ref, lw_ref, accw_ref, *, group, nq):
    hk = pl.program_id(1)
    i = pl.program_id(2)
    rows = group * TILE
    q = jnp.concatenate([q_ref[:, g * HEAD_DIM:(g + 1) * HEAD_DIM] for g in range(group)], axis=0)

    cb = cb_ref[...].reshape(rows, LANES)
    sc = lax.dot_general(q, kc_ref[0, 0, 0], _NT, preferred_element_type=F32) + cb
    valid = cb > 0.5 * NEG
    mc = jnp.max(sc, axis=1, keepdims=True)
    pc = jnp.where(valid, jnp.exp(sc - mc), 0.0)
    pc = pc / jnp.maximum(jnp.sum(pc, axis=1, keepdims=True), 1e-30)
    o_cmp = jnp.dot(pc.astype(BF16), vc_ref[0, 0, 0], preferred_element_type=F32)

    psum = pc[0:TILE]
    for g in range(1, group):
        psum = psum + pc[g * TILE:(g + 1) * TILE]
    n_sel = nq * (TILE // NSA_SEL_BLOCK)
    imp = sum(lax.dot_general(ov_ref[...], part, _NT, preferred_element_type=F32)
              for part in _split3(psum))[:n_sel]
    blk = lax.broadcasted_iota(jnp.int32, imp.shape, 0)
    qblk = (i * TILE + lax.broadcasted_iota(jnp.int32, imp.shape, 1)) // NSA_SEL_BLOCK
    forced = (blk == 0) | (blk == qblk) | (blk == qblk - 1)
    imp = jnp.where(blk > qblk, -jnp.inf, jnp.where(forced, jnp.inf, imp))
    cnt = _rank_before(imp, blk, n_sel)
    chosen = _chosen_rows(jnp.where((blk <= qblk) & (cnt < NSA_TOPN), 1.0, 0.0))
    for t in range(nq):
        @pl.when(t <= i)
        def _(t=t):
            hit = jnp.dot(chosen, ex_ref[:, t * TILE:(t + 1) * TILE], preferred_element_type=F32)
            sel_ref[t] = (hit - 1.0) * (-NEG)

    t0 = t0_ref[...].reshape(rows, TILE)

    def masked(s, t, with_far=False):
        sel = sel_ref[t]
        parts = []
        for g in range(group):
            add = sel + far_ref[hk * group + g] if with_far else sel
            parts.append(s[g * TILE:(g + 1) * TILE] + add)
        return jnp.concatenate(parts, axis=0)

    slc = (m_ref, l_ref, acc_ref)
    win = (mw_ref, lw_ref, accw_ref)
    _flash_reset(*slc)
    _flash_reset(*win)

    def logits(k_ref, t):
        return lax.dot_general(q, _rows(k_ref, t), _NT, preferred_element_type=F32)

    _flash_update(masked(logits(ksl_ref, i) + t0, i), _rows(vsl_ref, i), *slc)
    _flash_update(logits(kw_ref, i) + t0, _rows(vw_ref, i), *win)

    @pl.when(i > 0)
    def _():
        t1 = t1_ref[...].reshape(rows, TILE)
        _flash_update(masked(logits(ksl_ref, i - 1) + t1, i - 1), _rows(vsl_ref, i - 1), *slc)
        _flash_update(logits(kw_ref, i - 1) + t1, _rows(vw_ref, i - 1), *win)

    @pl.when(i > 1)
    def _():
        _flash_update(masked(logits(ksl_ref, i - 2), i - 2, with_far=True), _rows(vsl_ref, i - 2), *slc)
        _flash_update(logits(kw_ref, i - 2) + tw_ref[...].reshape(rows, TILE), _rows(vw_ref, i - 2), *win)

    def far_tile(t, carry):
        _flash_update(masked(logits(ksl_ref, t), t, with_far=True), _rows(vsl_ref, t), *slc)
        return carry

    lax.fori_loop(0, jnp.maximum(i - 2, 0), far_tile, 0)
    o_slc = acc_ref[...] / l_ref[...]
    o_win = accw_ref[...] / lw_ref[...]
    gates = jax.nn.sigmoid(gate_ref[...])
    for g in range(group):
        sl = slice(g * TILE, (g + 1) * TILE)
        o = (gates[:, 3 * g:3 * g + 1] * o_cmp[sl] + gates[:, 3 * g + 1:3 * g + 2] * o_slc[sl]
             + gates[:, 3 * g + 2:3 * g + 3] * o_win[sl])
        o_ref[:, g * HEAD_DIM:(g + 1) * HEAD_DIM] = o.astype(o_ref.dtype)


def _nsa_attention(proj, gates, cmp_kv, rel_bias, b, s):
    assert s % TILE == 0 and NSA_WINDOW == 2 * TILE and TILE % NSA_SEL_BLOCK == 0
    nq = s // TILE
    hk_n = NSA_KV_HEADS
    group = N_HEADS // hk_n
    gw = group * HEAD_DIM
    ncp = s // NSA_CMP_STRIDE
    n_cmp = (s - NSA_CMP_LEN) // NSA_CMP_STRIDE + 1
    n_sel = s // NSA_SEL_BLOCK
    assert ncp == LANES and n_sel <= LANES and min(NSA_TOPN, n_sel) == NSA_TOPN
    t0, t1, far = _bias_tiles(rel_bias)
    r = np.arange(TILE)[:, None]
    c = np.arange(TILE)[None, :]
    tw = jnp.where(jnp.asarray(c > r)[None], far[:, None, None], NEG)
    pos = np.arange(s)[:, None]
    cidx = np.arange(ncp)[None, :]
    cdist = pos - (cidx * NSA_CMP_STRIDE + NSA_CMP_LEN - 1)
    cmp_bias = _bias_lookup(rel_bias, cdist, (cdist >= 0) & (cidx < n_cmp))
    ci = np.arange(ncp)[:, None] * NSA_CMP_STRIDE
    sj = np.arange(LANES)[None, :] * NSA_SEL_BLOCK
    overlap = jnp.asarray(((ci < sj + NSA_SEL_BLOCK) & (ci + NSA_CMP_LEN > sj)
                           & (np.arange(ncp)[:, None] < n_cmp) & (np.arange(LANES)[None, :] < n_sel)).T, BF16)
    expand = jnp.asarray(np.arange(LANES)[:, None] == (np.arange(s)[None, :] // NSA_SEL_BLOCK), BF16)

    kern = functools.partial(_nsa_kernel, group=group, nq=nq)
    kv_spec = lambda off: pl.BlockSpec((s, HEAD_DIM), lambda bi, hk, i: (bi, N_HEADS + off * hk_n + hk))
    cmp_spec = lambda kv: pl.BlockSpec((1, 1, 1, ncp, HEAD_DIM), lambda bi, hk, i: (kv, bi, hk, 0, 0))
    tile_spec = pl.BlockSpec((group, TILE, TILE), lambda bi, hk, i: (hk, 0, 0))
    return pl.pallas_call(
        kern,
        out_shape=jax.ShapeDtypeStruct((b * s, N_HEADS * HEAD_DIM), BF16),
        grid=(b, hk_n, nq),
        in_specs=[pl.BlockSpec(memory_space=pltpu.SMEM),
                  pl.BlockSpec((TILE, gw), lambda bi, hk, i: (bi * nq + i, hk)),
                  cmp_spec(0), cmp_spec(1),
                  pl.BlockSpec((group, TILE, ncp), lambda bi, hk, i: (hk, i, 0)),
                  kv_spec(2), kv_spec(3), kv_spec(4), kv_spec(5),
                  pl.BlockSpec((TILE, LANES), lambda bi, hk, i: (bi * nq + i, hk)),
                  tile_spec, tile_spec, tile_spec,
                  pl.BlockSpec((ncp, LANES), lambda bi, hk, i: (0, 0)),
                  pl.BlockSpec((LANES, s), lambda bi, hk, i: (0, 0))],
        out_specs=pl.BlockSpec((TILE, gw), lambda bi, hk, i: (bi * nq + i, hk)),
        scratch_shapes=[pltpu.VMEM((group * TILE, LANES), F32), pltpu.VMEM((group * TILE, LANES), F32),
                        pltpu.VMEM((group * TILE, HEAD_DIM), F32),
                        pltpu.VMEM((nq, TILE, TILE), F32),
                        pltpu.VMEM((group * TILE, LANES), F32), pltpu.VMEM((group * TILE, LANES), F32),
                        pltpu.VMEM((group * TILE, HEAD_DIM), F32)],
        compiler_params=_cparams(("parallel", "parallel", "arbitrary")),
        name="nsa_attention",
    )(far, proj, cmp_kv, cmp_kv, cmp_bias, proj, proj, proj, proj, gates, t0, t1, tw, overlap, expand)


def _rms(x, g):
    return x * lax.rsqrt(jnp.mean(x * x, axis=-1, keepdims=True) + RMS_EPS) * g


def _mla_prep_kernel(c_ref, qg_ref, kg_ref, tab_ref, cq_ref, ckv_ref, kr_ref):
    c = c_ref[...]
    cq_ref[...] = _rms(c[:, :MLA_Q_LORA], qg_ref[...]).astype(BF16)
    ckv_ref[...] = _rms(c[:, MLA_Q_LORA:MLA_Q_LORA + MLA_KV_LORA], kg_ref[...]).astype(BF16)
    y = c[:, MLA_Q_LORA + MLA_KV_LORA:] * tab_ref[...]
    y = y + pltpu.roll(y, MLA_ROPE_DIM, 1)
    lane = lax.broadcasted_iota(jnp.int32, y.shape, 1)
    kr_ref[0] = jnp.where(lane < MLA_ROPE_DIM, y, 0.0).astype(BF16)
    kr_ref[1] = jnp.where(lane >= MLA_ROPE_DIM, y, 0.0).astype(BF16)


def _mla_prep(c, q_norm, kv_norm, ktab, s, *, tm=512):
    m, w = c.shape
    tm = min(tm, s)
    assert 2 * MLA_ROPE_DIM == LANES and w == MLA_Q_LORA + MLA_KV_LORA + LANES and s % tm == 0
    ns = s // tm
    return pl.pallas_call(
        _mla_prep_kernel,
        out_shape=(jax.ShapeDtypeStruct((m, MLA_Q_LORA), BF16),
                   jax.ShapeDtypeStruct((m, MLA_KV_LORA), BF16),
                   jax.ShapeDtypeStruct((2, m, LANES), BF16)),
        grid=(m // tm,),
        in_specs=[pl.BlockSpec((tm, w), lambda i: (i, 0)),
                  pl.BlockSpec((1, MLA_Q_LORA), lambda i: (0, 0)),
                  pl.BlockSpec((1, MLA_KV_LORA), lambda i: (0, 0)),
                  pl.BlockSpec((tm, LANES), lambda i: (i % ns, 0))],
        out_specs=(pl.BlockSpec((tm, MLA_Q_LORA), lambda i: (i, 0)),
                   pl.BlockSpec((tm, MLA_KV_LORA), lambda i: (i, 0)),
                   pl.BlockSpec((2, tm, LANES), lambda i: (0, i, 0))),
        compiler_params=_cparams(("parallel",)),
        name="mla_prep",
    )(c, q_norm.reshape(1, -1).astype(F32), kv_norm.reshape(1, -1).astype(F32), ktab)


def _mla_kernel(qn_ref, qr_ref, qs_ref, ct_ref, st_ref, kn_ref, kr_ref, v_ref, o_ref,
                m_ref, l_ref, acc_ref, kcat_ref):
    i = pl.program_id(2)

    @pl.when(i == 0)
    def _():
        kcat_ref[:, :HEAD_DIM] = kn_ref[...]
        kcat_ref[:, HEAD_DIM:] = kr_ref[0]

    qr = qr_ref[...].astype(F32) * ct_ref[...] + qs_ref[...].astype(F32) * st_ref[...]
    q = jnp.concatenate([qn_ref[...], qr.astype(BF16)], axis=1)

    _flash_reset(m_ref, l_ref, acc_ref)
    s = lax.dot_general(q, _rows(kcat_ref, i, BIG), _NT, preferred_element_type=F32)
    r = lax.broadcasted_iota(jnp.int32, s.shape, 0)
    c = lax.broadcasted_iota(jnp.int32, s.shape, 1)
    _flash_update(jnp.where(r >= c, s, NEG), _rows(v_ref, i, BIG), m_ref, l_ref, acc_ref)

    def past_tile(t, carry):
        s = lax.dot_general(q, _rows(kcat_ref, t, BIG), _NT, preferred_element_type=F32)
        _flash_update(s, _rows(v_ref, t, BIG), m_ref, l_ref, acc_ref)
        return carry

    lax.fori_loop(0, i, past_tile, 0)
    o_ref[...] = (acc_ref[...] / l_ref[...]).astype(o_ref.dtype)


def _mla_attention(qx, kv, kr, qc_tab, qs_tab, b, s):
    assert s % BIG == 0
    nq = s // BIG
    hh = N_HEADS
    return pl.pallas_call(
        _mla_kernel,
        out_shape=jax.ShapeDtypeStruct((b * s, hh * MLA_V_DIM), BF16),
        grid=(b, hh, nq),
        in_specs=[pl.BlockSpec((BIG, LANES), lambda bi, h, i: (bi * nq + i, h)),
                  pl.BlockSpec((BIG, LANES), lambda bi, h, i: (bi * nq + i, hh + h // 2)),
                  pl.BlockSpec((BIG, LANES), lambda bi, h, i: (bi * nq + i, hh + hh // 2 + h // 2)),
                  pl.BlockSpec((BIG, LANES), lambda bi, h, i: (i, 0)),
                  pl.BlockSpec((BIG, LANES), lambda bi, h, i: (i, 0)),
                  pl.BlockSpec((s, LANES), lambda bi, h, i: (bi, h)),
                  pl.BlockSpec((1, s, LANES), lambda bi, h, i: (h % 2, bi, 0)),
                  pl.BlockSpec((s, LANES), lambda bi, h, i: (bi, hh + h))],
        out_specs=pl.BlockSpec((BIG, LANES), lambda bi, h, i: (bi * nq + i, h)),
        scratch_shapes=[pltpu.VMEM((BIG, LANES), F32), pltpu.VMEM((BIG, LANES), F32),
                        pltpu.VMEM((BIG, MLA_V_DIM), F32), pltpu.VMEM((s, 2 * LANES), BF16)],
        compiler_params=_cparams(("parallel", "parallel", "arbitrary")),
        name="mla_attention",
    )(qx, qx, qx, qc_tab, qs_tab, kv, kr, kv)


def _bf(w):
    return w if isinstance(w, tuple) or w.dtype == BF16 else w.astype(BF16)


def _moba_mixer(xb, w_qkv, rel_bias, b, s):
    hd = N_HEADS * HEAD_DIM
    qkv = _matmul(xb, _bf(w_qkv), BF16, tn=512, scale_cols=hd, scale=HEAD_DIM ** -0.5)
    return _moba_attention(qkv, rel_bias, b, s)


def _swa_mixer(xb, w_qkv, sinks, rel_bias, b, s):
    hd = N_HEADS * HEAD_DIM
    qkv = _matmul(xb, _bf(w_qkv), BF16, tn=512, scale_cols=hd, scale=HEAD_DIM ** -0.5)
    return _swa_attention(qkv, sinks, rel_bias, b, s)


def _nsa_mixer(xb, w_in, w_gate, cmp_pos, cmp_w1, cmp_w2, rel_bias, b, s):
    hd = N_HEADS * HEAD_DIM
    hk_n = NSA_KV_HEADS
    group = N_HEADS // hk_n
    main = hd + 6 * hk_n * HEAD_DIM
    proj = _matmul(xb, _bf(w_in), BF16, tn=512, scale_cols=hd, scale=HEAD_DIM ** -0.5, n_out=main)
    wg = w_gate.reshape(-1, hk_n, 3 * group)
    wg = jnp.pad(wg, ((0, 0), (0, 0), (0, LANES - 3 * group))).reshape(-1, hk_n * LANES)
    gates = _matmul(xb, wg.astype(BF16), F32, tn=hk_n * LANES)
    cmp_kv = _nsa_compress(proj, cmp_pos, _bf(cmp_w1), _bf(cmp_w2), b, s)
    return _nsa_attention(proj, gates, cmp_kv, rel_bias, b, s)


def _mla_mixer(xb, w_down, q_norm, kv_norm, w_uq, w_ukv, b, s):
    hh = N_HEADS
    half = MLA_ROPE_DIM // 2
    lat = MLA_Q_LORA + MLA_KV_LORA
    swap = lambda t: jnp.concatenate([t[..., half:], t[..., :half]], axis=-1)
    w_down_x = jnp.concatenate([w_down, swap(w_down[:, lat:])], axis=1).astype(BF16)
    c = _matmul(xb, w_down_x, F32, tn=w_down_x.shape[1])

    inv = ROPE_BASE ** (-jnp.arange(0, MLA_ROPE_DIM, 2, dtype=F32) / MLA_ROPE_DIM)
    ang = jnp.arange(s, dtype=F32)[:, None] * inv[None, :]
    cos, sin = jnp.cos(ang), jnp.sin(ang)
    ktab = jnp.concatenate([cos, cos, -sin, sin], axis=1)
    qc_tab = jnp.concatenate([cos, cos, cos, cos], axis=1)
    qs_tab = jnp.concatenate([-sin, sin, -sin, sin], axis=1)
    cq, ckv, kr = _mla_prep(c, q_norm, kv_norm, ktab, s)

    wq = w_uq.reshape(MLA_Q_LORA, hh, MLA_NOPE_DIM + MLA_ROPE_DIM)
    wq_rope = wq[:, :, MLA_NOPE_DIM:]
    wq_x = jnp.concatenate([wq[:, :, :MLA_NOPE_DIM].reshape(MLA_Q_LORA, -1),
                            wq_rope.reshape(MLA_Q_LORA, -1),
                            swap(wq_rope).reshape(MLA_Q_LORA, -1)], axis=1).astype(BF16)
    scale = (MLA_NOPE_DIM + MLA_ROPE_DIM) ** -0.5
    qx = _matmul(cq, wq_x, BF16, tn=512, scale_cols=wq_x.shape[1], scale=scale)
    wkv = w_ukv.reshape(MLA_KV_LORA, hh, MLA_NOPE_DIM + MLA_V_DIM)
    wkv_x = jnp.concatenate([wkv[:, :, :MLA_NOPE_DIM].reshape(MLA_KV_LORA, -1),
                             wkv[:, :, MLA_NOPE_DIM:].reshape(MLA_KV_LORA, -1)], axis=1).astype(BF16)
    kv = _matmul(ckv, wkv_x, BF16, tn=512)
    return _mla_attention(qx, kv, kr, qc_tab, qs_tab, b, s)


def kernel(x, rel_bias, moba_w_qkv, moba_w_o, swa_w_qkv, swa_sinks, swa_w_o, nsa_w_in, nsa_cmp_pos,
           nsa_cmp_w1, nsa_cmp_w2, nsa_w_o, mla_w_down, mla_q_norm, mla_kv_norm, mla_w_uq, mla_w_ukv,
           mla_w_o, ln1_g, ln1_b, ffn_w_gate, ffn_w_up, ffn_w_down, ln2_g, ln2_b):
    b, s, d = x.shape
    depth = ln1_g.shape[0]
    n_mixers = 4
    xf = x.reshape(b * s, d).astype(F32)
    xb = xf.astype(BF16)
    moba_qkv_b, moba_o_b = _to_bf16(moba_w_qkv), _to_bf16(moba_w_o)
    swa_qkv_b, swa_o_b = _to_bf16(swa_w_qkv), _to_bf16(swa_w_o)
    nsa_in_b, nsa_o_b, nsa_w1_b = _to_bf16(nsa_w_in), _to_bf16(nsa_w_o), _to_bf16(nsa_cmp_w1)
    mla_o_b = _to_bf16(mla_w_o)
    gate_b, up_b, down_b = _to_bf16(ffn_w_gate), _to_bf16(ffn_w_up), _to_bf16(ffn_w_down)
    nsa_main = (N_HEADS + 6 * NSA_KV_HEADS) * HEAD_DIM
    for i in range(depth):
        kind, j = i % n_mixers, i // n_mixers
        if kind == 0:
            o = _moba_mixer(xb, (moba_qkv_b, j), rel_bias, b, s)
            w_o = (moba_o_b, j)
        elif kind == 1:
            o = _swa_mixer(xb, (swa_qkv_b, j), swa_sinks[j], rel_bias, b, s)
            w_o = (swa_o_b, j)
        elif kind == 2:
            o = _nsa_mixer(xb, (nsa_in_b, j), nsa_w_in[j][:, nsa_main:], nsa_cmp_pos[j], nsa_w1_b[j],
                           nsa_cmp_w2[j], rel_bias, b, s)
            w_o = (nsa_o_b, j)
        else:
            o = _mla_mixer(xb, mla_w_down[j], mla_q_norm[j], mla_kv_norm[j], mla_w_uq[j], mla_w_ukv[j], b, s)
            w_o = (mla_o_b, j)
        xf, xb = _proj_ln(o, w_o, xf, ln1_g[i], ln1_b[i])
        hmid = _ffn_up(xb, (gate_b, i), (up_b, i))
        xf, xb = _proj_ln(hmid, (down_b, i), xf, ln2_g[i], ln2_b[i])
    return xf.reshape(b, s, d).astype(x.dtype)
```

```python
import functools
import math

import jax
import jax.numpy as jnp
import numpy as np
from jax import lax
from jax.experimental import pallas as pl
from jax.experimental.pallas import tpu as pltpu

F32 = jnp.float32
BF16 = jnp.bfloat16

N_HEADS = 16
HEAD_DIM = 128
REL_BUCKETS = 32
REL_MAX_DIST = 128
MOBA_BLOCK = 256
MOBA_TOPK = 3
SWA_WINDOW = 128
SWA_KV_HEADS = 2
NSA_KV_HEADS = 4
NSA_CMP_LEN = 32
NSA_CMP_STRIDE = 16
NSA_CMP_HIDDEN = 256
NSA_SEL_BLOCK = 64
NSA_TOPN = 16
NSA_WINDOW = 512
MLA_Q_LORA = 512
MLA_KV_LORA = 512
MLA_NOPE_DIM = 128
MLA_ROPE_DIM = 64
MLA_V_DIM = 128
ROPE_BASE = 10000.0
DEPTH = 4
ALPHA = (2 * DEPTH) ** 0.25
LN_EPS = 1e-5
RMS_EPS = 1e-6

LANES = 128
TILE = 256
BIG = 2 * TILE
NEG = -1e30
VMEM_LIMIT = 56 * 1024 * 1024

_NT = (((1,), (1,)), ((), ()))


def _cparams(sem):
    return pltpu.CompilerParams(dimension_semantics=sem, vmem_limit_bytes=VMEM_LIMIT)


def _matmul_kernel(x_ref, w_ref, o_ref, *, scale_tiles, scale):
    acc = jnp.dot(x_ref[...], w_ref[...].astype(BF16), preferred_element_type=F32)
    if scale_tiles:
        acc = acc * jnp.where(pl.program_id(1) < scale_tiles, scale, 1.0).astype(F32)
    o_ref[...] = acc.astype(o_ref.dtype)


CAST_BLOCK_BYTES = 6 * 1024 * 1024


def _cast_kernel(x_ref, o_ref):
    o_ref[...] = x_ref[...].astype(o_ref.dtype)


def _to_bf16(w):
    w2 = w.reshape(-1, w.shape[-1])
    rows, cols = w2.shape
    tr = rows
    while tr % 2 == 0 and tr > 16 and tr * cols * 4 > CAST_BLOCK_BYTES:
        tr //= 2
    out = pl.pallas_call(
        _cast_kernel,
        out_shape=jax.ShapeDtypeStruct((rows, cols), BF16),
        grid=(rows // tr,),
        in_specs=[pl.BlockSpec((tr, cols), lambda i: (i, 0))],
        out_specs=pl.BlockSpec((tr, cols), lambda i: (i, 0)),
        compiler_params=_cparams(("parallel",)),
        name="cast_bf16",
    )(w2)
    return out.reshape(w.shape)


def _wspec(w, rows, cols, index):
    if isinstance(w, tuple):
        layer = w[1]
        return w[0], pl.BlockSpec((None, rows, cols), lambda i, j: (layer,) + index(i, j))
    return w, pl.BlockSpec((rows, cols), index)


def _wshape(w):
    return w[0].shape[1:] if isinstance(w, tuple) else w.shape


def _matmul(x, w, out_dtype, *, tn, tm=1024, scale_cols=0, scale=1.0, n_out=None):
    m, k = x.shape
    n = _wshape(w)[1] if n_out is None else n_out
    w, w_spec = _wspec(w, k, tn, lambda i, j: (0, j))
    tm = min(tm, m)
    assert m % tm == 0 and n % tn == 0 and scale_cols % tn == 0
    kern = functools.partial(_matmul_kernel, scale_tiles=scale_cols // tn, scale=scale)
    return pl.pallas_call(
        kern,
        out_shape=jax.ShapeDtypeStruct((m, n), out_dtype),
        grid=(m // tm, n // tn),
        in_specs=[pl.BlockSpec((tm, k), lambda i, j: (i, 0)), w_spec],
        out_specs=pl.BlockSpec((tm, tn), lambda i, j: (i, j)),
        compiler_params=_cparams(("parallel", "arbitrary")),
        name="proj_matmul",
    )(x, w)


LN_SUB = 128
ACC_SUB = 512


def _proj_ln_kernel(a_ref, w_ref, x_ref, g_ref, b_ref, o_ref, ob_ref, *, nk):
    kk = pl.program_id(1)
    tm = a_ref.shape[0]

    def part(rows):
        return jnp.dot(a_ref[rows, :], w_ref[...], preferred_element_type=F32)

    def finish(rows, y):
        mu = jnp.mean(y, axis=-1, keepdims=True)
        yc = y - mu
        var = jnp.mean(yc * yc, axis=-1, keepdims=True)
        out = yc * lax.rsqrt(var + LN_EPS) * g_ref[...] + b_ref[...]
        o_ref[rows, :] = out
        ob_ref[rows, :] = out.astype(BF16)

    subs = [pl.ds(r, LN_SUB) for r in range(0, tm, LN_SUB)]
    if nk == 1:
        for rows in subs:
            finish(rows, ALPHA * x_ref[rows, :] + part(rows))
        return

    acc_subs = [pl.ds(r, min(ACC_SUB, tm)) for r in range(0, tm, ACC_SUB)]

    @pl.when(kk == 0)
    def _():
        for rows in acc_subs:
            o_ref[rows, :] = part(rows)

    @pl.when((kk > 0) & (kk < nk - 1))
    def _():
        for rows in acc_subs:
            o_ref[rows, :] += part(rows)

    @pl.when(kk == nk - 1)
    def _():
        for rows in subs:
            finish(rows, ALPHA * x_ref[rows, :] + (o_ref[rows, :] + part(rows)))


MAX_TK = 2048


def _proj_ln(a, w, x, g, b, *, tm=512, tk=None):
    m, k = a.shape
    d = _wshape(w)[1]
    tm = min(tm, m)
    if tk is None:
        tk = max(t for t in range(LANES, min(k, MAX_TK) + 1, LANES) if k % t == 0)
    assert m % tm == 0 and k % tk == 0 and tm % LN_SUB == 0
    w, w_spec = _wspec(w, tk, d, lambda i, j: (j, 0))
    return pl.pallas_call(
        functools.partial(_proj_ln_kernel, nk=k // tk),
        out_shape=(jax.ShapeDtypeStruct((m, d), F32), jax.ShapeDtypeStruct((m, d), BF16)),
        grid=(m // tm, k // tk),
        in_specs=[pl.BlockSpec((tm, tk), lambda i, j: (i, j)),
                  w_spec,
                  pl.BlockSpec((tm, d), lambda i, j: (i, 0)),
                  pl.BlockSpec((1, d), lambda i, j: (0, 0)),
                  pl.BlockSpec((1, d), lambda i, j: (0, 0))],
        out_specs=(pl.BlockSpec((tm, d), lambda i, j: (i, 0)),
                   pl.BlockSpec((tm, d), lambda i, j: (i, 0))),
        compiler_params=_cparams(("parallel", "arbitrary")),
        name="proj_deepnorm",
    )(a, w, x, g.reshape(1, d), b.reshape(1, d))


def _ffn_up_kernel(x_ref, wg_ref, wu_ref, o_ref):
    x = x_ref[...]
    gate = jnp.dot(x, wg_ref[...].astype(BF16), preferred_element_type=F32)
    up = jnp.dot(x, wu_ref[...].astype(BF16), preferred_element_type=F32)
    o_ref[...] = (gate * jax.nn.sigmoid(gate) * up).astype(o_ref.dtype)


def _ffn_up(x, wg, wu, *, tm=1024, tn=512):
    m, k = x.shape
    f = _wshape(wg)[1]
    tm = min(tm, m)
    tn = min(tn, f)
    assert m % tm == 0 and f % tn == 0
    wg, wg_spec = _wspec(wg, k, tn, lambda i, j: (0, j))
    wu, wu_spec = _wspec(wu, k, tn, lambda i, j: (0, j))
    return pl.pallas_call(
        _ffn_up_kernel,
        out_shape=jax.ShapeDtypeStruct((m, f), BF16),
        grid=(m // tm, f // tn),
        in_specs=[pl.BlockSpec((tm, k), lambda i, j: (i, 0)), wg_spec, wu_spec],
        out_specs=pl.BlockSpec((tm, tn), lambda i, j: (i, j)),
        compiler_params=_cparams(("parallel", "arbitrary")),
        name="ffn_gate_up",
    )(x, wg, wu)


def _t5_bucket(dist):
    n = np.maximum(dist, 0)
    max_exact = REL_BUCKETS // 2
    nf = np.maximum(n, max_exact).astype(np.float32)
    large = max_exact + (np.log(nf / max_exact) / math.log(REL_MAX_DIST / max_exact)
                         * (REL_BUCKETS - max_exact)).astype(np.int32)
    return np.where(n < max_exact, n, np.minimum(large, REL_BUCKETS - 1)).astype(np.int32)


def _bias_lookup(rel_bias, dist, visible):
    bucket = _t5_bucket(dist).reshape(-1)
    onehot = (jnp.arange(REL_BUCKETS, dtype=jnp.int32)[:, None] == jnp.asarray(bucket)[None, :]).astype(F32)
    tab = jnp.dot(rel_bias.T.astype(F32), onehot, precision=lax.Precision.HIGHEST)
    tab = tab.reshape((rel_bias.shape[1],) + dist.shape)
    return jnp.where(jnp.asarray(visible)[None], tab, NEG)


def _bias_tiles(rel_bias, window=None):
    r = np.arange(TILE)[:, None]
    c = np.arange(TILE)[None, :]
    d0 = r - c
    d1 = TILE + r - c
    ok0 = d0 >= 0
    ok1 = np.ones_like(ok0)
    if window is not None:
        ok0 = ok0 & (d0 < window)
        ok1 = d1 < window
    t0 = _bias_lookup(rel_bias, d0, ok0)
    t1 = _bias_lookup(rel_bias, d1, ok1)
    far = rel_bias[int(_t5_bucket(np.asarray(2 * TILE)))].astype(F32)
    return t0, t1, far


def _flash_update(s, v, m_ref, l_ref, acc_ref):
    m_prev = m_ref[...]
    m_new = jnp.maximum(m_prev, jnp.max(s, axis=1, keepdims=True))
    alpha = jnp.exp(m_prev - m_new)
    p = jnp.exp(s - jnp.concatenate([m_new] * (s.shape[1] // LANES), axis=1))
    l_ref[...] = alpha * l_ref[...] + jnp.sum(p, axis=1, keepdims=True)
    acc_ref[...] = alpha * acc_ref[...] + jnp.dot(p.astype(BF16), v, preferred_element_type=F32)
    m_ref[...] = m_new


def _flash_reset(m_ref, l_ref, acc_ref):
    m_ref[...] = jnp.full(m_ref.shape, NEG, F32)
    l_ref[...] = jnp.zeros(l_ref.shape, F32)
    acc_ref[...] = jnp.zeros(acc_ref.shape, F32)


def _rows(ref, t, tile=TILE):
    return ref[pl.ds(pl.multiple_of(t * tile, tile), tile), :]


def _split3(x):
    hi = x.astype(BF16)
    r1 = x - hi.astype(F32)
    mid = r1.astype(BF16)
    lo = (r1 - mid.astype(F32)).astype(BF16)
    return hi, mid, lo


def _rank_before(val, idx, n):
    cnt = jnp.zeros(val.shape, jnp.int32)
    for m in range(n):
        vm = val[m:m + 1, :]
        beats = (vm > val) | ((vm == val) & (m < idx))
        cnt = cnt + beats.astype(jnp.int32)
    return cnt


def _chosen_rows(chosen_t):
    n, q = chosen_t.shape
    padded = jnp.concatenate([chosen_t, jnp.zeros((LANES - n, q), F32)], axis=0)
    return padded.T.astype(BF16)


def _moba_kernel(q_ref, k_ref, v_ref, tab_ref, ex_ref, o_ref, m_ref, l_ref, acc_ref, km_ref, *, nblk):
    ci = pl.program_id(2)
    q = q_ref[...]

    @pl.when(ci == 0)
    def _():
        kmean = jnp.concatenate(
            [jnp.mean(k_ref[n * TILE:(n + 1) * TILE, :].astype(F32), axis=0, keepdims=True)
             for n in range(nblk)]
            + [jnp.zeros((LANES - nblk, HEAD_DIM), F32)], axis=0)
        for j, part in enumerate(_split3(kmean)):
            km_ref[j] = part

    nrow = -(-nblk // 8) * 8
    gate = sum(lax.dot_general(km_ref[j], q, _NT, preferred_element_type=F32) for j in range(3))[:nrow]
    blk = lax.broadcasted_iota(jnp.int32, gate.shape, 0)
    qblk = 2 * ci + lax.broadcasted_iota(jnp.int32, gate.shape, 1) // TILE
    past = blk < qblk
    gate = jnp.where(past, gate, -jnp.inf)
    cnt = _rank_before(gate, blk, nblk)
    chosen = _chosen_rows(jnp.where((past & (cnt < MOBA_TOPK)) | (blk == qblk), 1.0, 0.0))

    def chunk(c, carry):
        rows = pl.ds(pl.multiple_of(c * BIG, BIG), BIG)
        s = lax.dot_general(q, k_ref[rows, :], _NT, preferred_element_type=F32)
        hit = jnp.dot(chosen, ex_ref[c], preferred_element_type=F32)
        s = s + tab_ref[0, jnp.minimum(ci - c, 2)] + (hit - 1.0) * (-NEG)
        _flash_update(s, v_ref[rows, :], m_ref, l_ref, acc_ref)
        return carry

    _flash_reset(m_ref, l_ref, acc_ref)
    chunk(ci, 0)
    lax.fori_loop(0, ci, chunk, 0)
    o_ref[...] = (acc_ref[...] / l_ref[...]).astype(o_ref.dtype)


def _moba_attention(qkv, rel_bias, b, s):
    assert MOBA_BLOCK == TILE and s % BIG == 0
    nblk = s // TILE
    nq = s // BIG
    assert max(1, min(MOBA_TOPK, nblk - 1)) == MOBA_TOPK and nblk <= LANES
    hh = N_HEADS
    t0, t1, far = _bias_tiles(rel_bias)
    ff = jnp.broadcast_to(far[:, None, None], t0.shape)
    zz = jnp.zeros_like(t0)
    blk = lambda a, b_, c, d: jnp.concatenate([jnp.concatenate([a, b_], axis=2),
                                               jnp.concatenate([c, d], axis=2)], axis=1)
    tab = jnp.stack([blk(t0, zz, t1, t0), blk(ff, t1, ff, ff), blk(ff, ff, ff, ff)], axis=1)
    expand = jnp.asarray(np.arange(LANES)[None, :, None]
                         == (2 * np.arange(nq)[:, None, None] + np.arange(BIG)[None, None, :] // TILE), BF16)
    kern = functools.partial(_moba_kernel, nblk=nblk)
    return pl.pallas_call(
        kern,
        out_shape=jax.ShapeDtypeStruct((b * s, hh * HEAD_DIM), BF16),
        grid=(b, hh, nq),
        in_specs=[pl.BlockSpec((BIG, HEAD_DIM), lambda bi, h, i: (bi * nq + i, h)),
                  pl.BlockSpec((s, HEAD_DIM), lambda bi, h, i: (bi, hh + h)),
                  pl.BlockSpec((s, HEAD_DIM), lambda bi, h, i: (bi, 2 * hh + h)),
                  pl.BlockSpec((1, 3, BIG, BIG), lambda bi, h, i: (h, 0, 0, 0)),
                  pl.BlockSpec((nq, LANES, BIG), lambda bi, h, i: (0, 0, 0))],
        out_specs=pl.BlockSpec((BIG, HEAD_DIM), lambda bi, h, i: (bi * nq + i, h)),
        scratch_shapes=[pltpu.VMEM((BIG, LANES), F32), pltpu.VMEM((BIG, LANES), F32),
                        pltpu.VMEM((BIG, HEAD_DIM), F32), pltpu.VMEM((3, LANES, HEAD_DIM), BF16)],
        compiler_params=_cparams(("parallel", "parallel", "arbitrary")),
        name="moba_attention",
    )(qkv, qkv, qkv, tab, expand)


def _swa_kernel(sink_ref, q_ref, k_ref, v_ref, t0_ref, t1_ref, o_ref, m_ref, l_ref, acc_ref, *, group):
    hk = pl.program_id(0)
    i = pl.program_id(2)
    q = jnp.concatenate([q_ref[:, g * HEAD_DIM:(g + 1) * HEAD_DIM] for g in range(group)], axis=0)

    for g in range(group):
        m_ref[g * TILE:(g + 1) * TILE, :] = jnp.full((TILE, LANES), sink_ref[hk * group + g], F32)
    l_ref[...] = jnp.ones(l_ref.shape, F32)
    acc_ref[...] = jnp.zeros(acc_ref.shape, F32)

    s = lax.dot_general(q, _rows(k_ref, i), _NT, preferred_element_type=F32)
    s = s + t0_ref[...].reshape(group * TILE, TILE)
    _flash_update(s, _rows(v_ref, i), m_ref, l_ref, acc_ref)

    @pl.when(i > 0)
    def _():
        s = lax.dot_general(q, _rows(k_ref, i - 1), _NT, preferred_element_type=F32)
        s = s + t1_ref[...].reshape(group * TILE, TILE)
        _flash_update(s, _rows(v_ref, i - 1), m_ref, l_ref, acc_ref)

    out = acc_ref[...] / l_ref[...]
    for g in range(group):
        o_ref[:, g * HEAD_DIM:(g + 1) * HEAD_DIM] = out[g * TILE:(g + 1) * TILE].astype(o_ref.dtype)


def _swa_attention(qkv, sinks, rel_bias, b, s):
    assert SWA_WINDOW <= TILE and s % TILE == 0
    nq = s // TILE
    hk_n = SWA_KV_HEADS
    group = N_HEADS // hk_n
    t0, t1, _ = _bias_tiles(rel_bias, window=SWA_WINDOW)
    kern = functools.partial(_swa_kernel, group=group)
    gw = group * HEAD_DIM
    return pl.pallas_call(
        kern,
        out_shape=jax.ShapeDtypeStruct((b * s, N_HEADS * HEAD_DIM), BF16),
        grid=(hk_n, b, nq),
        in_specs=[pl.BlockSpec(memory_space=pltpu.SMEM),
                  pl.BlockSpec((TILE, gw), lambda hk, bi, i: (bi * nq + i, hk)),
                  pl.BlockSpec((s, HEAD_DIM), lambda hk, bi, i: (bi, N_HEADS + hk)),
                  pl.BlockSpec((s, HEAD_DIM), lambda hk, bi, i: (bi, N_HEADS + hk_n + hk)),
                  pl.BlockSpec((group, TILE, TILE), lambda hk, bi, i: (hk, 0, 0)),
                  pl.BlockSpec((group, TILE, TILE), lambda hk, bi, i: (hk, 0, 0))],
        out_specs=pl.BlockSpec((TILE, gw), lambda hk, bi, i: (bi * nq + i, hk)),
        scratch_shapes=[pltpu.VMEM((group * TILE, LANES), F32), pltpu.VMEM((group * TILE, LANES), F32),
                        pltpu.VMEM((group * TILE, HEAD_DIM), F32)],
        compiler_params=_cparams(("parallel", "parallel", "arbitrary")),
        name="swa_attention",
    )(sinks.astype(F32), qkv, qkv, qkv, t0, t1)


def _gelu_tanh(x):
    return 0.5 * x * (1.0 + jnp.tanh(math.sqrt(2.0 / math.pi) * (x + 0.044715 * (x * x * x))))


def _nsa_compress_kernel(t_ref, pe_ref, w1_ref, w2_ref, o_ref, tf_ref, *, ncmp_pad):
    stride = NSA_CMP_STRIDE
    half = stride * HEAD_DIM
    pe = pe_ref[0]
    tf_ref[...] = t_ref[...].astype(F32)
    xa, xb = [], []
    for r in range(stride):
        x = tf_ref[pl.ds(r, ncmp_pad, stride=stride), :]
        xa.append((x + pe[r:r + 1, :]).astype(BF16))
        xb.append((x + pe[stride + r:stride + r + 1, :]).astype(BF16))
    a = jnp.dot(jnp.concatenate(xa, axis=1), w1_ref[0, :half, :], preferred_element_type=F32)
    bm = jnp.dot(jnp.concatenate(xb, axis=1), w1_ref[0, half:, :], preferred_element_type=F32)
    hid = _gelu_tanh(a + pltpu.roll(bm, ncmp_pad - 1, 0))
    o_ref[0, 0, 0] = jnp.dot(hid.astype(BF16), w2_ref[0], preferred_element_type=F32).astype(o_ref.dtype)


def _nsa_compress(proj, pe, w1, w2, b, s):
    assert NSA_CMP_LEN == 2 * NSA_CMP_STRIDE and s % NSA_CMP_STRIDE == 0
    ncp = s // NSA_CMP_STRIDE
    hk_n = NSA_KV_HEADS
    kern = functools.partial(_nsa_compress_kernel, ncmp_pad=ncp)
    return pl.pallas_call(
        kern,
        out_shape=jax.ShapeDtypeStruct((2, b, hk_n, ncp, HEAD_DIM), BF16),
        grid=(2, b, hk_n),
        in_specs=[pl.BlockSpec((s, HEAD_DIM), lambda kv, bi, hk: (bi, N_HEADS + kv * hk_n + hk)),
                  pl.BlockSpec((1, NSA_CMP_LEN, HEAD_DIM), lambda kv, bi, hk: (kv, 0, 0)),
                  pl.BlockSpec((1, NSA_CMP_LEN * HEAD_DIM, NSA_CMP_HIDDEN), lambda kv, bi, hk: (kv, 0, 0)),
                  pl.BlockSpec((1, NSA_CMP_HIDDEN, HEAD_DIM), lambda kv, bi, hk: (kv, 0, 0))],
        out_specs=pl.BlockSpec((1, 1, 1, ncp, HEAD_DIM), lambda kv, bi, hk: (kv, bi, hk, 0, 0)),
        scratch_shapes=[pltpu.VMEM((s, HEAD_DIM), F32)],
        compiler_params=_cparams(("parallel", "parallel", "arbitrary")),
        name="nsa_compress",
    )(proj, pe.astype(F32), w1, w2)


def _nsa_kernel(far_ref, q_ref, kc_ref, vc_ref, cb_ref, ksl_ref, vsl_ref, kw_ref, vw_ref, gate_ref,
                t0_ref, t1_ref, tw_ref, ov_ref, ex_ref, o_ref,
                m_ref, l_ref, acc_ref, sel_ref, mw_ref, lw_ref, accw_ref, *, group, nq):
    hk = pl.program_id(1)
    i = pl.program_id(2)
    rows = group * TILE
    q = jnp.concatenate([q_ref[:, g * HEAD_DIM:(g + 1) * HEAD_DIM] for g in range(group)], axis=0)

    cb = cb_ref[...].reshape(rows, LANES)
    sc = lax.dot_general(q, kc_ref[0, 0, 0], _NT, preferred_element_type=F32) + cb
    valid = cb > 0.5 * NEG
    mc = jnp.max(sc, axis=1, keepdims=True)
    pc = jnp.where(valid, jnp.exp(sc - mc), 0.0)
    pc = pc / jnp.maximum(jnp.sum(pc, axis=1, keepdims=True), 1e-30)
    o_cmp = jnp.dot(pc.astype(BF16), vc_ref[0, 0, 0], preferred_element_type=F32)

    psum = pc[0:TILE]
    for g in range(1, group):
        psum = psum + pc[g * TILE:(g + 1) * TILE]
    n_sel = nq * (TILE // NSA_SEL_BLOCK)
    imp = sum(lax.dot_general(ov_ref[...], part, _NT, preferred_element_type=F32)
              for part in _split3(psum))[:n_sel]
    blk = lax.broadcasted_iota(jnp.int32, imp.shape, 0)
    qblk = (i * TILE + lax.broadcasted_iota(jnp.int32, imp.shape, 1)) // NSA_SEL_BLOCK
    forced = (blk == 0) | (blk == qblk) | (blk == qblk - 1)
    imp = jnp.where(blk > qblk, -jnp.inf, jnp.where(forced, jnp.inf, imp))
    cnt = _rank_before(imp, blk, n_sel)
    chosen = _chosen_rows(jnp.where((blk <= qblk) & (cnt < NSA_TOPN), 1.0, 0.0))
    for t in range(nq):
        @pl.when(t <= i)
        def _(t=t):
            hit = jnp.dot(chosen, ex_ref[:, t * TILE:(t + 1) * TILE], preferred_element_type=F32)
            sel_ref[t] = (hit - 1.0) * (-NEG)

    t0 = t0_ref[...].reshape(rows, TILE)

    def masked(s, t, with_far=False):
        sel = sel_ref[t]
        parts = []
        for g in range(group):
            add = sel + far_ref[hk * group + g] if with_far else sel
            parts.append(s[g * TILE:(g + 1) * TILE] + add)
        return jnp.concatenate(parts, axis=0)

    slc = (m_ref, l_ref, acc_ref)
    win = (mw_ref, lw_ref, accw_ref)
    _flash_reset(*slc)
    _flash_reset(*win)

    def logits(k_ref, t):
        return lax.dot_general(q, _rows(k_ref, t), _NT, preferred_element_type=F32)

    _flash_update(masked(logits(ksl_ref, i) + t0, i), _rows(vsl_ref, i), *slc)
    _flash_update(logits(kw_ref, i) + t0, _rows(vw_ref, i), *win)

    @pl.when(i > 0)
    def _():
        t1 = t1_ref[...].reshape(rows, TILE)
        _flash_update(masked(logits(ksl_ref, i - 1) + t1, i - 1), _rows(vsl_ref, i - 1), *slc)
        _flash_update(logits(kw_ref, i - 1) + t1, _rows(vw_ref, i - 1), *win)

    @pl.when(i > 1)
    def _():
        _flash_update(masked(logits(ksl_ref, i - 2), i - 2, with_far=True), _rows(vsl_ref, i - 2), *slc)
        _flash_update(logits(kw_ref, i - 2) + tw_ref[...].reshape(rows, TILE), _rows(vw_ref, i - 2), *win)

    def far_tile(t, carry):
        _flash_update(masked(logits(ksl_ref, t), t, with_far=True), _rows(vsl_ref, t), *slc)
        return carry

    lax.fori_loop(0, jnp.maximum(i - 2, 0), far_tile, 0)
    o_slc = acc_ref[...] / l_ref[...]
    o_win = accw_ref[...] / lw_ref[...]
    gates = jax.nn.sigmoid(gate_ref[...])
    for g in range(group):
        sl = slice(g * TILE, (g + 1) * TILE)
        o = (gates[:, 3 * g:3 * g + 1] * o_cmp[sl] + gates[:, 3 * g + 1:3 * g + 2] * o_slc[sl]
             + gates[:, 3 * g + 2:3 * g + 3] * o_win[sl])
        o_ref[:, g * HEAD_DIM:(g + 1) * HEAD_DIM] = o.astype(o_ref.dtype)


def _nsa_attention(proj, gates, cmp_kv, rel_bias, b, s):
    assert s % TILE == 0 and NSA_WINDOW == 2 * TILE and TILE % NSA_SEL_BLOCK == 0
    nq = s // TILE
    hk_n = NSA_KV_HEADS
    group = N_HEADS // hk_n
    gw = group * HEAD_DIM
    ncp = s // NSA_CMP_STRIDE
    n_cmp = (s - NSA_CMP_LEN) // NSA_CMP_STRIDE + 1
    n_sel = s // NSA_SEL_BLOCK
    assert ncp == LANES and n_sel <= LANES and min(NSA_TOPN, n_sel) == NSA_TOPN
    t0, t1, far = _bias_tiles(rel_bias)
    r = np.arange(TILE)[:, None]
    c = np.arange(TILE)[None, :]
    tw = jnp.where(jnp.asarray(c > r)[None], far[:, None, None], NEG)
    pos = np.arange(s)[:, None]
    cidx = np.arange(ncp)[None, :]
    cdist = pos - (cidx * NSA_CMP_STRIDE + NSA_CMP_LEN - 1)
    cmp_bias = _bias_lookup(rel_bias, cdist, (cdist >= 0) & (cidx < n_cmp))
    ci = np.arange(ncp)[:, None] * NSA_CMP_STRIDE
    sj = np.arange(LANES)[None, :] * NSA_SEL_BLOCK
    overlap = jnp.asarray(((ci < sj + NSA_SEL_BLOCK) & (ci + NSA_CMP_LEN > sj)
                           & (np.arange(ncp)[:, None] < n_cmp) & (np.arange(LANES)[None, :] < n_sel)).T, BF16)
    expand = jnp.asarray(np.arange(LANES)[:, None] == (np.arange(s)[None, :] // NSA_SEL_BLOCK), BF16)

    kern = functools.partial(_nsa_kernel, group=group, nq=nq)
    kv_spec = lambda off: pl.BlockSpec((s, HEAD_DIM), lambda bi, hk, i: (bi, N_HEADS + off * hk_n + hk))
    cmp_spec = lambda kv: pl.BlockSpec((1, 1, 1, ncp, HEAD_DIM), lambda bi, hk, i: (kv, bi, hk, 0, 0))
    tile_spec = pl.BlockSpec((group, TILE, TILE), lambda bi, hk, i: (hk, 0, 0))
    return pl.pallas_call(
        kern,
        out_shape=jax.ShapeDtypeStruct((b * s, N_HEADS * HEAD_DIM), BF16),
        grid=(b, hk_n, nq),
        in_specs=[pl.BlockSpec(memory_space=pltpu.SMEM),
                  pl.BlockSpec((TILE, gw), lambda bi, hk, i: (bi * nq + i, hk)),
                  cmp_spec(0), cmp_spec(1),
                  pl.BlockSpec((group, TILE, ncp), lambda bi, hk, i: (hk, i, 0)),
                  kv_spec(2), kv_spec(3), kv_spec(4), kv_spec(5),
                  pl.BlockSpec((TILE, LANES), lambda bi, hk, i: (bi * nq + i, hk)),
                  tile_spec, tile_spec, tile_spec,
                  pl.BlockSpec((ncp, LANES), lambda bi, hk, i: (0, 0)),
                  pl.BlockSpec((LANES, s), lambda bi, hk, i: (0, 0))],
        out_specs=pl.BlockSpec((TILE, gw), lambda bi, hk, i: (bi * nq + i, hk)),
        scratch_shapes=[pltpu.VMEM((group * TILE, LANES), F32), pltpu.VMEM((group * TILE, LANES), F32),
                        pltpu.VMEM((group * TILE, HEAD_DIM), F32),
                        pltpu.VMEM((nq, TILE, TILE), F32),
                        pltpu.VMEM((group * TILE, LANES), F32), pltpu.VMEM((group * TILE, LANES), F32),
                        pltpu.VMEM((group * TILE, HEAD_DIM), F32)],
        compiler_params=_cparams(("parallel", "parallel", "arbitrary")),
        name="nsa_attention",
    )(far, proj, cmp_kv, cmp_kv, cmp_bias, proj, proj, proj, proj, gates, t0, t1, tw, overlap, expand)


def _rms(x, g):
    return x * lax.rsqrt(jnp.mean(x * x, axis=-1, keepdims=True) + RMS_EPS) * g


def _mla_prep_kernel(c_ref, qg_ref, kg_ref, tab_ref, cq_ref, ckv_ref, kr_ref):
    c = c_ref[...]
    cq_ref[...] = _rms(c[:, :MLA_Q_LORA], qg_ref[...]).astype(BF16)
    ckv_ref[...] = _rms(c[:, MLA_Q_LORA:MLA_Q_LORA + MLA_KV_LORA], kg_ref[...]).astype(BF16)
    y = c[:, MLA_Q_LORA + MLA_KV_LORA:] * tab_ref[...]
    y = y + pltpu.roll(y, MLA_ROPE_DIM, 1)
    lane = lax.broadcasted_iota(jnp.int32, y.shape, 1)
    kr_ref[0] = jnp.where(lane < MLA_ROPE_DIM, y, 0.0).astype(BF16)
    kr_ref[1] = jnp.where(lane >= MLA_ROPE_DIM, y, 0.0).astype(BF16)


def _mla_prep(c, q_norm, kv_norm, ktab, s, *, tm=512):
    m, w = c.shape
    tm = min(tm, s)
    assert 2 * MLA_ROPE_DIM == LANES and w == MLA_Q_LORA + MLA_KV_LORA + LANES and s % tm == 0
    ns = s // tm
    return pl.pallas_call(
        _mla_prep_kernel,
        out_shape=(jax.ShapeDtypeStruct((m, MLA_Q_LORA), BF16),
                   jax.ShapeDtypeStruct((m, MLA_KV_LORA), BF16),
                   jax.ShapeDtypeStruct((2, m, LANES), BF16)),
        grid=(m // tm,),
        in_specs=[pl.BlockSpec((tm, w), lambda i: (i, 0)),
                  pl.BlockSpec((1, MLA_Q_LORA), lambda i: (0, 0)),
                  pl.BlockSpec((1, MLA_KV_LORA), lambda i: (0, 0)),
                  pl.BlockSpec((tm, LANES), lambda i: (i % ns, 0))],
        out_specs=(pl.BlockSpec((tm, MLA_Q_LORA), lambda i: (i, 0)),
                   pl.BlockSpec((tm, MLA_KV_LORA), lambda i: (i, 0)),
                   pl.BlockSpec((2, tm, LANES), lambda i: (0, i, 0))),
        compiler_params=_cparams(("parallel",)),
        name="mla_prep",
    )(c, q_norm.reshape(1, -1).astype(F32), kv_norm.reshape(1, -1).astype(F32), ktab)


def _mla_kernel(qn_ref, qr_ref, qs_ref, ct_ref, st_ref, kn_ref, kr_ref, v_ref, o_ref,
                m_ref, l_ref, acc_ref, kcat_ref):
    i = pl.program_id(2)

    @pl.when(i == 0)
    def _():
        kcat_ref[:, :HEAD_DIM] = kn_ref[...]
        kcat_ref[:, HEAD_DIM:] = kr_ref[0]

    qr = qr_ref[...].astype(F32) * ct_ref[...] + qs_ref[...].astype(F32) * st_ref[...]
    q = jnp.concatenate([qn_ref[...], qr.astype(BF16)], axis=1)

    _flash_reset(m_ref, l_ref, acc_ref)
    s = lax.dot_general(q, _rows(kcat_ref, i, BIG), _NT, preferred_element_type=F32)
    r = lax.broadcasted_iota(jnp.int32, s.shape, 0)
    c = lax.broadcasted_iota(jnp.int32, s.shape, 1)
    _flash_update(jnp.where(r >= c, s, NEG), _rows(v_ref, i, BIG), m_ref, l_ref, acc_ref)

    def past_tile(t, carry):
        s = lax.dot_general(q, _rows(kcat_ref, t, BIG), _NT, preferred_element_type=F32)
        _flash_update(s, _rows(v_ref, t, BIG), m_ref, l_ref, acc_ref)
        return carry

    lax.fori_loop(0, i, past_tile, 0)
    o_ref[...] = (acc_ref[...] / l_ref[...]).astype(o_ref.dtype)


def _mla_attention(qx, kv, kr, qc_tab, qs_tab, b, s):
    assert s % BIG == 0
    nq = s // BIG
    hh = N_HEADS
    return pl.pallas_call(
        _mla_kernel,
        out_shape=jax.ShapeDtypeStruct((b * s, hh * MLA_V_DIM), BF16),
        grid=(b, hh, nq),
        in_specs=[pl.BlockSpec((BIG, LANES), lambda bi, h, i: (bi * nq + i, h)),
                  pl.BlockSpec((BIG, LANES), lambda bi, h, i: (bi * nq + i, hh + h // 2)),
                  pl.BlockSpec((BIG, LANES), lambda bi, h, i: (bi * nq + i, hh + hh // 2 + h // 2)),
                  pl.BlockSpec((BIG, LANES), lambda bi, h, i: (i, 0)),
                  pl.BlockSpec((BIG, LANES), lambda bi, h, i: (i, 0)),
                  pl.BlockSpec((s, LANES), lambda bi, h, i: (bi, h)),
                  pl.BlockSpec((1, s, LANES), lambda bi, h, i: (h % 2, bi, 0)),
                  pl.BlockSpec((s, LANES), lambda bi, h, i: (bi, hh + h))],
        out_specs=pl.BlockSpec((BIG, LANES), lambda bi, h, i: (bi * nq + i, h)),
        scratch_shapes=[pltpu.VMEM((BIG, LANES), F32), pltpu.VMEM((BIG, LANES), F32),
                        pltpu.VMEM((BIG, MLA_V_DIM), F32), pltpu.VMEM((s, 2 * LANES), BF16)],
        compiler_params=_cparams(("parallel", "parallel", "arbitrary")),
        name="mla_attention",
    )(qx, qx, qx, qc_tab, qs_tab, kv, kr, kv)


def _bf(w):
    return w if isinstance(w, tuple) or w.dtype == BF16 else w.astype(BF16)


def _moba_mixer(xb, w_qkv, rel_bias, b, s):
    hd = N_HEADS * HEAD_DIM
    qkv = _matmul(xb, _bf(w_qkv), BF16, tn=512, scale_cols=hd, scale=HEAD_DIM ** -0.5)
    return _moba_attention(qkv, rel_bias, b, s)


def _swa_mixer(xb, w_qkv, sinks, rel_bias, b, s):
    hd = N_HEADS * HEAD_DIM
    qkv = _matmul(xb, _bf(w_qkv), BF16, tn=512, scale_cols=hd, scale=HEAD_DIM ** -0.5)
    return _swa_attention(qkv, sinks, rel_bias, b, s)


def _nsa_mixer(xb, w_in, w_gate, cmp_pos, cmp_w1, cmp_w2, rel_bias, b, s):
    hd = N_HEADS * HEAD_DIM
    hk_n = NSA_KV_HEADS
    group = N_HEADS // hk_n
    main = hd + 6 * hk_n * HEAD_DIM
    proj = _matmul(xb, _bf(w_in), BF16, tn=512, scale_cols=hd, scale=HEAD_DIM ** -0.5, n_out=main)
    wg = w_gate.reshape(-1, hk_n, 3 * group)
    wg = jnp.pad(wg, ((0, 0), (0, 0), (0, LANES - 3 * group))).reshape(-1, hk_n * LANES)
    gates = _matmul(xb, wg.astype(BF16), F32, tn=hk_n * LANES)
    cmp_kv = _nsa_compress(proj, cmp_pos, _bf(cmp_w1), _bf(cmp_w2), b, s)
    return _nsa_attention(proj, gates, cmp_kv, rel_bias, b, s)


def _mla_mixer(xb, w_down, q_norm, kv_norm, w_uq, w_ukv, b, s):
    hh = N_HEADS
    half = MLA_ROPE_DIM // 2
    lat = MLA_Q_LORA + MLA_KV_LORA
    swap = lambda t: jnp.concatenate([t[..., half:], t[..., :half]], axis=-1)
    w_down_x = jnp.concatenate([w_down, swap(w_down[:, lat:])], axis=1).astype(BF16)
    c = _matmul(xb, w_down_x, F32, tn=w_down_x.shape[1])

    inv = ROPE_BASE ** (-jnp.arange(0, MLA_ROPE_DIM, 2, dtype=F32) / MLA_ROPE_DIM)
    ang = jnp.arange(s, dtype=F32)[:, None] * inv[None, :]
    cos, sin = jnp.cos(ang), jnp.sin(ang)
    ktab = jnp.concatenate([cos, cos, -sin, sin], axis=1)
    qc_tab = jnp.concatenate([cos, cos, cos, cos], axis=1)
    qs_tab = jnp.concatenate([-sin, sin, -sin, sin], axis=1)
    cq, ckv, kr = _mla_prep(c, q_norm, kv_norm, ktab, s)

    wq = w_uq.reshape(MLA_Q_LORA, hh, MLA_NOPE_DIM + MLA_ROPE_DIM)
    wq_rope = wq[:, :, MLA_NOPE_DIM:]
    wq_x = jnp.concatenate([wq[:, :, :MLA_NOPE_DIM].reshape(MLA_Q_LORA, -1),
                            wq_rope.reshape(MLA_Q_LORA, -1),
                            swap(wq_rope).reshape(MLA_Q_LORA, -1)], axis=1).astype(BF16)
    scale = (MLA_NOPE_DIM + MLA_ROPE_DIM) ** -0.5
    qx = _matmul(cq, wq_x, BF16, tn=512, scale_cols=wq_x.shape[1], scale=scale)
    wkv = w_ukv.reshape(MLA_KV_LORA, hh, MLA_NOPE_DIM + MLA_V_DIM)
    wkv_x = jnp.concatenate([wkv[:, :, :MLA_NOPE_DIM].reshape(MLA_KV_LORA, -1),
                             wkv[:, :, MLA_NOPE_DIM:].reshape(MLA_KV_LORA, -1)], axis=1).astype(BF16)
    kv = _matmul(ckv, wkv_x, BF16, tn=512)
    return _mla_attention(qx, kv, kr, qc_tab, qs_tab, b, s)


def kernel(x, rel_bias, moba_w_qkv, moba_w_o, swa_w_qkv, swa_sinks, swa_w_o, nsa_w_in, nsa_cmp_pos,
           nsa_cmp_w1, nsa_cmp_w2, nsa_w_o, mla_w_down, mla_q_norm, mla_kv_norm, mla_w_uq, mla_w_ukv,
           mla_w_o, ln1_g, ln1_b, ffn_w_gate, ffn_w_up, ffn_w_down, ln2_g, ln2_b):
    b, s, d = x.shape
    depth = ln1_g.shape[0]
    n_mixers = 4
    xf = x.reshape(b * s, d).astype(F32)
    xb = xf.astype(BF16)
    moba_qkv_b, moba_o_b = moba_w_qkv, _to_bf16(moba_w_o)
    swa_qkv_b, swa_o_b = swa_w_qkv, _to_bf16(swa_w_o)
    nsa_in_b, nsa_o_b, nsa_w1_b = nsa_w_in, _to_bf16(nsa_w_o), _to_bf16(nsa_cmp_w1)
    mla_o_b = _to_bf16(mla_w_o)
    gate_b, up_b, down_b = ffn_w_gate, ffn_w_up, _to_bf16(ffn_w_down)
    nsa_main = (N_HEADS + 6 * NSA_KV_HEADS) * HEAD_DIM
    for i in range(depth):
        kind, j = i % n_mixers, i // n_mixers
        if kind == 0:
            o = _moba_mixer(xb, (moba_qkv_b, j), rel_bias, b, s)
            w_o = (moba_o_b, j)
        elif kind == 1:
            o = _swa_mixer(xb, (swa_qkv_b, j), swa_sinks[j], rel_bias, b, s)
            w_o = (swa_o_b, j)
        elif kind == 2:
            o = _nsa_mixer(xb, (nsa_in_b, j), nsa_w_in[j][:, nsa_main:], nsa_cmp_pos[j], nsa_w1_b[j],
                           nsa_cmp_w2[j], rel_bias, b, s)
            w_o = (nsa_o_b, j)
        else:
            o = _mla_mixer(xb, mla_w_down[j], mla_q_norm[j], mla_kv_norm[j], mla_w_uq[j], mla_w_ukv[j], b, s)
            w_o = (mla_o_b, j)
        xf, xb = _proj_ln(o, w_o, xf, ln1_g[i], ln1_b[i])
        hmid = _ffn_up(xb, (gate_b, i), (up_b, i))
        xf, xb = _proj_ln(hmid, (down_b, i), xf, ln2_g[i], ln2_b[i])
    return xf.reshape(b, s, d).astype(x.dtype)
```

```python
import functools
import math

import jax
import jax.numpy as jnp
import numpy as np
from jax import lax
from jax.experimental import pallas as pl
from jax.experimental.pallas import tpu as pltpu

F32 = jnp.float32
BF16 = jnp.bfloat16

N_HEADS = 16
HEAD_DIM = 128
REL_BUCKETS = 32
REL_MAX_DIST = 128
MOBA_BLOCK = 256
MOBA_TOPK = 3
SWA_WINDOW = 128
SWA_KV_HEADS = 2
NSA_KV_HEADS = 4
NSA_CMP_LEN = 32
NSA_CMP_STRIDE = 16
NSA_CMP_HIDDEN = 256
NSA_SEL_BLOCK = 64
NSA_TOPN = 16
NSA_WINDOW = 512
MLA_Q_LORA = 512
MLA_KV_LORA = 512
MLA_NOPE_DIM = 128
MLA_ROPE_DIM = 64
MLA_V_DIM = 128
ROPE_BASE = 10000.0
DEPTH = 4
ALPHA = (2 * DEPTH) ** 0.25
LN_EPS = 1e-5
RMS_EPS = 1e-6

LANES = 128
TILE = 256
BIG = 2 * TILE
NEG = -1e30
VMEM_LIMIT = 56 * 1024 * 1024

_NT = (((1,), (1,)), ((), ()))


def _cparams(sem):
    return pltpu.CompilerParams(dimension_semantics=sem, vmem_limit_bytes=VMEM_LIMIT)


def _matmul_kernel(x_ref, w_ref, o_ref, *, scale_tiles, scale):
    acc = jnp.dot(x_ref[...], w_ref[...].astype(BF16), preferred_element_type=F32)
    if scale_tiles:
        acc = acc * jnp.where(pl.program_id(1) < scale_tiles, scale, 1.0).astype(F32)
    o_ref[...] = acc.astype(o_ref.dtype)


CAST_BLOCK_BYTES = 6 * 1024 * 1024


def _cast_kernel(x_ref, o_ref):
    o_ref[...] = x_ref[...].astype(o_ref.dtype)


def _to_bf16(w):
    w2 = w.reshape(-1, w.shape[-1])
    rows, cols = w2.shape
    tr = rows
    while tr % 2 == 0 and tr > 16 and tr * cols * 4 > CAST_BLOCK_BYTES:
        tr //= 2
    out = pl.pallas_call(
        _cast_kernel,
        out_shape=jax.ShapeDtypeStruct((rows, cols), BF16),
        grid=(rows // tr,),
        in_specs=[pl.BlockSpec((tr, cols), lambda i: (i, 0))],
        out_specs=pl.BlockSpec((tr, cols), lambda i: (i, 0)),
        compiler_params=_cparams(("parallel",)),
        name="cast_bf16",
    )(w2)
    return out.reshape(w.shape)


def _wspec(w, rows, cols, index):
    if isinstance(w, tuple):
        layer = w[1]
        return w[0], pl.BlockSpec((None, rows, cols), lambda i, j: (layer,) + index(i, j))
    return w, pl.BlockSpec((rows, cols), index)


def _wshape(w):
    return w[0].shape[1:] if isinstance(w, tuple) else w.shape


def _matmul(x, w, out_dtype, *, tn, tm=1024, scale_cols=0, scale=1.0, n_out=None):
    m, k = x.shape
    n = _wshape(w)[1] if n_out is None else n_out
    w, w_spec = _wspec(w, k, tn, lambda i, j: (0, j))
    tm = min(tm, m)
    assert m % tm == 0 and n % tn == 0 and scale_cols % tn == 0
    kern = functools.partial(_matmul_kernel, scale_tiles=scale_cols // tn, scale=scale)
    return pl.pallas_call(
        kern,
        out_shape=jax.ShapeDtypeStruct((m, n), out_dtype),
        grid=(m // tm, n // tn),
        in_specs=[pl.BlockSpec((tm, k), lambda i, j: (i, 0)), w_spec],
        out_specs=pl.BlockSpec((tm, tn), lambda i, j: (i, j)),
        compiler_params=_cparams(("parallel", "arbitrary")),
        name="proj_matmul",
    )(x, w)


LN_SUB = 128
ACC_SUB = 512


def _proj_ln_kernel(a_ref, w_ref, x_ref, g_ref, b_ref, o_ref, ob_ref, *, nk):
    kk = pl.program_id(1)
    tm = a_ref.shape[0]

    def part(rows):
        return jnp.dot(a_ref[rows, :], w_ref[...], preferred_element_type=F32)

    def finish(rows, y):
        mu = jnp.mean(y, axis=-1, keepdims=True)
        yc = y - mu
        var = jnp.mean(yc * yc, axis=-1, keepdims=True)
        out = yc * lax.rsqrt(var + LN_EPS) * g_ref[...] + b_ref[...]
        o_ref[rows, :] = out
        ob_ref[rows, :] = out.astype(BF16)

    subs = [pl.ds(r, LN_SUB) for r in range(0, tm, LN_SUB)]
    if nk == 1:
        for rows in subs:
            finish(rows, ALPHA * x_ref[rows, :] + part(rows))
        return

    acc_subs = [pl.ds(r, min(ACC_SUB, tm)) for r in range(0, tm, ACC_SUB)]

    @pl.when(kk == 0)
    def _():
        for rows in acc_subs:
            o_ref[rows, :] = part(rows)

    @pl.when((kk > 0) & (kk < nk - 1))
    def _():
        for rows in acc_subs:
            o_ref[rows, :] += part(rows)

    @pl.when(kk == nk - 1)
    def _():
        for rows in subs:
            finish(rows, ALPHA * x_ref[rows, :] + (o_ref[rows, :] + part(rows)))


MAX_TK = 2048


def _proj_ln(a, w, x, g, b, *, tm=512, tk=None):
    m, k = a.shape
    d = _wshape(w)[1]
    tm = min(tm, m)
    if tk is None:
        tk = max(t for t in range(LANES, min(k, MAX_TK) + 1, LANES) if k % t == 0)
    assert m % tm == 0 and k % tk == 0 and tm % LN_SUB == 0
    w, w_spec = _wspec(w, tk, d, lambda i, j: (j, 0))
    return pl.pallas_call(
        functools.partial(_proj_ln_kernel, nk=k // tk),
        out_shape=(jax.ShapeDtypeStruct((m, d), F32), jax.ShapeDtypeStruct((m, d), BF16)),
        grid=(m // tm, k // tk),
        in_specs=[pl.BlockSpec((tm, tk), lambda i, j: (i, j)),
                  w_spec,
                  pl.BlockSpec((tm, d), lambda i, j: (i, 0)),
                  pl.BlockSpec((1, d), lambda i, j: (0, 0)),
                  pl.BlockSpec((1, d), lambda i, j: (0, 0))],
        out_specs=(pl.BlockSpec((tm, d), lambda i, j: (i, 0)),
                   pl.BlockSpec((tm, d), lambda i, j: (i, 0))),
        compiler_params=_cparams(("parallel", "arbitrary")),
        name="proj_deepnorm",
    )(a, w, x, g.reshape(1, d), b.reshape(1, d))


def _ffn_up_kernel(x_ref, wg_ref, wu_ref, o_ref):
    x = x_ref[...]
    gate = jnp.dot(x, wg_ref[...].astype(BF16), preferred_element_type=F32)
    up = jnp.dot(x, wu_ref[...].astype(BF16), preferred_element_type=F32)
    o_ref[...] = (gate * jax.nn.sigmoid(gate) * up).astype(o_ref.dtype)


def _ffn_up(x, wg, wu, *, tm=1024, tn=512):
    m, k = x.shape
    f = _wshape(wg)[1]
    tm = min(tm, m)
    tn = min(tn, f)
    assert m % tm == 0 and f % tn == 0
    wg, wg_spec = _wspec(wg, k, tn, lambda i, j: (0, j))
    wu, wu_spec = _wspec(wu, k, tn, lambda i, j: (0, j))
    return pl.pallas_call(
        _ffn_up_kernel,
        out_shape=jax.ShapeDtypeStruct((m, f), BF16),
        grid=(m // tm, f // tn),
        in_specs=[pl.BlockSpec((tm, k), lambda i, j: (i, 0)), wg_spec, wu_spec],
        out_specs=pl.BlockSpec((tm, tn), lambda i, j: (i, j)),
        compiler_params=_cparams(("parallel", "arbitrary")),
        name="ffn_gate_up",
    )(x, wg, wu)


def _t5_bucket(dist):
    n = np.maximum(dist, 0)
    max_exact = REL_BUCKETS // 2
    nf = np.maximum(n, max_exact).astype(np.float32)
    large = max_exact + (np.log(nf / max_exact) / math.log(REL_MAX_DIST / max_exact)
                         * (REL_BUCKETS - max_exact)).astype(np.int32)
    return np.where(n < max_exact, n, np.minimum(large, REL_BUCKETS - 1)).astype(np.int32)


def _bias_lookup(rel_bias, dist, visible):
    bucket = _t5_bucket(dist).reshape(-1)
    onehot = (jnp.arange(REL_BUCKETS, dtype=jnp.int32)[:, None] == jnp.asarray(bucket)[None, :]).astype(F32)
    tab = jnp.dot(rel_bias.T.astype(F32), onehot, precision=lax.Precision.HIGHEST)
    tab = tab.reshape((rel_bias.shape[1],) + dist.shape)
    return jnp.where(jnp.asarray(visible)[None], tab, NEG)


def _bias_tiles(rel_bias, window=None):
    r = np.arange(TILE)[:, None]
    c = np.arange(TILE)[None, :]
    d0 = r - c
    d1 = TILE + r - c
    ok0 = d0 >= 0
    ok1 = np.ones_like(ok0)
    if window is not None:
        ok0 = ok0 & (d0 < window)
        ok1 = d1 < window
    t0 = _bias_lookup(rel_bias, d0, ok0)
    t1 = _bias_lookup(rel_bias, d1, ok1)
    far = rel_bias[int(_t5_bucket(np.asarray(2 * TILE)))].astype(F32)
    return t0, t1, far


def _flash_update(s, v, m_ref, l_ref, acc_ref):
    m_prev = m_ref[...]
    m_new = jnp.maximum(m_prev, jnp.max(s, axis=1, keepdims=True))
    alpha = jnp.exp(m_prev - m_new)
    p = jnp.exp(s - jnp.concatenate([m_new] * (s.shape[1] // LANES), axis=1))
    l_ref[...] = alpha * l_ref[...] + jnp.sum(p, axis=1, keepdims=True)
    acc_ref[...] = alpha * acc_ref[...] + jnp.dot(p.astype(BF16), v, preferred_element_type=F32)
    m_ref[...] = m_new


def _flash_update_multi(svs, m_ref, l_ref, acc_ref):
    m_prev = m_ref[...]
    m_new = m_prev
    for s, _ in svs:
        m_new = jnp.maximum(m_new, jnp.max(s, axis=1, keepdims=True))
    alpha = jnp.exp(m_prev - m_new)
    l_new = alpha * l_ref[...]
    acc = alpha * acc_ref[...]
    for s, v in svs:
        p = jnp.exp(s - jnp.concatenate([m_new] * (s.shape[1] // LANES), axis=1))
        l_new = l_new + jnp.sum(p, axis=1, keepdims=True)
        acc = acc + jnp.dot(p.astype(BF16), v, preferred_element_type=F32)
    l_ref[...] = l_new
    acc_ref[...] = acc
    m_ref[...] = m_new


def _causal_chunk_schedule(ci, nq, step):
    for k in range(nq):
        @pl.when(ci == k)
        def _(k=k):
            order = [k] + list(range(k))
            for g in range(0, len(order), 2):
                step(k, order[g:g + 2])


def _flash_reset(m_ref, l_ref, acc_ref):
    m_ref[...] = jnp.full(m_ref.shape, NEG, F32)
    l_ref[...] = jnp.zeros(l_ref.shape, F32)
    acc_ref[...] = jnp.zeros(acc_ref.shape, F32)


def _rows(ref, t, tile=TILE):
    return ref[pl.ds(pl.multiple_of(t * tile, tile), tile), :]


def _split3(x):
    hi = x.astype(BF16)
    r1 = x - hi.astype(F32)
    mid = r1.astype(BF16)
    lo = (r1 - mid.astype(F32)).astype(BF16)
    return hi, mid, lo


def _rank_before(val, idx, n):
    cnt = jnp.zeros(val.shape, jnp.int32)
    for m in range(n):
        vm = val[m:m + 1, :]
        beats = (vm > val) | ((vm == val) & (m < idx))
        cnt = cnt + beats.astype(jnp.int32)
    return cnt


def _chosen_rows(chosen_t):
    n, q = chosen_t.shape
    padded = jnp.concatenate([chosen_t, jnp.zeros((LANES - n, q), F32)], axis=0)
    return padded.T.astype(BF16)


def _moba_kernel(q_ref, k_ref, v_ref, tab_ref, ex_ref, o_ref, m_ref, l_ref, acc_ref, km_ref, *, nblk):
    ci = pl.program_id(2)
    q = q_ref[...]

    @pl.when(ci == 0)
    def _():
        kmean = jnp.concatenate(
            [jnp.mean(k_ref[n * TILE:(n + 1) * TILE, :].astype(F32), axis=0, keepdims=True)
             for n in range(nblk)]
            + [jnp.zeros((LANES - nblk, HEAD_DIM), F32)], axis=0)
        for j, part in enumerate(_split3(kmean)):
            km_ref[j] = part

    nrow = -(-nblk // 8) * 8
    gate = sum(lax.dot_general(km_ref[j], q, _NT, preferred_element_type=F32) for j in range(3))[:nrow]
    blk = lax.broadcasted_iota(jnp.int32, gate.shape, 0)
    qblk = 2 * ci + lax.broadcasted_iota(jnp.int32, gate.shape, 1) // TILE
    past = blk < qblk
    gate = jnp.where(past, gate, -jnp.inf)
    cnt = _rank_before(gate, blk, nblk)
    chosen = _chosen_rows(jnp.where((past & (cnt < MOBA_TOPK)) | (blk == qblk), 1.0, 0.0))

    def chunk(c, k):
        rows = pl.ds(c * BIG, BIG)
        s = lax.dot_general(q, k_ref[rows, :], _NT, preferred_element_type=F32)
        hit = jnp.dot(chosen, ex_ref[c], preferred_element_type=F32)
        s = s + tab_ref[0, min(k - c, 2)] + (hit - 1.0) * (-NEG)
        return s, v_ref[rows, :]

    def step(k, chunks):
        _flash_update_multi([chunk(c, k) for c in chunks], m_ref, l_ref, acc_ref)

    _flash_reset(m_ref, l_ref, acc_ref)
    _causal_chunk_schedule(ci, nblk * TILE // BIG, step)
    o_ref[...] = (acc_ref[...] / l_ref[...]).astype(o_ref.dtype)


def _moba_attention(qkv, rel_bias, b, s):
    assert MOBA_BLOCK == TILE and s % BIG == 0
    nblk = s // TILE
    nq = s // BIG
    assert max(1, min(MOBA_TOPK, nblk - 1)) == MOBA_TOPK and nblk <= LANES
    hh = N_HEADS
    t0, t1, far = _bias_tiles(rel_bias)
    ff = jnp.broadcast_to(far[:, None, None], t0.shape)
    zz = jnp.zeros_like(t0)
    blk = lambda a, b_, c, d: jnp.concatenate([jnp.concatenate([a, b_], axis=2),
                                               jnp.concatenate([c, d], axis=2)], axis=1)
    tab = jnp.stack([blk(t0, zz, t1, t0), blk(ff, t1, ff, ff), blk(ff, ff, ff, ff)], axis=1)
    expand = jnp.asarray(np.arange(LANES)[None, :, None]
                         == (2 * np.arange(nq)[:, None, None] + np.arange(BIG)[None, None, :] // TILE), BF16)
    kern = functools.partial(_moba_kernel, nblk=nblk)
    return pl.pallas_call(
        kern,
        out_shape=jax.ShapeDtypeStruct((b * s, hh * HEAD_DIM), BF16),
        grid=(b, hh, nq),
        in_specs=[pl.BlockSpec((BIG, HEAD_DIM), lambda bi, h, i: (bi * nq + i, h)),
                  pl.BlockSpec((s, HEAD_DIM), lambda bi, h, i: (bi, hh + h)),
                  pl.BlockSpec((s, HEAD_DIM), lambda bi, h, i: (bi, 2 * hh + h)),
                  pl.BlockSpec((1, 3, BIG, BIG), lambda bi, h, i: (h, 0, 0, 0)),
                  pl.BlockSpec((nq, LANES, BIG), lambda bi, h, i: (0, 0, 0))],
        out_specs=pl.BlockSpec((BIG, HEAD_DIM), lambda bi, h, i: (bi * nq + i, h)),
        scratch_shapes=[pltpu.VMEM((BIG, LANES), F32), pltpu.VMEM((BIG, LANES), F32),
                        pltpu.VMEM((BIG, HEAD_DIM), F32), pltpu.VMEM((3, LANES, HEAD_DIM), BF16)],
        compiler_params=_cparams(("parallel", "parallel", "arbitrary")),
        name="moba_attention",
    )(qkv, qkv, qkv, tab, expand)


def _swa_kernel(sink_ref, q_ref, k_ref, v_ref, t0_ref, t1_ref, o_ref, m_ref, l_ref, acc_ref, *, group):
    hk = pl.program_id(0)
    i = pl.program_id(2)
    q = jnp.concatenate([q_ref[:, g * HEAD_DIM:(g + 1) * HEAD_DIM] for g in range(group)], axis=0)

    for g in range(group):
        m_ref[g * TILE:(g + 1) * TILE, :] = jnp.full((TILE, LANES), sink_ref[hk * group + g], F32)
    l_ref[...] = jnp.ones(l_ref.shape, F32)
    acc_ref[...] = jnp.zeros(acc_ref.shape, F32)

    s = lax.dot_general(q, _rows(k_ref, i), _NT, preferred_element_type=F32)
    s = s + t0_ref[...].reshape(group * TILE, TILE)
    _flash_update(s, _rows(v_ref, i), m_ref, l_ref, acc_ref)

    @pl.when(i > 0)
    def _():
        s = lax.dot_general(q, _rows(k_ref, i - 1), _NT, preferred_element_type=F32)
        s = s + t1_ref[...].reshape(group * TILE, TILE)
        _flash_update(s, _rows(v_ref, i - 1), m_ref, l_ref, acc_ref)

    out = acc_ref[...] / l_ref[...]
    for g in range(group):
        o_ref[:, g * HEAD_DIM:(g + 1) * HEAD_DIM] = out[g * TILE:(g + 1) * TILE].astype(o_ref.dtype)


def _swa_attention(qkv, sinks, rel_bias, b, s):
    assert SWA_WINDOW <= TILE and s % TILE == 0
    nq = s // TILE
    hk_n = SWA_KV_HEADS
    group = N_HEADS // hk_n
    t0, t1, _ = _bias_tiles(rel_bias, window=SWA_WINDOW)
    kern = functools.partial(_swa_kernel, group=group)
    gw = group * HEAD_DIM
    return pl.pallas_call(
        kern,
        out_shape=jax.ShapeDtypeStruct((b * s, N_HEADS * HEAD_DIM), BF16),
        grid=(hk_n, b, nq),
        in_specs=[pl.BlockSpec(memory_space=pltpu.SMEM),
                  pl.BlockSpec((TILE, gw), lambda hk, bi, i: (bi * nq + i, hk)),
                  pl.BlockSpec((s, HEAD_DIM), lambda hk, bi, i: (bi, N_HEADS + hk)),
                  pl.BlockSpec((s, HEAD_DIM), lambda hk, bi, i: (bi, N_HEADS + hk_n + hk)),
                  pl.BlockSpec((group, TILE, TILE), lambda hk, bi, i: (hk, 0, 0)),
                  pl.BlockSpec((group, TILE, TILE), lambda hk, bi, i: (hk, 0, 0))],
        out_specs=pl.BlockSpec((TILE, gw), lambda hk, bi, i: (bi * nq + i, hk)),
        scratch_shapes=[pltpu.VMEM((group * TILE, LANES), F32), pltpu.VMEM((group * TILE, LANES), F32),
                        pltpu.VMEM((group * TILE, HEAD_DIM), F32)],
        compiler_params=_cparams(("parallel", "parallel", "arbitrary")),
        name="swa_attention",
    )(sinks.astype(F32), qkv, qkv, qkv, t0, t1)


def _gelu_tanh(x):
    return 0.5 * x * (1.0 + jnp.tanh(math.sqrt(2.0 / math.pi) * (x + 0.044715 * (x * x * x))))


def _nsa_compress_kernel(t_ref, pe_ref, w1_ref, w2_ref, o_ref, tf_ref, *, ncmp_pad):
    stride = NSA_CMP_STRIDE
    half = stride * HEAD_DIM
    pe = pe_ref[0]
    tf_ref[...] = t_ref[...].astype(F32)
    xa, xb = [], []
    for r in range(stride):
        x = tf_ref[pl.ds(r, ncmp_pad, stride=stride), :]
        xa.append((x + pe[r:r + 1, :]).astype(BF16))
        xb.append((x + pe[stride + r:stride + r + 1, :]).astype(BF16))
    a = jnp.dot(jnp.concatenate(xa, axis=1), w1_ref[0, :half, :], preferred_element_type=F32)
    bm = jnp.dot(jnp.concatenate(xb, axis=1), w1_ref[0, half:, :], preferred_element_type=F32)
    hid = _gelu_tanh(a + pltpu.roll(bm, ncmp_pad - 1, 0))
    o_ref[0, 0, 0] = jnp.dot(hid.astype(BF16), w2_ref[0], preferred_element_type=F32).astype(o_ref.dtype)


def _nsa_compress(proj, pe, w1, w2, b, s):
    assert NSA_CMP_LEN == 2 * NSA_CMP_STRIDE and s % NSA_CMP_STRIDE == 0
    ncp = s // NSA_CMP_STRIDE
    hk_n = NSA_KV_HEADS
    kern = functools.partial(_nsa_compress_kernel, ncmp_pad=ncp)
    return pl.pallas_call(
        kern,
        out_shape=jax.ShapeDtypeStruct((2, b, hk_n, ncp, HEAD_DIM), BF16),
        grid=(2, b, hk_n),
        in_specs=[pl.BlockSpec((s, HEAD_DIM), lambda kv, bi, hk: (bi, N_HEADS + kv * hk_n + hk)),
                  pl.BlockSpec((1, NSA_CMP_LEN, HEAD_DIM), lambda kv, bi, hk: (kv, 0, 0)),
                  pl.BlockSpec((1, NSA_CMP_LEN * HEAD_DIM, NSA_CMP_HIDDEN), lambda kv, bi, hk: (kv, 0, 0)),
                  pl.BlockSpec((1, NSA_CMP_HIDDEN, HEAD_DIM), lambda kv, bi, hk: (kv, 0, 0))],
        out_specs=pl.BlockSpec((1, 1, 1, ncp, HEAD_DIM), lambda kv, bi, hk: (kv, bi, hk, 0, 0)),
        scratch_shapes=[pltpu.VMEM((s, HEAD_DIM), F32)],
        compiler_params=_cparams(("parallel", "parallel", "arbitrary")),
        name="nsa_compress",
    )(proj, pe.astype(F32), w1, w2)


def _nsa_kernel(far_ref, q_ref, kc_ref, vc_ref, cb_ref, ksl_ref, vsl_ref, kw_ref, vw_ref, gate_ref,
                t0_ref, t1_ref, tw_ref, ov_ref, ex_ref, o_ref,
                m_ref, l_ref, acc_ref, sel_ref, mw_ref, lw_ref, accw_ref, *, group, nq):
    hk = pl.program_id(1)
    i = pl.program_id(2)
    rows = group * TILE
    q = jnp.concatenate([q_ref[:, g * HEAD_DIM:(g + 1) * HEAD_DIM] for g in range(group)], axis=0)

    cb = cb_ref[...].reshape(rows, LANES)
    sc = lax.dot_general(q, kc_ref[0, 0, 0], _NT, preferred_element_type=F32) + cb
    valid = cb > 0.5 * NEG
    mc = jnp.max(sc, axis=1, keepdims=True)
    pc = jnp.where(valid, jnp.exp(sc - mc), 0.0)
    pc = pc / jnp.maximum(jnp.sum(pc, axis=1, keepdims=True), 1e-30)
    o_cmp = jnp.dot(pc.astype(BF16), vc_ref[0, 0, 0], preferred_element_type=F32)

    psum = pc[0:TILE]
    for g in range(1, group):
        psum = psum + pc[g * TILE:(g + 1) * TILE]
    n_sel = nq * (TILE // NSA_SEL_BLOCK)
    imp = sum(lax.dot_general(ov_ref[...], part, _NT, preferred_element_type=F32)
              for part in _split3(psum))[:n_sel]
    blk = lax.broadcasted_iota(jnp.int32, imp.shape, 0)
    qblk = (i * TILE + lax.broadcasted_iota(jnp.int32, imp.shape, 1)) // NSA_SEL_BLOCK
    forced = (blk == 0) | (blk == qblk) | (blk == qblk - 1)
    imp = jnp.where(blk > qblk, -jnp.inf, jnp.where(forced, jnp.inf, imp))
    cnt = _rank_before(imp, blk, n_sel)
    chosen = _chosen_rows(jnp.where((blk <= qblk) & (cnt < NSA_TOPN), 1.0, 0.0))
    for t in range(nq):
        @pl.when(t <= i)
        def _(t=t):
            hit = jnp.dot(chosen, ex_ref[:, t * TILE:(t + 1) * TILE], preferred_element_type=F32)
            sel_ref[t] = (hit - 1.0) * (-NEG)

    t0 = t0_ref[...].reshape(rows, TILE)

    def masked(s, t, with_far=False):
        sel = sel_ref[t]
        parts = []
        for g in range(group):
            add = sel + far_ref[hk * group + g] if with_far else sel
            parts.append(s[g * TILE:(g + 1) * TILE] + add)
        return jnp.concatenate(parts, axis=0)

    slc = (m_ref, l_ref, acc_ref)
    win = (mw_ref, lw_ref, accw_ref)
    _flash_reset(*slc)
    _flash_reset(*win)

    def logits(k_ref, t):
        return lax.dot_general(q, _rows(k_ref, t), _NT, preferred_element_type=F32)

    _flash_update(masked(logits(ksl_ref, i) + t0, i), _rows(vsl_ref, i), *slc)
    _flash_update(logits(kw_ref, i) + t0, _rows(vw_ref, i), *win)

    @pl.when(i > 0)
    def _():
        t1 = t1_ref[...].reshape(rows, TILE)
        _flash_update(masked(logits(ksl_ref, i - 1) + t1, i - 1), _rows(vsl_ref, i - 1), *slc)
        _flash_update(logits(kw_ref, i - 1) + t1, _rows(vw_ref, i - 1), *win)

    @pl.when(i > 1)
    def _():
        _flash_update(masked(logits(ksl_ref, i - 2), i - 2, with_far=True), _rows(vsl_ref, i - 2), *slc)
        _flash_update(logits(kw_ref, i - 2) + tw_ref[...].reshape(rows, TILE), _rows(vw_ref, i - 2), *win)

    def far_tile(t, carry):
        _flash_update(masked(logits(ksl_ref, t), t, with_far=True), _rows(vsl_ref, t), *slc)
        return carry

    lax.fori_loop(0, jnp.maximum(i - 2, 0), far_tile, 0)
    o_slc = acc_ref[...] / l_ref[...]
    o_win = accw_ref[...] / lw_ref[...]
    gates = jax.nn.sigmoid(gate_ref[...])
    for g in range(group):
        sl = slice(g * TILE, (g + 1) * TILE)
        o = (gates[:, 3 * g:3 * g + 1] * o_cmp[sl] + gates[:, 3 * g + 1:3 * g + 2] * o_slc[sl]
             + gates[:, 3 * g + 2:3 * g + 3] * o_win[sl])
        o_ref[:, g * HEAD_DIM:(g + 1) * HEAD_DIM] = o.astype(o_ref.dtype)


def _nsa_attention(proj, gates, cmp_kv, rel_bias, b, s):
    assert s % TILE == 0 and NSA_WINDOW == 2 * TILE and TILE % NSA_SEL_BLOCK == 0
    nq = s // TILE
    hk_n = NSA_KV_HEADS
    group = N_HEADS // hk_n
    gw = group * HEAD_DIM
    ncp = s // NSA_CMP_STRIDE
    n_cmp = (s - NSA_CMP_LEN) // NSA_CMP_STRIDE + 1
    n_sel = s // NSA_SEL_BLOCK
    assert ncp == LANES and n_sel <= LANES and min(NSA_TOPN, n_sel) == NSA_TOPN
    t0, t1, far = _bias_tiles(rel_bias)
    r = np.arange(TILE)[:, None]
    c = np.arange(TILE)[None, :]
    tw = jnp.where(jnp.asarray(c > r)[None], far[:, None, None], NEG)
    pos = np.arange(s)[:, None]
    cidx = np.arange(ncp)[None, :]
    cdist = pos - (cidx * NSA_CMP_STRIDE + NSA_CMP_LEN - 1)
    cmp_bias = _bias_lookup(rel_bias, cdist, (cdist >= 0) & (cidx < n_cmp))
    ci = np.arange(ncp)[:, None] * NSA_CMP_STRIDE
    sj = np.arange(LANES)[None, :] * NSA_SEL_BLOCK
    overlap = jnp.asarray(((ci < sj + NSA_SEL_BLOCK) & (ci + NSA_CMP_LEN > sj)
                           & (np.arange(ncp)[:, None] < n_cmp) & (np.arange(LANES)[None, :] < n_sel)).T, BF16)
    expand = jnp.asarray(np.arange(LANES)[:, None] == (np.arange(s)[None, :] // NSA_SEL_BLOCK), BF16)

    kern = functools.partial(_nsa_kernel, group=group, nq=nq)
    kv_spec = lambda off: pl.BlockSpec((s, HEAD_DIM), lambda bi, hk, i: (bi, N_HEADS + off * hk_n + hk))
    cmp_spec = lambda kv: pl.BlockSpec((1, 1, 1, ncp, HEAD_DIM), lambda bi, hk, i: (kv, bi, hk, 0, 0))
    tile_spec = pl.BlockSpec((group, TILE, TILE), lambda bi, hk, i: (hk, 0, 0))
    return pl.pallas_call(
        kern,
        out_shape=jax.ShapeDtypeStruct((b * s, N_HEADS * HEAD_DIM), BF16),
        grid=(b, hk_n, nq),
        in_specs=[pl.BlockSpec(memory_space=pltpu.SMEM),
                  pl.BlockSpec((TILE, gw), lambda bi, hk, i: (bi * nq + i, hk)),
                  cmp_spec(0), cmp_spec(1),
                  pl.BlockSpec((group, TILE, ncp), lambda bi, hk, i: (hk, i, 0)),
                  kv_spec(2), kv_spec(3), kv_spec(4), kv_spec(5),
                  pl.BlockSpec((TILE, LANES), lambda bi, hk, i: (bi * nq + i, hk)),
                  tile_spec, tile_spec, tile_spec,
                  pl.BlockSpec((ncp, LANES), lambda bi, hk, i: (0, 0)),
                  pl.BlockSpec((LANES, s), lambda bi, hk, i: (0, 0))],
        out_specs=pl.BlockSpec((TILE, gw), lambda bi, hk, i: (bi * nq + i, hk)),
        scratch_shapes=[pltpu.VMEM((group * TILE, LANES), F32), pltpu.VMEM((group * TILE, LANES), F32),
                        pltpu.VMEM((group * TILE, HEAD_DIM), F32),
                        pltpu.VMEM((nq, TILE, TILE), F32),
                        pltpu.VMEM((group * TILE, LANES), F32), pltpu.VMEM((group * TILE, LANES), F32),
                        pltpu.VMEM((group * TILE, HEAD_DIM), F32)],
        compiler_params=_cparams(("parallel", "parallel", "arbitrary")),
        name="nsa_attention",
    )(far, proj, cmp_kv, cmp_kv, cmp_bias, proj, proj, proj, proj, gates, t0, t1, tw, overlap, expand)


def _rms(x, g):
    return x * lax.rsqrt(jnp.mean(x * x, axis=-1, keepdims=True) + RMS_EPS) * g


def _mla_prep_kernel(c_ref, qg_ref, kg_ref, tab_ref, cq_ref, ckv_ref, kr_ref):
    c = c_ref[...]
    cq_ref[...] = _rms(c[:, :MLA_Q_LORA], qg_ref[...]).astype(BF16)
    ckv_ref[...] = _rms(c[:, MLA_Q_LORA:MLA_Q_LORA + MLA_KV_LORA], kg_ref[...]).astype(BF16)
    y = c[:, MLA_Q_LORA + MLA_KV_LORA:] * tab_ref[...]
    y = y + pltpu.roll(y, MLA_ROPE_DIM, 1)
    lane = lax.broadcasted_iota(jnp.int32, y.shape, 1)
    kr_ref[0] = jnp.where(lane < MLA_ROPE_DIM, y, 0.0).astype(BF16)
    kr_ref[1] = jnp.where(lane >= MLA_ROPE_DIM, y, 0.0).astype(BF16)


def _mla_prep(c, q_norm, kv_norm, ktab, s, *, tm=512):
    m, w = c.shape
    tm = min(tm, s)
    assert 2 * MLA_ROPE_DIM == LANES and w == MLA_Q_LORA + MLA_KV_LORA + LANES and s % tm == 0
    ns = s // tm
    return pl.pallas_call(
        _mla_prep_kernel,
        out_shape=(jax.ShapeDtypeStruct((m, MLA_Q_LORA), BF16),
                   jax.ShapeDtypeStruct((m, MLA_KV_LORA), BF16),
                   jax.ShapeDtypeStruct((2, m, LANES), BF16)),
        grid=(m // tm,),
        in_specs=[pl.BlockSpec((tm, w), lambda i: (i, 0)),
                  pl.BlockSpec((1, MLA_Q_LORA), lambda i: (0, 0)),
                  pl.BlockSpec((1, MLA_KV_LORA), lambda i: (0, 0)),
                  pl.BlockSpec((tm, LANES), lambda i: (i % ns, 0))],
        out_specs=(pl.BlockSpec((tm, MLA_Q_LORA), lambda i: (i, 0)),
                   pl.BlockSpec((tm, MLA_KV_LORA), lambda i: (i, 0)),
                   pl.BlockSpec((2, tm, LANES), lambda i: (0, i, 0))),
        compiler_params=_cparams(("parallel",)),
        name="mla_prep",
    )(c, q_norm.reshape(1, -1).astype(F32), kv_norm.reshape(1, -1).astype(F32), ktab)


def _mla_kernel(qn_ref, qr_ref, qs_ref, ct_ref, st_ref, kn_ref, kr_ref, v_ref, o_ref,
                m_ref, l_ref, acc_ref, kcat_ref):
    i = pl.program_id(2)

    @pl.when(i == 0)
    def _():
        kcat_ref[:, :HEAD_DIM] = kn_ref[...]
        kcat_ref[:, HEAD_DIM:] = kr_ref[0]

    qr = qr_ref[...].astype(F32) * ct_ref[...] + qs_ref[...].astype(F32) * st_ref[...]
    q = jnp.concatenate([qn_ref[...], qr.astype(BF16)], axis=1)

    def chunk(c, k):
        rows = pl.ds(c * BIG, BIG)
        s = lax.dot_general(q, kcat_ref[rows, :], _NT, preferred_element_type=F32)
        if c == k:
            r = lax.broadcasted_iota(jnp.int32, s.shape, 0)
            col = lax.broadcasted_iota(jnp.int32, s.shape, 1)
            s = jnp.where(r >= col, s, NEG)
        return s, v_ref[rows, :]

    def step(k, chunks):
        _flash_update_multi([chunk(c, k) for c in chunks], m_ref, l_ref, acc_ref)

    _flash_reset(m_ref, l_ref, acc_ref)
    _causal_chunk_schedule(i, kcat_ref.shape[0] // BIG, step)
    o_ref[...] = (acc_ref[...] / l_ref[...]).astype(o_ref.dtype)


def _mla_attention(qx, kv, kr, qc_tab, qs_tab, b, s):
    assert s % BIG == 0
    nq = s // BIG
    hh = N_HEADS
    return pl.pallas_call(
        _mla_kernel,
        out_shape=jax.ShapeDtypeStruct((b * s, hh * MLA_V_DIM), BF16),
        grid=(b, hh, nq),
        in_specs=[pl.BlockSpec((BIG, LANES), lambda bi, h, i: (bi * nq + i, h)),
                  pl.BlockSpec((BIG, LANES), lambda bi, h, i: (bi * nq + i, hh + h // 2)),
                  pl.BlockSpec((BIG, LANES), lambda bi, h, i: (bi * nq + i, hh + hh // 2 + h // 2)),
                  pl.BlockSpec((BIG, LANES), lambda bi, h, i: (i, 0)),
                  pl.BlockSpec((BIG, LANES), lambda bi, h, i: (i, 0)),
                  pl.BlockSpec((s, LANES), lambda bi, h, i: (bi, h)),
                  pl.BlockSpec((1, s, LANES), lambda bi, h, i: (h % 2, bi, 0)),
                  pl.BlockSpec((s, LANES), lambda bi, h, i: (bi, hh + h))],
        out_specs=pl.BlockSpec((BIG, LANES), lambda bi, h, i: (bi * nq + i, h)),
        scratch_shapes=[pltpu.VMEM((BIG, LANES), F32), pltpu.VMEM((BIG, LANES), F32),
                        pltpu.VMEM((BIG, MLA_V_DIM), F32), pltpu.VMEM((s, 2 * LANES), BF16)],
        compiler_params=_cparams(("parallel", "parallel", "arbitrary")),
        name="mla_attention",
    )(qx, qx, qx, qc_tab, qs_tab, kv, kr, kv)


def _bf(w):
    return w if isinstance(w, tuple) or w.dtype == BF16 else w.astype(BF16)


def _moba_mixer(xb, w_qkv, rel_bias, b, s):
    hd = N_HEADS * HEAD_DIM
    qkv = _matmul(xb, _bf(w_qkv), BF16, tn=512, scale_cols=hd, scale=HEAD_DIM ** -0.5)
    return _moba_attention(qkv, rel_bias, b, s)


def _swa_mixer(xb, w_qkv, sinks, rel_bias, b, s):
    hd = N_HEADS * HEAD_DIM
    qkv = _matmul(xb, _bf(w_qkv), BF16, tn=512, scale_cols=hd, scale=HEAD_DIM ** -0.5)
    return _swa_attention(qkv, sinks, rel_bias, b, s)


def _nsa_mixer(xb, w_in, w_gate, cmp_pos, cmp_w1, cmp_w2, rel_bias, b, s):
    hd = N_HEADS * HEAD_DIM
    hk_n = NSA_KV_HEADS
    group = N_HEADS // hk_n
    main = hd + 6 * hk_n * HEAD_DIM
    proj = _matmul(xb, _bf(w_in), BF16, tn=512, scale_cols=hd, scale=HEAD_DIM ** -0.5, n_out=main)
    wg = w_gate.reshape(-1, hk_n, 3 * group)
    wg = jnp.pad(wg, ((0, 0), (0, 0), (0, LANES - 3 * group))).reshape(-1, hk_n * LANES)
    gates = _matmul(xb, wg.astype(BF16), F32, tn=hk_n * LANES)
    cmp_kv = _nsa_compress(proj, cmp_pos, _bf(cmp_w1), _bf(cmp_w2), b, s)
    return _nsa_attention(proj, gates, cmp_kv, rel_bias, b, s)


def _mla_mixer(xb, w_down, q_norm, kv_norm, w_uq, w_ukv, b, s):
    hh = N_HEADS
    half = MLA_ROPE_DIM // 2
    lat = MLA_Q_LORA + MLA_KV_LORA
    swap = lambda t: jnp.concatenate([t[..., half:], t[..., :half]], axis=-1)
    w_down_x = jnp.concatenate([w_down, swap(w_down[:, lat:])], axis=1).astype(BF16)
    c = _matmul(xb, w_down_x, F32, tn=w_down_x.shape[1])

    inv = ROPE_BASE ** (-jnp.arange(0, MLA_ROPE_DIM, 2, dtype=F32) / MLA_ROPE_DIM)
    ang = jnp.arange(s, dtype=F32)[:, None] * inv[None, :]
    cos, sin = jnp.cos(ang), jnp.sin(ang)
    ktab = jnp.concatenate([cos, cos, -sin, sin], axis=1)
    qc_tab = jnp.concatenate([cos, cos, cos, cos], axis=1)
    qs_tab = jnp.concatenate([-sin, sin, -sin, sin], axis=1)
    cq, ckv, kr = _mla_prep(c, q_norm, kv_norm, ktab, s)

    wq = w_uq.reshape(MLA_Q_LORA, hh, MLA_NOPE_DIM + MLA_ROPE_DIM)
    wq_rope = wq[:, :, MLA_NOPE_DIM:]
    wq_x = jnp.concatenate([wq[:, :, :MLA_NOPE_DIM].reshape(MLA_Q_LORA, -1),
                            wq_rope.reshape(MLA_Q_LORA, -1),
                            swap(wq_rope).reshape(MLA_Q_LORA, -1)], axis=1).astype(BF16)
    scale = (MLA_NOPE_DIM + MLA_ROPE_DIM) ** -0.5
    qx = _matmul(cq, wq_x, BF16, tn=512, scale_cols=wq_x.shape[1], scale=scale)
    wkv = w_ukv.reshape(MLA_KV_LORA, hh, MLA_NOPE_DIM + MLA_V_DIM)
    wkv_x = jnp.concatenate([wkv[:, :, :MLA_NOPE_DIM].reshape(MLA_KV_LORA, -1),
                             wkv[:, :, MLA_NOPE_DIM:].reshape(MLA_KV_LORA, -1)], axis=1).astype(BF16)
    kv = _matmul(ckv, wkv_x, BF16, tn=512)
    return _mla_attention(qx, kv, kr, qc_tab, qs_tab, b, s)


def kernel(x, rel_bias, moba_w_qkv, moba_w_o, swa_w_qkv, swa_sinks, swa_w_o, nsa_w_in, nsa_cmp_pos,
           nsa_cmp_w1, nsa_cmp_w2, nsa_w_o, mla_w_down, mla_q_norm, mla_kv_norm, mla_w_uq, mla_w_ukv,
           mla_w_o, ln1_g, ln1_b, ffn_w_gate, ffn_w_up, ffn_w_down, ln2_g, ln2_b):
    b, s, d = x.shape
    depth = ln1_g.shape[0]
    n_mixers = 4
    xf = x.reshape(b * s, d).astype(F32)
    xb = xf.astype(BF16)
    moba_qkv_b, moba_o_b = moba_w_qkv, _to_bf16(moba_w_o)
    swa_qkv_b, swa_o_b = swa_w_qkv, _to_bf16(swa_w_o)
    nsa_in_b, nsa_o_b, nsa_w1_b = nsa_w_in, _to_bf16(nsa_w_o), _to_bf16(nsa_cmp_w1)
    mla_o_b = _to_bf16(mla_w_o)
    gate_b, up_b, down_b = ffn_w_gate, ffn_w_up, _to_bf16(ffn_w_down)
    nsa_main = (N_HEADS + 6 * NSA_KV_HEADS) * HEAD_DIM
    for i in range(depth):
        kind, j = i % n_mixers, i // n_mixers
        if kind == 0:
            o = _moba_mixer(xb, (moba_qkv_b, j), rel_bias, b, s)
            w_o = (moba_o_b, j)
        elif kind == 1:
            o = _swa_mixer(xb, (swa_qkv_b, j), swa_sinks[j], rel_bias, b, s)
            w_o = (swa_o_b, j)
        elif kind == 2:
            o = _nsa_mixer(xb, (nsa_in_b, j), nsa_w_in[j][:, nsa_main:], nsa_cmp_pos[j], nsa_w1_b[j],
                           nsa_cmp_w2[j], rel_bias, b, s)
            w_o = (nsa_o_b, j)
        else:
            o = _mla_mixer(xb, mla_w_down[j], mla_q_norm[j], mla_kv_norm[j], mla_w_uq[j], mla_w_ukv[j], b, s)
            w_o = (mla_o_b, j)
        xf, xb = _proj_ln(o, w_o, xf, ln1_g[i], ln1_b[i])
        hmid = _ffn_up(xb, (gate_b, i), (up_b, i))
        xf, xb = _proj_ln(hmid, (down_b, i), xf, ln2_g[i], ln2_b[i])
    return xf.reshape(b, s, d).astype(x.dtype)
```

```python
import functools
import math

import jax
import jax.numpy as jnp
import numpy as np
from jax import lax
from jax.experimental import pallas as pl
from jax.experimental.pallas import tpu as pltpu

F32 = jnp.float32
BF16 = jnp.bfloat16

N_HEADS = 16
HEAD_DIM = 128
REL_BUCKETS = 32
REL_MAX_DIST = 128
MOBA_BLOCK = 256
MOBA_TOPK = 3
SWA_WINDOW = 128
SWA_KV_HEADS = 2
NSA_KV_HEADS = 4
NSA_CMP_LEN = 32
NSA_CMP_STRIDE = 16
NSA_CMP_HIDDEN = 256
NSA_SEL_BLOCK = 64
NSA_TOPN = 16
NSA_WINDOW = 512
MLA_Q_LORA = 512
MLA_KV_LORA = 512
MLA_NOPE_DIM = 128
MLA_ROPE_DIM = 64
MLA_V_DIM = 128
ROPE_BASE = 10000.0
DEPTH = 4
ALPHA = (2 * DEPTH) ** 0.25
LN_EPS = 1e-5
RMS_EPS = 1e-6

LANES = 128
TILE = 256
BIG = 2 * TILE
NEG = -1e30
VMEM_LIMIT = 56 * 1024 * 1024

_NT = (((1,), (1,)), ((), ()))


def _cparams(sem):
    return pltpu.CompilerParams(dimension_semantics=sem, vmem_limit_bytes=VMEM_LIMIT)


def _matmul_kernel(x_ref, w_ref, o_ref, *, scale_tiles, scale):
    acc = jnp.dot(x_ref[...], w_ref[...].astype(BF16), preferred_element_type=F32)
    if scale_tiles:
        acc = acc * jnp.where(pl.program_id(1) < scale_tiles, scale, 1.0).astype(F32)
    o_ref[...] = acc.astype(o_ref.dtype)


CAST_BLOCK_BYTES = 6 * 1024 * 1024


def _cast_kernel(x_ref, o_ref):
    o_ref[...] = x_ref[...].astype(o_ref.dtype)


def _to_bf16(w):
    w2 = w.reshape(-1, w.shape[-1])
    rows, cols = w2.shape
    tr = rows
    while tr % 2 == 0 and tr > 16 and tr * cols * 4 > CAST_BLOCK_BYTES:
        tr //= 2
    out = pl.pallas_call(
        _cast_kernel,
        out_shape=jax.ShapeDtypeStruct((rows, cols), BF16),
        grid=(rows // tr,),
        in_specs=[pl.BlockSpec((tr, cols), lambda i: (i, 0))],
        out_specs=pl.BlockSpec((tr, cols), lambda i: (i, 0)),
        compiler_params=_cparams(("parallel",)),
        name="cast_bf16",
    )(w2)
    return out.reshape(w.shape)


def _wspec(w, rows, cols, index):
    if isinstance(w, tuple):
        layer = w[1]
        return w[0], pl.BlockSpec((None, rows, cols), lambda i, j: (layer,) + index(i, j))
    return w, pl.BlockSpec((rows, cols), index)


def _wshape(w):
    return w[0].shape[1:] if isinstance(w, tuple) else w.shape


def _matmul(x, w, out_dtype, *, tn, tm=1024, scale_cols=0, scale=1.0, n_out=None):
    m, k = x.shape
    n = _wshape(w)[1] if n_out is None else n_out
    w, w_spec = _wspec(w, k, tn, lambda i, j: (0, j))
    tm = min(tm, m)
    assert m % tm == 0 and n % tn == 0 and scale_cols % tn == 0
    kern = functools.partial(_matmul_kernel, scale_tiles=scale_cols // tn, scale=scale)
    return pl.pallas_call(
        kern,
        out_shape=jax.ShapeDtypeStruct((m, n), out_dtype),
        grid=(m // tm, n // tn),
        in_specs=[pl.BlockSpec((tm, k), lambda i, j: (i, 0)), w_spec],
        out_specs=pl.BlockSpec((tm, tn), lambda i, j: (i, j)),
        compiler_params=_cparams(("parallel", "arbitrary")),
        name="proj_matmul",
    )(x, w)


LN_SUB = 128
ACC_SUB = 512


def _proj_ln_kernel(a_ref, w_ref, x_ref, g_ref, b_ref, o_ref, ob_ref, *, nk):
    kk = pl.program_id(1)
    tm = a_ref.shape[0]

    def part(rows):
        return jnp.dot(a_ref[rows, :], w_ref[...], preferred_element_type=F32)

    def finish(rows, y):
        mu = jnp.mean(y, axis=-1, keepdims=True)
        yc = y - mu
        var = jnp.mean(yc * yc, axis=-1, keepdims=True)
        out = yc * lax.rsqrt(var + LN_EPS) * g_ref[...] + b_ref[...]
        o_ref[rows, :] = out
        ob_ref[rows, :] = out.astype(BF16)

    subs = [pl.ds(r, LN_SUB) for r in range(0, tm, LN_SUB)]
    if nk == 1:
        for rows in subs:
            finish(rows, ALPHA * x_ref[rows, :] + part(rows))
        return

    acc_subs = [pl.ds(r, min(ACC_SUB, tm)) for r in range(0, tm, ACC_SUB)]

    @pl.when(kk == 0)
    def _():
        for rows in acc_subs:
            o_ref[rows, :] = part(rows)

    @pl.when((kk > 0) & (kk < nk - 1))
    def _():
        for rows in acc_subs:
            o_ref[rows, :] += part(rows)

    @pl.when(kk == nk - 1)
    def _():
        for rows in subs:
            finish(rows, ALPHA * x_ref[rows, :] + (o_ref[rows, :] + part(rows)))


MAX_TK = 2048


def _proj_ln(a, w, x, g, b, *, tm=512, tk=None):
    m, k = a.shape
    d = _wshape(w)[1]
    tm = min(tm, m)
    if tk is None:
        tk = max(t for t in range(LANES, min(k, MAX_TK) + 1, LANES) if k % t == 0)
    assert m % tm == 0 and k % tk == 0 and tm % LN_SUB == 0
    w, w_spec = _wspec(w, tk, d, lambda i, j: (j, 0))
    return pl.pallas_call(
        functools.partial(_proj_ln_kernel, nk=k // tk),
        out_shape=(jax.ShapeDtypeStruct((m, d), F32), jax.ShapeDtypeStruct((m, d), BF16)),
        grid=(m // tm, k // tk),
        in_specs=[pl.BlockSpec((tm, tk), lambda i, j: (i, j)),
                  w_spec,
                  pl.BlockSpec((tm, d), lambda i, j: (i, 0)),
                  pl.BlockSpec((1, d), lambda i, j: (0, 0)),
                  pl.BlockSpec((1, d), lambda i, j: (0, 0))],
        out_specs=(pl.BlockSpec((tm, d), lambda i, j: (i, 0)),
                   pl.BlockSpec((tm, d), lambda i, j: (i, 0))),
        compiler_params=_cparams(("parallel", "arbitrary")),
        name="proj_deepnorm",
    )(a, w, x, g.reshape(1, d), b.reshape(1, d))


def _ffn_up_kernel(x_ref, wg_ref, wu_ref, o_ref):
    x = x_ref[...]
    gate = jnp.dot(x, wg_ref[...].astype(BF16), preferred_element_type=F32)
    up = jnp.dot(x, wu_ref[...].astype(BF16), preferred_element_type=F32)
    o_ref[...] = (gate * jax.nn.sigmoid(gate) * up).astype(o_ref.dtype)


def _ffn_up(x, wg, wu, *, tm=1024, tn=512):
    m, k = x.shape
    f = _wshape(wg)[1]
    tm = min(tm, m)
    tn = min(tn, f)
    assert m % tm == 0 and f % tn == 0
    wg, wg_spec = _wspec(wg, k, tn, lambda i, j: (0, j))
    wu, wu_spec = _wspec(wu, k, tn, lambda i, j: (0, j))
    return pl.pallas_call(
        _ffn_up_kernel,
        out_shape=jax.ShapeDtypeStruct((m, f), BF16),
        grid=(m // tm, f // tn),
        in_specs=[pl.BlockSpec((tm, k), lambda i, j: (i, 0)), wg_spec, wu_spec],
        out_specs=pl.BlockSpec((tm, tn), lambda i, j: (i, j)),
        compiler_params=_cparams(("parallel", "arbitrary")),
        name="ffn_gate_up",
    )(x, wg, wu)


def _t5_bucket(dist):
    n = np.maximum(dist, 0)
    max_exact = REL_BUCKETS // 2
    nf = np.maximum(n, max_exact).astype(np.float32)
    large = max_exact + (np.log(nf / max_exact) / math.log(REL_MAX_DIST / max_exact)
                         * (REL_BUCKETS - max_exact)).astype(np.int32)
    return np.where(n < max_exact, n, np.minimum(large, REL_BUCKETS - 1)).astype(np.int32)


def _bias_lookup(rel_bias, dist, visible):
    bucket = _t5_bucket(dist).reshape(-1)
    onehot = (jnp.arange(REL_BUCKETS, dtype=jnp.int32)[:, None] == jnp.asarray(bucket)[None, :]).astype(F32)
    tab = jnp.dot(rel_bias.T.astype(F32), onehot, precision=lax.Precision.HIGHEST)
    tab = tab.reshape((rel_bias.shape[1],) + dist.shape)
    return jnp.where(jnp.asarray(visible)[None], tab, NEG)


def _bias_tiles(rel_bias, window=None):
    r = np.arange(TILE)[:, None]
    c = np.arange(TILE)[None, :]
    d0 = r - c
    d1 = TILE + r - c
    ok0 = d0 >= 0
    ok1 = np.ones_like(ok0)
    if window is not None:
        ok0 = ok0 & (d0 < window)
        ok1 = d1 < window
    t0 = _bias_lookup(rel_bias, d0, ok0)
    t1 = _bias_lookup(rel_bias, d1, ok1)
    far = rel_bias[int(_t5_bucket(np.asarray(2 * TILE)))].astype(F32)
    return t0, t1, far


def _flash_update(s, v, m_ref, l_ref, acc_ref):
    m_prev = m_ref[...]
    m_new = jnp.maximum(m_prev, jnp.max(s, axis=1, keepdims=True))
    alpha = jnp.exp(m_prev - m_new)
    p = jnp.exp(s - jnp.concatenate([m_new] * (s.shape[1] // LANES), axis=1))
    l_ref[...] = alpha * l_ref[...] + jnp.sum(p, axis=1, keepdims=True)
    acc_ref[...] = alpha * acc_ref[...] + jnp.dot(p.astype(BF16), v, preferred_element_type=F32)
    m_ref[...] = m_new


def _flash_update_multi(svs, m_ref, l_ref, acc_ref):
    m_prev = m_ref[...]
    m_new = m_prev
    for s, _ in svs:
        m_new = jnp.maximum(m_new, jnp.max(s, axis=1, keepdims=True))
    alpha = jnp.exp(m_prev - m_new)
    l_new = alpha * l_ref[...]
    acc = alpha * acc_ref[...]
    for s, v in svs:
        p = jnp.exp(s - jnp.concatenate([m_new] * (s.shape[1] // LANES), axis=1))
        l_new = l_new + jnp.sum(p, axis=1, keepdims=True)
        acc = acc + jnp.dot(p.astype(BF16), v, preferred_element_type=F32)
    l_ref[...] = l_new
    acc_ref[...] = acc
    m_ref[...] = m_new


def _causal_chunk_schedule(ci, nq, step):
    for k in range(nq):
        @pl.when(ci == k)
        def _(k=k):
            order = [k] + list(range(k))
            for g in range(0, len(order), 3):
                step(k, order[g:g + 3])


def _flash_reset(m_ref, l_ref, acc_ref):
    m_ref[...] = jnp.full(m_ref.shape, NEG, F32)
    l_ref[...] = jnp.zeros(l_ref.shape, F32)
    acc_ref[...] = jnp.zeros(acc_ref.shape, F32)


def _rows(ref, t, tile=TILE):
    return ref[pl.ds(pl.multiple_of(t * tile, tile), tile), :]


def _split3(x):
    hi = x.astype(BF16)
    r1 = x - hi.astype(F32)
    mid = r1.astype(BF16)
    lo = (r1 - mid.astype(F32)).astype(BF16)
    return hi, mid, lo


def _rank_before(val, idx, n):
    cnt = jnp.zeros(val.shape, jnp.int32)
    for m in range(n):
        vm = val[m:m + 1, :]
        beats = (vm > val) | ((vm == val) & (m < idx))
        cnt = cnt + beats.astype(jnp.int32)
    return cnt


def _chosen_rows(chosen_t):
    n, q = chosen_t.shape
    padded = jnp.concatenate([chosen_t, jnp.zeros((LANES - n, q), F32)], axis=0)
    return padded.T.astype(BF16)


def _moba_kernel(q_ref, k_ref, v_ref, tab_ref, ex_ref, o_ref, m_ref, l_ref, acc_ref, km_ref, *, nblk):
    ci = pl.program_id(2)
    q = q_ref[...]

    @pl.when(ci == 0)
    def _():
        kmean = jnp.concatenate(
            [jnp.mean(k_ref[n * TILE:(n + 1) * TILE, :].astype(F32), axis=0, keepdims=True)
             for n in range(nblk)]
            + [jnp.zeros((LANES - nblk, HEAD_DIM), F32)], axis=0)
        for j, part in enumerate(_split3(kmean)):
            km_ref[j] = part

    nrow = -(-nblk // 8) * 8
    gate = sum(lax.dot_general(km_ref[j], q, _NT, preferred_element_type=F32) for j in range(3))[:nrow]
    blk = lax.broadcasted_iota(jnp.int32, gate.shape, 0)
    qblk = 2 * ci + lax.broadcasted_iota(jnp.int32, gate.shape, 1) // TILE
    past = blk < qblk
    gate = jnp.where(past, gate, -jnp.inf)
    cnt = _rank_before(gate, blk, nblk)
    chosen = _chosen_rows(jnp.where((past & (cnt < MOBA_TOPK)) | (blk == qblk), 1.0, 0.0))

    def chunk(c, k):
        rows = pl.ds(c * BIG, BIG)
        s = lax.dot_general(q, k_ref[rows, :], _NT, preferred_element_type=F32)
        hit = jnp.dot(chosen, ex_ref[c], preferred_element_type=F32)
        s = s + tab_ref[0, min(k - c, 2)] + (hit - 1.0) * (-NEG)
        return s, v_ref[rows, :]

    def step(k, chunks):
        _flash_update_multi([chunk(c, k) for c in chunks], m_ref, l_ref, acc_ref)

    _flash_reset(m_ref, l_ref, acc_ref)
    _causal_chunk_schedule(ci, nblk * TILE // BIG, step)
    o_ref[...] = (acc_ref[...] / l_ref[...]).astype(o_ref.dtype)


def _moba_attention(qkv, rel_bias, b, s):
    assert MOBA_BLOCK == TILE and s % BIG == 0
    nblk = s // TILE
    nq = s // BIG
    assert max(1, min(MOBA_TOPK, nblk - 1)) == MOBA_TOPK and nblk <= LANES
    hh = N_HEADS
    t0, t1, far = _bias_tiles(rel_bias)
    ff = jnp.broadcast_to(far[:, None, None], t0.shape)
    zz = jnp.zeros_like(t0)
    blk = lambda a, b_, c, d: jnp.concatenate([jnp.concatenate([a, b_], axis=2),
                                               jnp.concatenate([c, d], axis=2)], axis=1)
    tab = jnp.stack([blk(t0, zz, t1, t0), blk(ff, t1, ff, ff), blk(ff, ff, ff, ff)], axis=1)
    expand = jnp.asarray(np.arange(LANES)[None, :, None]
                         == (2 * np.arange(nq)[:, None, None] + np.arange(BIG)[None, None, :] // TILE), BF16)
    kern = functools.partial(_moba_kernel, nblk=nblk)
    return pl.pallas_call(
        kern,
        out_shape=jax.ShapeDtypeStruct((b * s, hh * HEAD_DIM), BF16),
        grid=(b, hh, nq),
        in_specs=[pl.BlockSpec((BIG, HEAD_DIM), lambda bi, h, i: (bi * nq + i, h)),
                  pl.BlockSpec((s, HEAD_DIM), lambda bi, h, i: (bi, hh + h)),
                  pl.BlockSpec((s, HEAD_DIM), lambda bi, h, i: (bi, 2 * hh + h)),
                  pl.BlockSpec((1, 3, BIG, BIG), lambda bi, h, i: (h, 0, 0, 0)),
                  pl.BlockSpec((nq, LANES, BIG), lambda bi, h, i: (0, 0, 0))],
        out_specs=pl.BlockSpec((BIG, HEAD_DIM), lambda bi, h, i: (bi * nq + i, h)),
        scratch_shapes=[pltpu.VMEM((BIG, LANES), F32), pltpu.VMEM((BIG, LANES), F32),
                        pltpu.VMEM((BIG, HEAD_DIM), F32), pltpu.VMEM((3, LANES, HEAD_DIM), BF16)],
        compiler_params=_cparams(("parallel", "parallel", "arbitrary")),
        name="moba_attention",
    )(qkv, qkv, qkv, tab, expand)


def _swa_kernel(sink_ref, q_ref, k_ref, v_ref, t0_ref, t1_ref, o_ref, m_ref, l_ref, acc_ref, *, group):
    hk = pl.program_id(0)
    i = pl.program_id(2)
    q = jnp.concatenate([q_ref[:, g * HEAD_DIM:(g + 1) * HEAD_DIM] for g in range(group)], axis=0)

    for g in range(group):
        m_ref[g * TILE:(g + 1) * TILE, :] = jnp.full((TILE, LANES), sink_ref[hk * group + g], F32)
    l_ref[...] = jnp.ones(l_ref.shape, F32)
    acc_ref[...] = jnp.zeros(acc_ref.shape, F32)

    s = lax.dot_general(q, _rows(k_ref, i), _NT, preferred_element_type=F32)
    s = s + t0_ref[...].reshape(group * TILE, TILE)
    _flash_update(s, _rows(v_ref, i), m_ref, l_ref, acc_ref)

    @pl.when(i > 0)
    def _():
        s = lax.dot_general(q, _rows(k_ref, i - 1), _NT, preferred_element_type=F32)
        s = s + t1_ref[...].reshape(group * TILE, TILE)
        _flash_update(s, _rows(v_ref, i - 1), m_ref, l_ref, acc_ref)

    out = acc_ref[...] / l_ref[...]
    for g in range(group):
        o_ref[:, g * HEAD_DIM:(g + 1) * HEAD_DIM] = out[g * TILE:(g + 1) * TILE].astype(o_ref.dtype)


def _swa_attention(qkv, sinks, rel_bias, b, s):
    assert SWA_WINDOW <= TILE and s % TILE == 0
    nq = s // TILE
    hk_n = SWA_KV_HEADS
    group = N_HEADS // hk_n
    t0, t1, _ = _bias_tiles(rel_bias, window=SWA_WINDOW)
    kern = functools.partial(_swa_kernel, group=group)
    gw = group * HEAD_DIM
    return pl.pallas_call(
        kern,
        out_shape=jax.ShapeDtypeStruct((b * s, N_HEADS * HEAD_DIM), BF16),
        grid=(hk_n, b, nq),
        in_specs=[pl.BlockSpec(memory_space=pltpu.SMEM),
                  pl.BlockSpec((TILE, gw), lambda hk, bi, i: (bi * nq + i, hk)),
                  pl.BlockSpec((s, HEAD_DIM), lambda hk, bi, i: (bi, N_HEADS + hk)),
                  pl.BlockSpec((s, HEAD_DIM), lambda hk, bi, i: (bi, N_HEADS + hk_n + hk)),
                  pl.BlockSpec((group, TILE, TILE), lambda hk, bi, i: (hk, 0, 0)),
                  pl.BlockSpec((group, TILE, TILE), lambda hk, bi, i: (hk, 0, 0))],
        out_specs=pl.BlockSpec((TILE, gw), lambda hk, bi, i: (bi * nq + i, hk)),
        scratch_shapes=[pltpu.VMEM((group * TILE, LANES), F32), pltpu.VMEM((group * TILE, LANES), F32),
                        pltpu.VMEM((group * TILE, HEAD_DIM), F32)],
        compiler_params=_cparams(("parallel", "parallel", "arbitrary")),
        name="swa_attention",
    )(sinks.astype(F32), qkv, qkv, qkv, t0, t1)


def _gelu_tanh(x):
    return 0.5 * x * (1.0 + jnp.tanh(math.sqrt(2.0 / math.pi) * (x + 0.044715 * (x * x * x))))


def _nsa_compress_kernel(t_ref, pe_ref, w1_ref, w2_ref, o_ref, tf_ref, *, ncmp_pad):
    stride = NSA_CMP_STRIDE
    half = stride * HEAD_DIM
    pe = pe_ref[0]
    tf_ref[...] = t_ref[...].astype(F32)
    xa, xb = [], []
    for r in range(stride):
        x = tf_ref[pl.ds(r, ncmp_pad, stride=stride), :]
        xa.append((x + pe[r:r + 1, :]).astype(BF16))
        xb.append((x + pe[stride + r:stride + r + 1, :]).astype(BF16))
    a = jnp.dot(jnp.concatenate(xa, axis=1), w1_ref[0, :half, :], preferred_element_type=F32)
    bm = jnp.dot(jnp.concatenate(xb, axis=1), w1_ref[0, half:, :], preferred_element_type=F32)
    hid = _gelu_tanh(a + pltpu.roll(bm, ncmp_pad - 1, 0))
    o_ref[0, 0, 0] = jnp.dot(hid.astype(BF16), w2_ref[0], preferred_element_type=F32).astype(o_ref.dtype)


def _nsa_compress(proj, pe, w1, w2, b, s):
    assert NSA_CMP_LEN == 2 * NSA_CMP_STRIDE and s % NSA_CMP_STRIDE == 0
    ncp = s // NSA_CMP_STRIDE
    hk_n = NSA_KV_HEADS
    kern = functools.partial(_nsa_compress_kernel, ncmp_pad=ncp)
    return pl.pallas_call(
        kern,
        out_shape=jax.ShapeDtypeStruct((2, b, hk_n, ncp, HEAD_DIM), BF16),
        grid=(2, b, hk_n),
        in_specs=[pl.BlockSpec((s, HEAD_DIM), lambda kv, bi, hk: (bi, N_HEADS + kv * hk_n + hk)),
                  pl.BlockSpec((1, NSA_CMP_LEN, HEAD_DIM), lambda kv, bi, hk: (kv, 0, 0)),
                  pl.BlockSpec((1, NSA_CMP_LEN * HEAD_DIM, NSA_CMP_HIDDEN), lambda kv, bi, hk: (kv, 0, 0)),
                  pl.BlockSpec((1, NSA_CMP_HIDDEN, HEAD_DIM), lambda kv, bi, hk: (kv, 0, 0))],
        out_specs=pl.BlockSpec((1, 1, 1, ncp, HEAD_DIM), lambda kv, bi, hk: (kv, bi, hk, 0, 0)),
        scratch_shapes=[pltpu.VMEM((s, HEAD_DIM), F32)],
        compiler_params=_cparams(("parallel", "parallel", "arbitrary")),
        name="nsa_compress",
    )(proj, pe.astype(F32), w1, w2)


def _nsa_kernel(far_ref, q_ref, kc_ref, vc_ref, cb_ref, ksl_ref, vsl_ref, kw_ref, vw_ref, gate_ref,
                t0_ref, t1_ref, tw_ref, ov_ref, ex_ref, o_ref,
                m_ref, l_ref, acc_ref, sel_ref, mw_ref, lw_ref, accw_ref, *, group, nq):
    hk = pl.program_id(1)
    i = pl.program_id(2)
    rows = group * TILE
    q = jnp.concatenate([q_ref[:, g * HEAD_DIM:(g + 1) * HEAD_DIM] for g in range(group)], axis=0)

    cb = cb_ref[...].reshape(rows, LANES)
    sc = lax.dot_general(q, kc_ref[0, 0, 0], _NT, preferred_element_type=F32) + cb
    valid = cb > 0.5 * NEG
    mc = jnp.max(sc, axis=1, keepdims=True)
    pc = jnp.where(valid, jnp.exp(sc - mc), 0.0)
    pc = pc / jnp.maximum(jnp.sum(pc, axis=1, keepdims=True), 1e-30)
    o_cmp = jnp.dot(pc.astype(BF16), vc_ref[0, 0, 0], preferred_element_type=F32)

    psum = pc[0:TILE]
    for g in range(1, group):
        psum = psum + pc[g * TILE:(g + 1) * TILE]
    n_sel = nq * (TILE // NSA_SEL_BLOCK)
    imp = sum(lax.dot_general(ov_ref[...], part, _NT, preferred_element_type=F32)
              for part in _split3(psum))[:n_sel]
    blk = lax.broadcasted_iota(jnp.int32, imp.shape, 0)
    qblk = (i * TILE + lax.broadcasted_iota(jnp.int32, imp.shape, 1)) // NSA_SEL_BLOCK
    forced = (blk == 0) | (blk == qblk) | (blk == qblk - 1)
    imp = jnp.where(blk > qblk, -jnp.inf, jnp.where(forced, jnp.inf, imp))
    cnt = _rank_before(imp, blk, n_sel)
    chosen = _chosen_rows(jnp.where((blk <= qblk) & (cnt < NSA_TOPN), 1.0, 0.0))
    for t in range(nq):
        @pl.when(t <= i)
        def _(t=t):
            hit = jnp.dot(chosen, ex_ref[:, t * TILE:(t + 1) * TILE], preferred_element_type=F32)
            sel_ref[t] = (hit - 1.0) * (-NEG)

    t0 = t0_ref[...].reshape(rows, TILE)

    def masked(s, t, with_far=False):
        sel = sel_ref[t]
        parts = []
        for g in range(group):
            add = sel + far_ref[hk * group + g] if with_far else sel
            parts.append(s[g * TILE:(g + 1) * TILE] + add)
        return jnp.concatenate(parts, axis=0)

    slc = (m_ref, l_ref, acc_ref)
    win = (mw_ref, lw_ref, accw_ref)
    _flash_reset(*slc)
    _flash_reset(*win)

    def logits(k_ref, t):
        return lax.dot_general(q, _rows(k_ref, t), _NT, preferred_element_type=F32)

    _flash_update(masked(logits(ksl_ref, i) + t0, i), _rows(vsl_ref, i), *slc)
    _flash_update(logits(kw_ref, i) + t0, _rows(vw_ref, i), *win)

    @pl.when(i > 0)
    def _():
        t1 = t1_ref[...].reshape(rows, TILE)
        _flash_update(masked(logits(ksl_ref, i - 1) + t1, i - 1), _rows(vsl_ref, i - 1), *slc)
        _flash_update(logits(kw_ref, i - 1) + t1, _rows(vw_ref, i - 1), *win)

    @pl.when(i > 1)
    def _():
        _flash_update(masked(logits(ksl_ref, i - 2), i - 2, with_far=True), _rows(vsl_ref, i - 2), *slc)
        _flash_update(logits(kw_ref, i - 2) + tw_ref[...].reshape(rows, TILE), _rows(vw_ref, i - 2), *win)

    def far_tile(t, carry):
        _flash_update(masked(logits(ksl_ref, t), t, with_far=True), _rows(vsl_ref, t), *slc)
        return carry

    lax.fori_loop(0, jnp.maximum(i - 2, 0), far_tile, 0)
    o_slc = acc_ref[...] / l_ref[...]
    o_win = accw_ref[...] / lw_ref[...]
    gates = jax.nn.sigmoid(gate_ref[...])
    for g in range(group):
        sl = slice(g * TILE, (g + 1) * TILE)
        o = (gates[:, 3 * g:3 * g + 1] * o_cmp[sl] + gates[:, 3 * g + 1:3 * g + 2] * o_slc[sl]
             + gates[:, 3 * g + 2:3 * g + 3] * o_win[sl])
        o_ref[:, g * HEAD_DIM:(g + 1) * HEAD_DIM] = o.astype(o_ref.dtype)


def _nsa_attention(proj, gates, cmp_kv, rel_bias, b, s):
    assert s % TILE == 0 and NSA_WINDOW == 2 * TILE and TILE % NSA_SEL_BLOCK == 0
    nq = s // TILE
    hk_n = NSA_KV_HEADS
    group = N_HEADS // hk_n
    gw = group * HEAD_DIM
    ncp = s // NSA_CMP_STRIDE
    n_cmp = (s - NSA_CMP_LEN) // NSA_CMP_STRIDE + 1
    n_sel = s // NSA_SEL_BLOCK
    assert ncp == LANES and n_sel <= LANES and min(NSA_TOPN, n_sel) == NSA_TOPN
    t0, t1, far = _bias_tiles(rel_bias)
    r = np.arange(TILE)[:, None]
    c = np.arange(TILE)[None, :]
    tw = jnp.where(jnp.asarray(c > r)[None], far[:, None, None], NEG)
    pos = np.arange(s)[:, None]
    cidx = np.arange(ncp)[None, :]
    cdist = pos - (cidx * NSA_CMP_STRIDE + NSA_CMP_LEN - 1)
    cmp_bias = _bias_lookup(rel_bias, cdist, (cdist >= 0) & (cidx < n_cmp))
    ci = np.arange(ncp)[:, None] * NSA_CMP_STRIDE
    sj = np.arange(LANES)[None, :] * NSA_SEL_BLOCK
    overlap = jnp.asarray(((ci < sj + NSA_SEL_BLOCK) & (ci + NSA_CMP_LEN > sj)
                           & (np.arange(ncp)[:, None] < n_cmp) & (np.arange(LANES)[None, :] < n_sel)).T, BF16)
    expand = jnp.asarray(np.arange(LANES)[:, None] == (np.arange(s)[None, :] // NSA_SEL_BLOCK), BF16)

    kern = functools.partial(_nsa_kernel, group=group, nq=nq)
    kv_spec = lambda off: pl.BlockSpec((s, HEAD_DIM), lambda bi, hk, i: (bi, N_HEADS + off * hk_n + hk))
    cmp_spec = lambda kv: pl.BlockSpec((1, 1, 1, ncp, HEAD_DIM), lambda bi, hk, i: (kv, bi, hk, 0, 0))
    tile_spec = pl.BlockSpec((group, TILE, TILE), lambda bi, hk, i: (hk, 0, 0))
    return pl.pallas_call(
        kern,
        out_shape=jax.ShapeDtypeStruct((b * s, N_HEADS * HEAD_DIM), BF16),
        grid=(b, hk_n, nq),
        in_specs=[pl.BlockSpec(memory_space=pltpu.SMEM),
                  pl.BlockSpec((TILE, gw), lambda bi, hk, i: (bi * nq + i, hk)),
                  cmp_spec(0), cmp_spec(1),
                  pl.BlockSpec((group, TILE, ncp), lambda bi, hk, i: (hk, i, 0)),
                  kv_spec(2), kv_spec(3), kv_spec(4), kv_spec(5),
                  pl.BlockSpec((TILE, LANES), lambda bi, hk, i: (bi * nq + i, hk)),
                  tile_spec, tile_spec, tile_spec,
                  pl.BlockSpec((ncp, LANES), lambda bi, hk, i: (0, 0)),
                  pl.BlockSpec((LANES, s), lambda bi, hk, i: (0, 0))],
        out_specs=pl.BlockSpec((TILE, gw), lambda bi, hk, i: (bi * nq + i, hk)),
        scratch_shapes=[pltpu.VMEM((group * TILE, LANES), F32), pltpu.VMEM((group * TILE, LANES), F32),
                        pltpu.VMEM((group * TILE, HEAD_DIM), F32),
                        pltpu.VMEM((nq, TILE, TILE), F32),
                        pltpu.VMEM((group * TILE, LANES), F32), pltpu.VMEM((group * TILE, LANES), F32),
                        pltpu.VMEM((group * TILE, HEAD_DIM), F32)],
        compiler_params=_cparams(("parallel", "parallel", "arbitrary")),
        name="nsa_attention",
    )(far, proj, cmp_kv, cmp_kv, cmp_bias, proj, proj, proj, proj, gates, t0, t1, tw, overlap, expand)


def _rms(x, g):
    return x * lax.rsqrt(jnp.mean(x * x, axis=-1, keepdims=True) + RMS_EPS) * g


def _mla_prep_kernel(c_ref, qg_ref, kg_ref, tab_ref, cq_ref, ckv_ref, kr_ref):
    c = c_ref[...]
    cq_ref[...] = _rms(c[:, :MLA_Q_LORA], qg_ref[...]).astype(BF16)
    ckv_ref[...] = _rms(c[:, MLA_Q_LORA:MLA_Q_LORA + MLA_KV_LORA], kg_ref[...]).astype(BF16)
    y = c[:, MLA_Q_LORA + MLA_KV_LORA:] * tab_ref[...]
    y = y + pltpu.roll(y, MLA_ROPE_DIM, 1)
    lane = lax.broadcasted_iota(jnp.int32, y.shape, 1)
    kr_ref[0] = jnp.where(lane < MLA_ROPE_DIM, y, 0.0).astype(BF16)
    kr_ref[1] = jnp.where(lane >= MLA_ROPE_DIM, y, 0.0).astype(BF16)


def _mla_prep(c, q_norm, kv_norm, ktab, s, *, tm=512):
    m, w = c.shape
    tm = min(tm, s)
    assert 2 * MLA_ROPE_DIM == LANES and w == MLA_Q_LORA + MLA_KV_LORA + LANES and s % tm == 0
    ns = s // tm
    return pl.pallas_call(
        _mla_prep_kernel,
        out_shape=(jax.ShapeDtypeStruct((m, MLA_Q_LORA), BF16),
                   jax.ShapeDtypeStruct((m, MLA_KV_LORA), BF16),
                   jax.ShapeDtypeStruct((2, m, LANES), BF16)),
        grid=(m // tm,),
        in_specs=[pl.BlockSpec((tm, w), lambda i: (i, 0)),
                  pl.BlockSpec((1, MLA_Q_LORA), lambda i: (0, 0)),
                  pl.BlockSpec((1, MLA_KV_LORA), lambda i: (0, 0)),
                  pl.BlockSpec((tm, LANES), lambda i: (i % ns, 0))],
        out_specs=(pl.BlockSpec((tm, MLA_Q_LORA), lambda i: (i, 0)),
                   pl.BlockSpec((tm, MLA_KV_LORA), lambda i: (i, 0)),
                   pl.BlockSpec((2, tm, LANES), lambda i: (0, i, 0))),
        compiler_params=_cparams(("parallel",)),
        name="mla_prep",
    )(c, q_norm.reshape(1, -1).astype(F32), kv_norm.reshape(1, -1).astype(F32), ktab)


def _mla_kernel(qn_ref, qr_ref, qs_ref, ct_ref, st_ref, kn_ref, kr_ref, v_ref, o_ref,
                m_ref, l_ref, acc_ref, kcat_ref):
    i = pl.program_id(2)

    @pl.when(i == 0)
    def _():
        kcat_ref[:, :HEAD_DIM] = kn_ref[...]
        kcat_ref[:, HEAD_DIM:] = kr_ref[0]

    qr = qr_ref[...].astype(F32) * ct_ref[...] + qs_ref[...].astype(F32) * st_ref[...]
    q = jnp.concatenate([qn_ref[...], qr.astype(BF16)], axis=1)

    def chunk(c, k):
        rows = pl.ds(c * BIG, BIG)
        s = lax.dot_general(q, kcat_ref[rows, :], _NT, preferred_element_type=F32)
        if c == k:
            r = lax.broadcasted_iota(jnp.int32, s.shape, 0)
            col = lax.broadcasted_iota(jnp.int32, s.shape, 1)
            s = jnp.where(r >= col, s, NEG)
        return s, v_ref[rows, :]

    def step(k, chunks):
        _flash_update_multi([chunk(c, k) for c in chunks], m_ref, l_ref, acc_ref)

    _flash_reset(m_ref, l_ref, acc_ref)
    _causal_chunk_schedule(i, kcat_ref.shape[0] // BIG, step)
    o_ref[...] = (acc_ref[...] / l_ref[...]).astype(o_ref.dtype)


def _mla_attention(qx, kv, kr, qc_tab, qs_tab, b, s):
    assert s % BIG == 0
    nq = s // BIG
    hh = N_HEADS
    return pl.pallas_call(
        _mla_kernel,
        out_shape=jax.ShapeDtypeStruct((b * s, hh * MLA_V_DIM), BF16),
        grid=(b, hh, nq),
        in_specs=[pl.BlockSpec((BIG, LANES), lambda bi, h, i: (bi * nq + i, h)),
                  pl.BlockSpec((BIG, LANES), lambda bi, h, i: (bi * nq + i, hh + h // 2)),
                  pl.BlockSpec((BIG, LANES), lambda bi, h, i: (bi * nq + i, hh + hh // 2 + h // 2)),
                  pl.BlockSpec((BIG, LANES), lambda bi, h, i: (i, 0)),
                  pl.BlockSpec((BIG, LANES), lambda bi, h, i: (i, 0)),
                  pl.BlockSpec((s, LANES), lambda bi, h, i: (bi, h)),
                  pl.BlockSpec((1, s, LANES), lambda bi, h, i: (h % 2, bi, 0)),
                  pl.BlockSpec((s, LANES), lambda bi, h, i: (bi, hh + h))],
        out_specs=pl.BlockSpec((BIG, LANES), lambda bi, h, i: (bi * nq + i, h)),
        scratch_shapes=[pltpu.VMEM((BIG, LANES), F32), pltpu.VMEM((BIG, LANES), F32),
                        pltpu.VMEM((BIG, MLA_V_DIM), F32), pltpu.VMEM((s, 2 * LANES), BF16)],
        compiler_params=_cparams(("parallel", "parallel", "arbitrary")),
        name="mla_attention",
    )(qx, qx, qx, qc_tab, qs_tab, kv, kr, kv)


def _bf(w):
    return w if isinstance(w, tuple) or w.dtype == BF16 else w.astype(BF16)


def _moba_mixer(xb, w_qkv, rel_bias, b, s):
    hd = N_HEADS * HEAD_DIM
    qkv = _matmul(xb, _bf(w_qkv), BF16, tn=512, scale_cols=hd, scale=HEAD_DIM ** -0.5)
    return _moba_attention(qkv, rel_bias, b, s)


def _swa_mixer(xb, w_qkv, sinks, rel_bias, b, s):
    hd = N_HEADS * HEAD_DIM
    qkv = _matmul(xb, _bf(w_qkv), BF16, tn=512, scale_cols=hd, scale=HEAD_DIM ** -0.5)
    return _swa_attention(qkv, sinks, rel_bias, b, s)


def _nsa_mixer(xb, w_in, w_gate, cmp_pos, cmp_w1, cmp_w2, rel_bias, b, s):
    hd = N_HEADS * HEAD_DIM
    hk_n = NSA_KV_HEADS
    group = N_HEADS // hk_n
    main = hd + 6 * hk_n * HEAD_DIM
    proj = _matmul(xb, _bf(w_in), BF16, tn=512, scale_cols=hd, scale=HEAD_DIM ** -0.5, n_out=main)
    wg = w_gate.reshape(-1, hk_n, 3 * group)
    wg = jnp.pad(wg, ((0, 0), (0, 0), (0, LANES - 3 * group))).reshape(-1, hk_n * LANES)
    gates = _matmul(xb, wg.astype(BF16), F32, tn=hk_n * LANES)
    cmp_kv = _nsa_compress(proj, cmp_pos, _bf(cmp_w1), _bf(cmp_w2), b, s)
    return _nsa_attention(proj, gates, cmp_kv, rel_bias, b, s)


def _mla_mixer(xb, w_down, q_norm, kv_norm, w_uq, w_ukv, b, s):
    hh = N_HEADS
    half = MLA_ROPE_DIM // 2
    lat = MLA_Q_LORA + MLA_KV_LORA
    swap = lambda t: jnp.concatenate([t[..., half:], t[..., :half]], axis=-1)
    w_down_x = jnp.concatenate([w_down, swap(w_down[:, lat:])], axis=1).astype(BF16)
    c = _matmul(xb, w_down_x, F32, tn=w_down_x.shape[1])

    inv = ROPE_BASE ** (-jnp.arange(0, MLA_ROPE_DIM, 2, dtype=F32) / MLA_ROPE_DIM)
    ang = jnp.arange(s, dtype=F32)[:, None] * inv[None, :]
    cos, sin = jnp.cos(ang), jnp.sin(ang)
    ktab = jnp.concatenate([cos, cos, -sin, sin], axis=1)
    qc_tab = jnp.concatenate([cos, cos, cos, cos], axis=1)
    qs_tab = jnp.concatenate([-sin, sin, -sin, sin], axis=1)
    cq, ckv, kr = _mla_prep(c, q_norm, kv_norm, ktab, s)

    wq = w_uq.reshape(MLA_Q_LORA, hh, MLA_NOPE_DIM + MLA_ROPE_DIM)
    wq_rope = wq[:, :, MLA_NOPE_DIM:]
    wq_x = jnp.concatenate([wq[:, :, :MLA_NOPE_DIM].reshape(MLA_Q_LORA, -1),
                            wq_rope.reshape(MLA_Q_LORA, -1),
                            swap(wq_rope).reshape(MLA_Q_LORA, -1)], axis=1).astype(BF16)
    scale = (MLA_NOPE_DIM + MLA_ROPE_DIM) ** -0.5
    qx = _matmul(cq, wq_x, BF16, tn=512, scale_cols=wq_x.shape[1], scale=scale)
    wkv = w_ukv.reshape(MLA_KV_LORA, hh, MLA_NOPE_DIM + MLA_V_DIM)
    wkv_x = jnp.concatenate([wkv[:, :, :MLA_NOPE_DIM].reshape(MLA_KV_LORA, -1),
                             wkv[:, :, MLA_NOPE_DIM:].reshape(MLA_KV_LORA, -1)], axis=1).astype(BF16)
    kv = _matmul(ckv, wkv_x, BF16, tn=512)
    return _mla_attention(qx, kv, kr, qc_tab, qs_tab, b, s)


def kernel(x, rel_bias, moba_w_qkv, moba_w_o, swa_w_qkv, swa_sinks, swa_w_o, nsa_w_in, nsa_cmp_pos,
           nsa_cmp_w1, nsa_cmp_w2, nsa_w_o, mla_w_down, mla_q_norm, mla_kv_norm, mla_w_uq, mla_w_ukv,
           mla_w_o, ln1_g, ln1_b, ffn_w_gate, ffn_w_up, ffn_w_down, ln2_g, ln2_b):
    b, s, d = x.shape
    depth = ln1_g.shape[0]
    n_mixers = 4
    xf = x.reshape(b * s, d).astype(F32)
    xb = xf.astype(BF16)
    moba_qkv_b, moba_o_b = moba_w_qkv, _to_bf16(moba_w_o)
    swa_qkv_b, swa_o_b = swa_w_qkv, _to_bf16(swa_w_o)
    nsa_in_b, nsa_o_b, nsa_w1_b = nsa_w_in, _to_bf16(nsa_w_o), _to_bf16(nsa_cmp_w1)
    mla_o_b = _to_bf16(mla_w_o)
    gate_b, up_b, down_b = ffn_w_gate, ffn_w_up, _to_bf16(ffn_w_down)
    nsa_main = (N_HEADS + 6 * NSA_KV_HEADS) * HEAD_DIM
    for i in range(depth):
        kind, j = i % n_mixers, i // n_mixers
        if kind == 0:
            o = _moba_mixer(xb, (moba_qkv_b, j), rel_bias, b, s)
            w_o = (moba_o_b, j)
        elif kind == 1:
            o = _swa_mixer(xb, (swa_qkv_b, j), swa_sinks[j], rel_bias, b, s)
            w_o = (swa_o_b, j)
        elif kind == 2:
            o = _nsa_mixer(xb, (nsa_in_b, j), nsa_w_in[j][:, nsa_main:], nsa_cmp_pos[j], nsa_w1_b[j],
                           nsa_cmp_w2[j], rel_bias, b, s)
            w_o = (nsa_o_b, j)
        else:
            o = _mla_mixer(xb, mla_w_down[j], mla_q_norm[j], mla_kv_norm[j], mla_w_uq[j], mla_w_ukv[j], b, s)
            w_o = (mla_o_b, j)
        xf, xb = _proj_ln(o, w_o, xf, ln1_g[i], ln1_b[i])
        hmid = _ffn_up(xb, (gate_b, i), (up_b, i))
        xf, xb = _proj_ln(hmid, (down_b, i), xf, ln2_g[i], ln2_b[i], tk=hmid.shape[1] // 2)
    return xf.reshape(b, s, d).astype(x.dtype)
```

```python
import functools
import math

import jax
import jax.numpy as jnp
import numpy as np
from jax import lax
from jax.experimental import pallas as pl
from jax.experimental.pallas import tpu as pltpu

F32 = jnp.float32
BF16 = jnp.bfloat16

N_HEADS = 16
HEAD_DIM = 128
REL_BUCKETS = 32
REL_MAX_DIST = 128
MOBA_BLOCK = 256
MOBA_TOPK = 3
SWA_WINDOW = 128
SWA_KV_HEADS = 2
NSA_KV_HEADS = 4
NSA_CMP_LEN = 32
NSA_CMP_STRIDE = 16
NSA_CMP_HIDDEN = 256
NSA_SEL_BLOCK = 64
NSA_TOPN = 16
NSA_WINDOW = 512
MLA_Q_LORA = 512
MLA_KV_LORA = 512
MLA_NOPE_DIM = 128
MLA_ROPE_DIM = 64
MLA_V_DIM = 128
ROPE_BASE = 10000.0
DEPTH = 4
ALPHA = (2 * DEPTH) ** 0.25
LN_EPS = 1e-5
RMS_EPS = 1e-6

LANES = 128
TILE = 256
BIG = 2 * TILE
NEG = -1e30
VMEM_LIMIT = 56 * 1024 * 1024

_NT = (((1,), (1,)), ((), ()))


def _cparams(sem):
    return pltpu.CompilerParams(dimension_semantics=sem, vmem_limit_bytes=VMEM_LIMIT)


def _matmul_kernel(x_ref, w_ref, o_ref, *, scale_tiles, scale):
    acc = jnp.dot(x_ref[...], w_ref[...].astype(BF16), preferred_element_type=F32)
    if scale_tiles:
        acc = acc * jnp.where(pl.program_id(1) < scale_tiles, scale, 1.0).astype(F32)
    o_ref[...] = acc.astype(o_ref.dtype)


CAST_BLOCK_BYTES = 6 * 1024 * 1024


def _cast_kernel(x_ref, o_ref):
    o_ref[...] = x_ref[...].astype(o_ref.dtype)


def _to_bf16(w):
    w2 = w.reshape(-1, w.shape[-1])
    rows, cols = w2.shape
    tr = rows
    while tr % 2 == 0 and tr > 16 and tr * cols * 4 > CAST_BLOCK_BYTES:
        tr //= 2
    out = pl.pallas_call(
        _cast_kernel,
        out_shape=jax.ShapeDtypeStruct((rows, cols), BF16),
        grid=(rows // tr,),
        in_specs=[pl.BlockSpec((tr, cols), lambda i: (i, 0))],
        out_specs=pl.BlockSpec((tr, cols), lambda i: (i, 0)),
        compiler_params=_cparams(("parallel",)),
        name="cast_bf16",
    )(w2)
    return out.reshape(w.shape)


def _wspec(w, rows, cols, index):
    if isinstance(w, tuple):
        layer = w[1]
        return w[0], pl.BlockSpec((None, rows, cols), lambda i, j: (layer,) + index(i, j))
    return w, pl.BlockSpec((rows, cols), index)


def _wshape(w):
    return w[0].shape[1:] if isinstance(w, tuple) else w.shape


def _matmul(x, w, out_dtype, *, tn, tm=1024, scale_cols=0, scale=1.0, n_out=None):
    m, k = x.shape
    n = _wshape(w)[1] if n_out is None else n_out
    w, w_spec = _wspec(w, k, tn, lambda i, j: (0, j))
    tm = min(tm, m)
    assert m % tm == 0 and n % tn == 0 and scale_cols % tn == 0
    kern = functools.partial(_matmul_kernel, scale_tiles=scale_cols // tn, scale=scale)
    return pl.pallas_call(
        kern,
        out_shape=jax.ShapeDtypeStruct((m, n), out_dtype),
        grid=(m // tm, n // tn),
        in_specs=[pl.BlockSpec((tm, k), lambda i, j: (i, 0)), w_spec],
        out_specs=pl.BlockSpec((tm, tn), lambda i, j: (i, j)),
        compiler_params=_cparams(("parallel", "arbitrary")),
        name="proj_matmul",
    )(x, w)


LN_SUB = 128
ACC_SUB = 512


def _proj_ln_kernel(a_ref, w_ref, x_ref, g_ref, b_ref, o_ref, ob_ref, *, nk):
    kk = pl.program_id(1)
    tm = a_ref.shape[0]

    def part(rows):
        return jnp.dot(a_ref[rows, :], w_ref[...], preferred_element_type=F32)

    def finish(rows, y):
        mu = jnp.mean(y, axis=-1, keepdims=True)
        yc = y - mu
        var = jnp.mean(yc * yc, axis=-1, keepdims=True)
        out = yc * lax.rsqrt(var + LN_EPS) * g_ref[...] + b_ref[...]
        o_ref[rows, :] = out
        ob_ref[rows, :] = out.astype(BF16)

    subs = [pl.ds(r, LN_SUB) for r in range(0, tm, LN_SUB)]
    if nk == 1:
        for rows in subs:
            finish(rows, ALPHA * x_ref[rows, :] + part(rows))
        return

    acc_subs = [pl.ds(r, min(ACC_SUB, tm)) for r in range(0, tm, ACC_SUB)]

    @pl.when(kk == 0)
    def _():
        for rows in acc_subs:
            o_ref[rows, :] = part(rows)

    @pl.when((kk > 0) & (kk < nk - 1))
    def _():
        for rows in acc_subs:
            o_ref[rows, :] += part(rows)

    @pl.when(kk == nk - 1)
    def _():
        for rows in subs:
            finish(rows, ALPHA * x_ref[rows, :] + (o_ref[rows, :] + part(rows)))


MAX_TK = 2048


def _proj_ln(a, w, x, g, b, *, tm=512, tk=None):
    m, k = a.shape
    d = _wshape(w)[1]
    tm = min(tm, m)
    if tk is None:
        tk = max(t for t in range(LANES, min(k, MAX_TK) + 1, LANES) if k % t == 0)
    assert m % tm == 0 and k % tk == 0 and tm % LN_SUB == 0
    w, w_spec = _wspec(w, tk, d, lambda i, j: (j, 0))
    return pl.pallas_call(
        functools.partial(_proj_ln_kernel, nk=k // tk),
        out_shape=(jax.ShapeDtypeStruct((m, d), F32), jax.ShapeDtypeStruct((m, d), BF16)),
        grid=(m // tm, k // tk),
        in_specs=[pl.BlockSpec((tm, tk), lambda i, j: (i, j)),
                  w_spec,
                  pl.BlockSpec((tm, d), lambda i, j: (i, 0)),
                  pl.BlockSpec((1, d), lambda i, j: (0, 0)),
                  pl.BlockSpec((1, d), lambda i, j: (0, 0))],
        out_specs=(pl.BlockSpec((tm, d), lambda i, j: (i, 0)),
                   pl.BlockSpec((tm, d), lambda i, j: (i, 0))),
        compiler_params=_cparams(("parallel", "arbitrary")),
        name="proj_deepnorm",
    )(a, w, x, g.reshape(1, d), b.reshape(1, d))


def _ffn_up_kernel(x_ref, wg_ref, wu_ref, o_ref):
    x = x_ref[...]
    gate = jnp.dot(x, wg_ref[...].astype(BF16), preferred_element_type=F32)
    up = jnp.dot(x, wu_ref[...].astype(BF16), preferred_element_type=F32)
    o_ref[...] = (gate * jax.nn.sigmoid(gate) * up).astype(o_ref.dtype)


def _ffn_up(x, wg, wu, *, tm=1024, tn=512):
    m, k = x.shape
    f = _wshape(wg)[1]
    tm = min(tm, m)
    tn = min(tn, f)
    assert m % tm == 0 and f % tn == 0
    wg, wg_spec = _wspec(wg, k, tn, lambda i, j: (0, j))
    wu, wu_spec = _wspec(wu, k, tn, lambda i, j: (0, j))
    return pl.pallas_call(
        _ffn_up_kernel,
        out_shape=jax.ShapeDtypeStruct((m, f), BF16),
        grid=(m // tm, f // tn),
        in_specs=[pl.BlockSpec((tm, k), lambda i, j: (i, 0)), wg_spec, wu_spec],
        out_specs=pl.BlockSpec((tm, tn), lambda i, j: (i, j)),
        compiler_params=_cparams(("parallel", "arbitrary")),
        name="ffn_gate_up",
    )(x, wg, wu)


def _t5_bucket(dist):
    n = np.maximum(dist, 0)
    max_exact = REL_BUCKETS // 2
    nf = np.maximum(n, max_exact).astype(np.float32)
    large = max_exact + (np.log(nf / max_exact) / math.log(REL_MAX_DIST / max_exact)
                         * (REL_BUCKETS - max_exact)).astype(np.int32)
    return np.where(n < max_exact, n, np.minimum(large, REL_BUCKETS - 1)).astype(np.int32)


def _bias_lookup(rel_bias, dist, visible):
    bucket = _t5_bucket(dist).reshape(-1)
    onehot = (jnp.arange(REL_BUCKETS, dtype=jnp.int32)[:, None] == jnp.asarray(bucket)[None, :]).astype(F32)
    tab = jnp.dot(rel_bias.T.astype(F32), onehot, precision=lax.Precision.HIGHEST)
    tab = tab.reshape((rel_bias.shape[1],) + dist.shape)
    return jnp.where(jnp.asarray(visible)[None], tab, NEG)


def _bias_tiles(rel_bias, window=None):
    r = np.arange(TILE)[:, None]
    c = np.arange(TILE)[None, :]
    d0 = r - c
    d1 = TILE + r - c
    ok0 = d0 >= 0
    ok1 = np.ones_like(ok0)
    if window is not None:
        ok0 = ok0 & (d0 < window)
        ok1 = d1 < window
    t0 = _bias_lookup(rel_bias, d0, ok0)
    t1 = _bias_lookup(rel_bias, d1, ok1)
    far = rel_bias[int(_t5_bucket(np.asarray(2 * TILE)))].astype(F32)
    return t0, t1, far


def _flash_update(s, v, m_ref, l_ref, acc_ref):
    m_prev = m_ref[...]
    m_new = jnp.maximum(m_prev, jnp.max(s, axis=1, keepdims=True))
    alpha = jnp.exp(m_prev - m_new)
    p = jnp.exp(s - jnp.concatenate([m_new] * (s.shape[1] // LANES), axis=1))
    l_ref[...] = alpha * l_ref[...] + jnp.sum(p, axis=1, keepdims=True)
    acc_ref[...] = alpha * acc_ref[...] + jnp.dot(p.astype(BF16), v, preferred_element_type=F32)
    m_ref[...] = m_new


def _flash_update_multi(svs, m_ref, l_ref, acc_ref):
    m_prev = m_ref[...]
    m_new = m_prev
    for s, _ in svs:
        m_new = jnp.maximum(m_new, jnp.max(s, axis=1, keepdims=True))
    alpha = jnp.exp(m_prev - m_new)
    l_new = alpha * l_ref[...]
    acc = alpha * acc_ref[...]
    for s, v in svs:
        p = jnp.exp(s - jnp.concatenate([m_new] * (s.shape[1] // LANES), axis=1))
        l_new = l_new + jnp.sum(p, axis=1, keepdims=True)
        acc = acc + jnp.dot(p.astype(BF16), v, preferred_element_type=F32)
    l_ref[...] = l_new
    acc_ref[...] = acc
    m_ref[...] = m_new


def _causal_chunk_schedule(ci, nq, step):
    for k in range(nq):
        @pl.when(ci == k)
        def _(k=k):
            order = [k] + list(range(k))
            for g in range(0, len(order), 2):
                step(k, order[g:g + 2])


def _flash_reset(m_ref, l_ref, acc_ref):
    m_ref[...] = jnp.full(m_ref.shape, NEG, F32)
    l_ref[...] = jnp.zeros(l_ref.shape, F32)
    acc_ref[...] = jnp.zeros(acc_ref.shape, F32)


def _rows(ref, t, tile=TILE):
    return ref[pl.ds(pl.multiple_of(t * tile, tile), tile), :]


def _split3(x):
    hi = x.astype(BF16)
    r1 = x - hi.astype(F32)
    mid = r1.astype(BF16)
    lo = (r1 - mid.astype(F32)).astype(BF16)
    return hi, mid, lo


def _rank_before(val, idx, n):
    cnt = jnp.zeros(val.shape, jnp.int32)
    for m in range(n):
        vm = val[m:m + 1, :]
        beats = (vm > val) | ((vm == val) & (m < idx))
        cnt = cnt + beats.astype(jnp.int32)
    return cnt


def _chosen_rows(chosen_t):
    n, q = chosen_t.shape
    padded = jnp.concatenate([chosen_t, jnp.zeros((LANES - n, q), F32)], axis=0)
    return padded.T.astype(BF16)


def _moba_kernel(q_ref, k_ref, v_ref, tab_ref, ex_ref, o_ref, m_ref, l_ref, acc_ref, km_ref, *, nblk):
    ci = pl.program_id(2)
    q = q_ref[...]

    @pl.when(ci == 0)
    def _():
        kmean = jnp.concatenate(
            [jnp.mean(k_ref[n * TILE:(n + 1) * TILE, :].astype(F32), axis=0, keepdims=True)
             for n in range(nblk)]
            + [jnp.zeros((LANES - nblk, HEAD_DIM), F32)], axis=0)
        for j, part in enumerate(_split3(kmean)):
            km_ref[j] = part

    nrow = -(-nblk // 8) * 8
    gate = sum(lax.dot_general(km_ref[j], q, _NT, preferred_element_type=F32) for j in range(3))[:nrow]
    blk = lax.broadcasted_iota(jnp.int32, gate.shape, 0)
    qblk = 2 * ci + lax.broadcasted_iota(jnp.int32, gate.shape, 1) // TILE
    past = blk < qblk
    gate = jnp.where(past, gate, -jnp.inf)
    cnt = _rank_before(gate, blk, nblk)
    chosen = _chosen_rows(jnp.where((past & (cnt < MOBA_TOPK)) | (blk == qblk), 1.0, 0.0))

    def chunk(c, k):
        rows = pl.ds(c * BIG, BIG)
        s = lax.dot_general(q, k_ref[rows, :], _NT, preferred_element_type=F32)
        hit = jnp.dot(chosen, ex_ref[c], preferred_element_type=F32)
        s = s + tab_ref[0, min(k - c, 2)] + (hit - 1.0) * (-NEG)
        return s, v_ref[rows, :]

    def step(k, chunks):
        _flash_update_multi([chunk(c, k) for c in chunks], m_ref, l_ref, acc_ref)

    _flash_reset(m_ref, l_ref, acc_ref)
    _causal_chunk_schedule(ci, nblk * TILE // BIG, step)
    o_ref[...] = (acc_ref[...] / l_ref[...]).astype(o_ref.dtype)


def _moba_attention(qkv, rel_bias, b, s):
    assert MOBA_BLOCK == TILE and s % BIG == 0
    nblk = s // TILE
    nq = s // BIG
    assert max(1, min(MOBA_TOPK, nblk - 1)) == MOBA_TOPK and nblk <= LANES
    hh = N_HEADS
    t0, t1, far = _bias_tiles(rel_bias)
    ff = jnp.broadcast_to(far[:, None, None], t0.shape)
    zz = jnp.zeros_like(t0)
    blk = lambda a, b_, c, d: jnp.concatenate([jnp.concatenate([a, b_], axis=2),
                                               jnp.concatenate([c, d], axis=2)], axis=1)
    tab = jnp.stack([blk(t0, zz, t1, t0), blk(ff, t1, ff, ff), blk(ff, ff, ff, ff)], axis=1)
    expand = jnp.asarray(np.arange(LANES)[None, :, None]
                         == (2 * np.arange(nq)[:, None, None] + np.arange(BIG)[None, None, :] // TILE), BF16)
    kern = functools.partial(_moba_kernel, nblk=nblk)
    return pl.pallas_call(
        kern,
        out_shape=jax.ShapeDtypeStruct((b * s, hh * HEAD_DIM), BF16),
        grid=(b, hh, nq),
        in_specs=[pl.BlockSpec((BIG, HEAD_DIM), lambda bi, h, i: (bi * nq + i, h)),
                  pl.BlockSpec((s, HEAD_DIM), lambda bi, h, i: (bi, hh + h)),
                  pl.BlockSpec((s, HEAD_DIM), lambda bi, h, i: (bi, 2 * hh + h)),
                  pl.BlockSpec((1, 3, BIG, BIG), lambda bi, h, i: (h, 0, 0, 0)),
                  pl.BlockSpec((nq, LANES, BIG), lambda bi, h, i: (0, 0, 0))],
        out_specs=pl.BlockSpec((BIG, HEAD_DIM), lambda bi, h, i: (bi * nq + i, h)),
        scratch_shapes=[pltpu.VMEM((BIG, LANES), F32), pltpu.VMEM((BIG, LANES), F32),
                        pltpu.VMEM((BIG, HEAD_DIM), F32), pltpu.VMEM((3, LANES, HEAD_DIM), BF16)],
        compiler_params=_cparams(("parallel", "parallel", "arbitrary")),
        name="moba_attention",
    )(qkv, qkv, qkv, tab, expand)


def _swa_kernel(sink_ref, q_ref, k_ref, v_ref, t0_ref, t1_ref, o_ref, m_ref, l_ref, acc_ref, *, group):
    hk = pl.program_id(0)
    i = pl.program_id(2)
    q = jnp.concatenate([q_ref[:, g * HEAD_DIM:(g + 1) * HEAD_DIM] for g in range(group)], axis=0)

    for g in range(group):
        m_ref[g * TILE:(g + 1) * TILE, :] = jnp.full((TILE, LANES), sink_ref[hk * group + g], F32)
    l_ref[...] = jnp.ones(l_ref.shape, F32)
    acc_ref[...] = jnp.zeros(acc_ref.shape, F32)

    s = lax.dot_general(q, _rows(k_ref, i), _NT, preferred_element_type=F32)
    s = s + t0_ref[...].reshape(group * TILE, TILE)
    _flash_update(s, _rows(v_ref, i), m_ref, l_ref, acc_ref)

    @pl.when(i > 0)
    def _():
        s = lax.dot_general(q, _rows(k_ref, i - 1), _NT, preferred_element_type=F32)
        s = s + t1_ref[...].reshape(group * TILE, TILE)
        _flash_update(s, _rows(v_ref, i - 1), m_ref, l_ref, acc_ref)

    out = acc_ref[...] / l_ref[...]
    for g in range(group):
        o_ref[:, g * HEAD_DIM:(g + 1) * HEAD_DIM] = out[g * TILE:(g + 1) * TILE].astype(o_ref.dtype)


def _swa_attention(qkv, sinks, rel_bias, b, s):
    assert SWA_WINDOW <= TILE and s % TILE == 0
    nq = s // TILE
    hk_n = SWA_KV_HEADS
    group = N_HEADS // hk_n
    t0, t1, _ = _bias_tiles(rel_bias, window=SWA_WINDOW)
    kern = functools.partial(_swa_kernel, group=group)
    gw = group * HEAD_DIM
    return pl.pallas_call(
        kern,
        out_shape=jax.ShapeDtypeStruct((b * s, N_HEADS * HEAD_DIM), BF16),
        grid=(hk_n, b, nq),
        in_specs=[pl.BlockSpec(memory_space=pltpu.SMEM),
                  pl.BlockSpec((TILE, gw), lambda hk, bi, i: (bi * nq + i, hk)),
                  pl.BlockSpec((s, HEAD_DIM), lambda hk, bi, i: (bi, N_HEADS + hk)),
                  pl.BlockSpec((s, HEAD_DIM), lambda hk, bi, i: (bi, N_HEADS + hk_n + hk)),
                  pl.BlockSpec((group, TILE, TILE), lambda hk, bi, i: (hk, 0, 0)),
                  pl.BlockSpec((group, TILE, TILE), lambda hk, bi, i: (hk, 0, 0))],
        out_specs=pl.BlockSpec((TILE, gw), lambda hk, bi, i: (bi * nq + i, hk)),
        scratch_shapes=[pltpu.VMEM((group * TILE, LANES), F32), pltpu.VMEM((group * TILE, LANES), F32),
                        pltpu.VMEM((group * TILE, HEAD_DIM), F32)],
        compiler_params=_cparams(("parallel", "parallel", "arbitrary")),
        name="swa_attention",
    )(sinks.astype(F32), qkv, qkv, qkv, t0, t1)


def _gelu_tanh(x):
    return 0.5 * x * (1.0 + jnp.tanh(math.sqrt(2.0 / math.pi) * (x + 0.044715 * (x * x * x))))


def _nsa_compress_kernel(t_ref, pe_ref, w1_ref, w2_ref, o_ref, tf_ref, *, ncmp_pad):
    stride = NSA_CMP_STRIDE
    half = stride * HEAD_DIM
    pe = pe_ref[0]
    tf_ref[...] = t_ref[...].astype(F32)
    xa, xb = [], []
    for r in range(stride):
        x = tf_ref[pl.ds(r, ncmp_pad, stride=stride), :]
        xa.append((x + pe[r:r + 1, :]).astype(BF16))
        xb.append((x + pe[stride + r:stride + r + 1, :]).astype(BF16))
    a = jnp.dot(jnp.concatenate(xa, axis=1), w1_ref[0, :half, :], preferred_element_type=F32)
    bm = jnp.dot(jnp.concatenate(xb, axis=1), w1_ref[0, half:, :], preferred_element_type=F32)
    hid = _gelu_tanh(a + pltpu.roll(bm, ncmp_pad - 1, 0))
    o_ref[0, 0, 0] = jnp.dot(hid.astype(BF16), w2_ref[0], preferred_element_type=F32).astype(o_ref.dtype)


def _nsa_compress(proj, pe, w1, w2, b, s):
    assert NSA_CMP_LEN == 2 * NSA_CMP_STRIDE and s % NSA_CMP_STRIDE == 0
    ncp = s // NSA_CMP_STRIDE
    hk_n = NSA_KV_HEADS
    kern = functools.partial(_nsa_compress_kernel, ncmp_pad=ncp)
    return pl.pallas_call(
        kern,
        out_shape=jax.ShapeDtypeStruct((2, b, hk_n, ncp, HEAD_DIM), BF16),
        grid=(2, b, hk_n),
        in_specs=[pl.BlockSpec((s, HEAD_DIM), lambda kv, bi, hk: (bi, N_HEADS + kv * hk_n + hk)),
                  pl.BlockSpec((1, NSA_CMP_LEN, HEAD_DIM), lambda kv, bi, hk: (kv, 0, 0)),
                  pl.BlockSpec((1, NSA_CMP_LEN * HEAD_DIM, NSA_CMP_HIDDEN), lambda kv, bi, hk: (kv, 0, 0)),
                  pl.BlockSpec((1, NSA_CMP_HIDDEN, HEAD_DIM), lambda kv, bi, hk: (kv, 0, 0))],
        out_specs=pl.BlockSpec((1, 1, 1, ncp, HEAD_DIM), lambda kv, bi, hk: (kv, bi, hk, 0, 0)),
        scratch_shapes=[pltpu.VMEM((s, HEAD_DIM), F32)],
        compiler_params=_cparams(("parallel", "parallel", "arbitrary")),
        name="nsa_compress",
    )(proj, pe.astype(F32), w1, w2)


def _nsa_kernel(far_ref, q_ref, kc_ref, vc_ref, cb_ref, ksl_ref, vsl_ref, kw_ref, vw_ref, gate_ref,
                t0_ref, t1_ref, tw_ref, ov_ref, ex_ref, o_ref,
                m_ref, l_ref, acc_ref, sel_ref, mw_ref, lw_ref, accw_ref, *, group, nq):
    hk = pl.program_id(1)
    i = pl.program_id(2)
    rows = group * TILE
    q = jnp.concatenate([q_ref[:, g * HEAD_DIM:(g + 1) * HEAD_DIM] for g in range(group)], axis=0)

    cb = cb_ref[...].reshape(rows, LANES)
    sc = lax.dot_general(q, kc_ref[0, 0, 0], _NT, preferred_element_type=F32) + cb
    valid = cb > 0.5 * NEG
    mc = jnp.max(sc, axis=1, keepdims=True)
    pc = jnp.where(valid, jnp.exp(sc - mc), 0.0)
    pc = pc / jnp.maximum(jnp.sum(pc, axis=1, keepdims=True), 1e-30)
    o_cmp = jnp.dot(pc.astype(BF16), vc_ref[0, 0, 0], preferred_element_type=F32)

    psum = pc[0:TILE]
    for g in range(1, group):
        psum = psum + pc[g * TILE:(g + 1) * TILE]
    n_sel = nq * (TILE // NSA_SEL_BLOCK)
    imp = sum(lax.dot_general(ov_ref[...], part, _NT, preferred_element_type=F32)
              for part in _split3(psum))[:n_sel]
    blk = lax.broadcasted_iota(jnp.int32, imp.shape, 0)
    qblk = (i * TILE + lax.broadcasted_iota(jnp.int32, imp.shape, 1)) // NSA_SEL_BLOCK
    forced = (blk == 0) | (blk == qblk) | (blk == qblk - 1)
    imp = jnp.where(blk > qblk, -jnp.inf, jnp.where(forced, jnp.inf, imp))
    cnt = _rank_before(imp, blk, n_sel)
    chosen = _chosen_rows(jnp.where((blk <= qblk) & (cnt < NSA_TOPN), 1.0, 0.0))
    for t in range(nq):
        @pl.when(t <= i)
        def _(t=t):
            hit = jnp.dot(chosen, ex_ref[:, t * TILE:(t + 1) * TILE], preferred_element_type=F32)
            sel_ref[t] = (hit - 1.0) * (-NEG)

    t0 = t0_ref[...].reshape(rows, TILE)

    def masked(s, t, with_far=False):
        sel = sel_ref[t]
        parts = []
        for g in range(group):
            add = sel + far_ref[hk * group + g] if with_far else sel
            parts.append(s[g * TILE:(g + 1) * TILE] + add)
        return jnp.concatenate(parts, axis=0)

    slc = (m_ref, l_ref, acc_ref)
    win = (mw_ref, lw_ref, accw_ref)
    _flash_reset(*slc)
    _flash_reset(*win)

    def logits(k_ref, t):
        return lax.dot_general(q, _rows(k_ref, t), _NT, preferred_element_type=F32)

    _flash_update(masked(logits(ksl_ref, i) + t0, i), _rows(vsl_ref, i), *slc)
    _flash_update(logits(kw_ref, i) + t0, _rows(vw_ref, i), *win)

    @pl.when(i > 0)
    def _():
        t1 = t1_ref[...].reshape(rows, TILE)
        _flash_update(masked(logits(ksl_ref, i - 1) + t1, i - 1), _rows(vsl_ref, i - 1), *slc)
        _flash_update(logits(kw_ref, i - 1) + t1, _rows(vw_ref, i - 1), *win)

    @pl.when(i > 1)
    def _():
        _flash_update(masked(logits(ksl_ref, i - 2), i - 2, with_far=True), _rows(vsl_ref, i - 2), *slc)
        _flash_update(logits(kw_ref, i - 2) + tw_ref[...].reshape(rows, TILE), _rows(vw_ref, i - 2), *win)

    def far_tile(t, carry):
        _flash_update(masked(logits(ksl_ref, t), t, with_far=True), _rows(vsl_ref, t), *slc)
        return carry

    lax.fori_loop(0, jnp.maximum(i - 2, 0), far_tile, 0)
    o_slc = acc_ref[...] / l_ref[...]
    o_win = accw_ref[...] / lw_ref[...]
    gates = jax.nn.sigmoid(gate_ref[...])
    for g in range(group):
        sl = slice(g * TILE, (g + 1) * TILE)
        o = (gates[:, 3 * g:3 * g + 1] * o_cmp[sl] + gates[:, 3 * g + 1:3 * g + 2] * o_slc[sl]
             + gates[:, 3 * g + 2:3 * g + 3] * o_win[sl])
        o_ref[:, g * HEAD_DIM:(g + 1) * HEAD_DIM] = o.astype(o_ref.dtype)


def _nsa_attention(proj, gates, cmp_kv, rel_bias, b, s):
    assert s % TILE == 0 and NSA_WINDOW == 2 * TILE and TILE % NSA_SEL_BLOCK == 0
    nq = s // TILE
    hk_n = NSA_KV_HEADS
    group = N_HEADS // hk_n
    gw = group * HEAD_DIM
    ncp = s // NSA_CMP_STRIDE
    n_cmp = (s - NSA_CMP_LEN) // NSA_CMP_STRIDE + 1
    n_sel = s // NSA_SEL_BLOCK
    assert ncp == LANES and n_sel <= LANES and min(NSA_TOPN, n_sel) == NSA_TOPN
    t0, t1, far = _bias_tiles(rel_bias)
    r = np.arange(TILE)[:, None]
    c = np.arange(TILE)[None, :]
    tw = jnp.where(jnp.asarray(c > r)[None], far[:, None, None], NEG)
    pos = np.arange(s)[:, None]
    cidx = np.arange(ncp)[None, :]
    cdist = pos - (cidx * NSA_CMP_STRIDE + NSA_CMP_LEN - 1)
    cmp_bias = _bias_lookup(rel_bias, cdist, (cdist >= 0) & (cidx < n_cmp))
    ci = np.arange(ncp)[:, None] * NSA_CMP_STRIDE
    sj = np.arange(LANES)[None, :] * NSA_SEL_BLOCK
    overlap = jnp.asarray(((ci < sj + NSA_SEL_BLOCK) & (ci + NSA_CMP_LEN > sj)
                           & (np.arange(ncp)[:, None] < n_cmp) & (np.arange(LANES)[None, :] < n_sel)).T, BF16)
    expand = jnp.asarray(np.arange(LANES)[:, None] == (np.arange(s)[None, :] // NSA_SEL_BLOCK), BF16)

    kern = functools.partial(_nsa_kernel, group=group, nq=nq)
    kv_spec = lambda off: pl.BlockSpec((s, HEAD_DIM), lambda bi, hk, i: (bi, N_HEADS + off * hk_n + hk))
    cmp_spec = lambda kv: pl.BlockSpec((1, 1, 1, ncp, HEAD_DIM), lambda bi, hk, i: (kv, bi, hk, 0, 0))
    tile_spec = pl.BlockSpec((group, TILE, TILE), lambda bi, hk, i: (hk, 0, 0))
    return pl.pallas_call(
        kern,
        out_shape=jax.ShapeDtypeStruct((b * s, N_HEADS * HEAD_DIM), BF16),
        grid=(b, hk_n, nq),
        in_specs=[pl.BlockSpec(memory_space=pltpu.SMEM),
                  pl.BlockSpec((TILE, gw), lambda bi, hk, i: (bi * nq + i, hk)),
                  cmp_spec(0), cmp_spec(1),
                  pl.BlockSpec((group, TILE, ncp), lambda bi, hk, i: (hk, i, 0)),
                  kv_spec(2), kv_spec(3), kv_spec(4), kv_spec(5),
                  pl.BlockSpec((TILE, LANES), lambda bi, hk, i: (bi * nq + i, hk)),
                  tile_spec, tile_spec, tile_spec,
                  pl.BlockSpec((ncp, LANES), lambda bi, hk, i: (0, 0)),
                  pl.BlockSpec((LANES, s), lambda bi, hk, i: (0, 0))],
        out_specs=pl.BlockSpec((TILE, gw), lambda bi, hk, i: (bi * nq + i, hk)),
        scratch_shapes=[pltpu.VMEM((group * TILE, LANES), F32), pltpu.VMEM((group * TILE, LANES), F32),
                        pltpu.VMEM((group * TILE, HEAD_DIM), F32),
                        pltpu.VMEM((nq, TILE, TILE), F32),
                        pltpu.VMEM((group * TILE, LANES), F32), pltpu.VMEM((group * TILE, LANES), F32),
                        pltpu.VMEM((group * TILE, HEAD_DIM), F32)],
        compiler_params=_cparams(("parallel", "parallel", "arbitrary")),
        name="nsa_attention",
    )(far, proj, cmp_kv, cmp_kv, cmp_bias, proj, proj, proj, proj, gates, t0, t1, tw, overlap, expand)


def _rms(x, g):
    return x * lax.rsqrt(jnp.mean(x * x, axis=-1, keepdims=True) + RMS_EPS) * g


def _mla_prep_kernel(c_ref, qg_ref, kg_ref, tab_ref, cq_ref, ckv_ref, kr_ref):
    c = c_ref[...]
    cq_ref[...] = _rms(c[:, :MLA_Q_LORA], qg_ref[...]).astype(BF16)
    ckv_ref[...] = _rms(c[:, MLA_Q_LORA:MLA_Q_LORA + MLA_KV_LORA], kg_ref[...]).astype(BF16)
    y = c[:, MLA_Q_LORA + MLA_KV_LORA:] * tab_ref[...]
    y = y + pltpu.roll(y, MLA_ROPE_DIM, 1)
    lane = lax.broadcasted_iota(jnp.int32, y.shape, 1)
    kr_ref[0] = jnp.where(lane < MLA_ROPE_DIM, y, 0.0).astype(BF16)
    kr_ref[1] = jnp.where(lane >= MLA_ROPE_DIM, y, 0.0).astype(BF16)


def _mla_prep(c, q_norm, kv_norm, ktab, s, *, tm=512):
    m, w = c.shape
    tm = min(tm, s)
    assert 2 * MLA_ROPE_DIM == LANES and w == MLA_Q_LORA + MLA_KV_LORA + LANES and s % tm == 0
    ns = s // tm
    return pl.pallas_call(
        _mla_prep_kernel,
        out_shape=(jax.ShapeDtypeStruct((m, MLA_Q_LORA), BF16),
                   jax.ShapeDtypeStruct((m, MLA_KV_LORA), BF16),
                   jax.ShapeDtypeStruct((2, m, LANES), BF16)),
        grid=(m // tm,),
        in_specs=[pl.BlockSpec((tm, w), lambda i: (i, 0)),
                  pl.BlockSpec((1, MLA_Q_LORA), lambda i: (0, 0)),
                  pl.BlockSpec((1, MLA_KV_LORA), lambda i: (0, 0)),
                  pl.BlockSpec((tm, LANES), lambda i: (i % ns, 0))],
        out_specs=(pl.BlockSpec((tm, MLA_Q_LORA), lambda i: (i, 0)),
                   pl.BlockSpec((tm, MLA_KV_LORA), lambda i: (i, 0)),
                   pl.BlockSpec((2, tm, LANES), lambda i: (0, i, 0))),
        compiler_params=_cparams(("parallel",)),
        name="mla_prep",
    )(c, q_norm.reshape(1, -1).astype(F32), kv_norm.reshape(1, -1).astype(F32), ktab)


def _mla_kernel(qn_ref, qr_ref, qs_ref, ct_ref, st_ref, kn_ref, kr_ref, v_ref, o_ref,
                m_ref, l_ref, acc_ref, kcat_ref):
    i = pl.program_id(2)

    @pl.when(i == 0)
    def _():
        kcat_ref[:, :HEAD_DIM] = kn_ref[...]
        kcat_ref[:, HEAD_DIM:] = kr_ref[0]

    qr = qr_ref[...].astype(F32) * ct_ref[...] + qs_ref[...].astype(F32) * st_ref[...]
    q = jnp.concatenate([qn_ref[...], qr.astype(BF16)], axis=1)

    def chunk(c, k):
        rows = pl.ds(c * BIG, BIG)
        s = lax.dot_general(q, kcat_ref[rows, :], _NT, preferred_element_type=F32)
        if c == k:
            r = lax.broadcasted_iota(jnp.int32, s.shape, 0)
            col = lax.broadcasted_iota(jnp.int32, s.shape, 1)
            s = jnp.where(r >= col, s, NEG)
        return s, v_ref[rows, :]

    def step(k, chunks):
        _flash_update_multi([chunk(c, k) for c in chunks], m_ref, l_ref, acc_ref)

    _flash_reset(m_ref, l_ref, acc_ref)
    _causal_chunk_schedule(i, kcat_ref.shape[0] // BIG, step)
    o_ref[...] = (acc_ref[...] / l_ref[...]).astype(o_ref.dtype)


def _mla_attention(qx, kv, kr, qc_tab, qs_tab, b, s):
    assert s % BIG == 0
    nq = s // BIG
    hh = N_HEADS
    return pl.pallas_call(
        _mla_kernel,
        out_shape=jax.ShapeDtypeStruct((b * s, hh * MLA_V_DIM), BF16),
        grid=(b, hh, nq),
        in_specs=[pl.BlockSpec((BIG, LANES), lambda bi, h, i: (bi * nq + i, h)),
                  pl.BlockSpec((BIG, LANES), lambda bi, h, i: (bi * nq + i, hh + h // 2)),
                  pl.BlockSpec((BIG, LANES), lambda bi, h, i: (bi * nq + i, hh + hh // 2 + h // 2)),
                  pl.BlockSpec((BIG, LANES), lambda bi, h, i: (i, 0)),
                  pl.BlockSpec((BIG, LANES), lambda bi, h, i: (i, 0)),
                  pl.BlockSpec((s, LANES), lambda bi, h, i: (bi, h)),
                  pl.BlockSpec((1, s, LANES), lambda bi, h, i: (h % 2, bi, 0)),
                  pl.BlockSpec((s, LANES), lambda bi, h, i: (bi, hh + h))],
        out_specs=pl.BlockSpec((BIG, LANES), lambda bi, h, i: (bi * nq + i, h)),
        scratch_shapes=[pltpu.VMEM((BIG, LANES), F32), pltpu.VMEM((BIG, LANES), F32),
                        pltpu.VMEM((BIG, MLA_V_DIM), F32), pltpu.VMEM((s, 2 * LANES), BF16)],
        compiler_params=_cparams(("parallel", "parallel", "arbitrary")),
        name="mla_attention",
    )(qx, qx, qx, qc_tab, qs_tab, kv, kr, kv)


def _bf(w):
    return w if isinstance(w, tuple) or w.dtype == BF16 else w.astype(BF16)


def _moba_mixer(xb, w_qkv, rel_bias, b, s):
    hd = N_HEADS * HEAD_DIM
    qkv = _matmul(xb, _bf(w_qkv), BF16, tn=512, scale_cols=hd, scale=HEAD_DIM ** -0.5)
    return _moba_attention(qkv, rel_bias, b, s)


def _swa_mixer(xb, w_qkv, sinks, rel_bias, b, s):
    hd = N_HEADS * HEAD_DIM
    qkv = _matmul(xb, _bf(w_qkv), BF16, tn=512, scale_cols=hd, scale=HEAD_DIM ** -0.5)
    return _swa_attention(qkv, sinks, rel_bias, b, s)


def _nsa_mixer(xb, w_in, w_gate, cmp_pos, cmp_w1, cmp_w2, rel_bias, b, s):
    hd = N_HEADS * HEAD_DIM
    hk_n = NSA_KV_HEADS
    group = N_HEADS // hk_n
    main = hd + 6 * hk_n * HEAD_DIM
    proj = _matmul(xb, _bf(w_in), BF16, tn=512, scale_cols=hd, scale=HEAD_DIM ** -0.5, n_out=main)
    wg = w_gate.reshape(-1, hk_n, 3 * group)
    wg = jnp.pad(wg, ((0, 0), (0, 0), (0, LANES - 3 * group))).reshape(-1, hk_n * LANES)
    gates = _matmul(xb, wg.astype(BF16), F32, tn=hk_n * LANES)
    cmp_kv = _nsa_compress(proj, cmp_pos, _bf(cmp_w1), _bf(cmp_w2), b, s)
    return _nsa_attention(proj, gates, cmp_kv, rel_bias, b, s)


def _mla_mixer(xb, w_down, q_norm, kv_norm, w_uq, w_ukv, b, s):
    hh = N_HEADS
    half = MLA_ROPE_DIM // 2
    lat = MLA_Q_LORA + MLA_KV_LORA
    swap = lambda t: jnp.concatenate([t[..., half:], t[..., :half]], axis=-1)
    w_down_x = jnp.concatenate([w_down, swap(w_down[:, lat:])], axis=1).astype(BF16)
    c = _matmul(xb, w_down_x, F32, tn=w_down_x.shape[1])

    inv = ROPE_BASE ** (-jnp.arange(0, MLA_ROPE_DIM, 2, dtype=F32) / MLA_ROPE_DIM)
    ang = jnp.arange(s, dtype=F32)[:, None] * inv[None, :]
    cos, sin = jnp.cos(ang), jnp.sin(ang)
    ktab = jnp.concatenate([cos, cos, -sin, sin], axis=1)
    qc_tab = jnp.concatenate([cos, cos, cos, cos], axis=1)
    qs_tab = jnp.concatenate([-sin, sin, -sin, sin], axis=1)
    cq, ckv, kr = _mla_prep(c, q_norm, kv_norm, ktab, s)

    wq = w_uq.reshape(MLA_Q_LORA, hh, MLA_NOPE_DIM + MLA_ROPE_DIM)
    wq_rope = wq[:, :, MLA_NOPE_DIM:]
    wq_x = jnp.concatenate([wq[:, :, :MLA_NOPE_DIM].reshape(MLA_Q_LORA, -1),
                            wq_rope.reshape(MLA_Q_LORA, -1),
                            swap(wq_rope).reshape(MLA_Q_LORA, -1)], axis=1).astype(BF16)
    scale = (MLA_NOPE_DIM + MLA_ROPE_DIM) ** -0.5
    qx = _matmul(cq, wq_x, BF16, tn=512, scale_cols=wq_x.shape[1], scale=scale)
    wkv = w_ukv.reshape(MLA_KV_LORA, hh, MLA_NOPE_DIM + MLA_V_DIM)
    wkv_x = jnp.concatenate([wkv[:, :, :MLA_NOPE_DIM].reshape(MLA_KV_LORA, -1),
                             wkv[:, :, MLA_NOPE_DIM:].reshape(MLA_KV_LORA, -1)], axis=1).astype(BF16)
    kv = _matmul(ckv, wkv_x, BF16, tn=512)
    return _mla_attention(qx, kv, kr, qc_tab, qs_tab, b, s)


def kernel(x, rel_bias, moba_w_qkv, moba_w_o, swa_w_qkv, swa_sinks, swa_w_o, nsa_w_in, nsa_cmp_pos,
           nsa_cmp_w1, nsa_cmp_w2, nsa_w_o, mla_w_down, mla_q_norm, mla_kv_norm, mla_w_uq, mla_w_ukv,
           mla_w_o, ln1_g, ln1_b, ffn_w_gate, ffn_w_up, ffn_w_down, ln2_g, ln2_b):
    b, s, d = x.shape
    depth = ln1_g.shape[0]
    n_mixers = 4
    xf = x.reshape(b * s, d).astype(F32)
    xb = xf.astype(BF16)
    moba_qkv_b, moba_o_b = moba_w_qkv, _to_bf16(moba_w_o)
    swa_qkv_b, swa_o_b = swa_w_qkv, _to_bf16(swa_w_o)
    nsa_in_b, nsa_o_b, nsa_w1_b = nsa_w_in, _to_bf16(nsa_w_o), _to_bf16(nsa_cmp_w1)
    mla_o_b = _to_bf16(mla_w_o)
    gate_b, up_b, down_b = ffn_w_gate, ffn_w_up, _to_bf16(ffn_w_down)
    nsa_main = (N_HEADS + 6 * NSA_KV_HEADS) * HEAD_DIM
    for i in range(depth):
        kind, j = i % n_mixers, i // n_mixers
        if kind == 0:
            o = _moba_mixer(xb, (moba_qkv_b, j), rel_bias, b, s)
            w_o = (moba_o_b, j)
        elif kind == 1:
            o = _swa_mixer(xb, (swa_qkv_b, j), swa_sinks[j], rel_bias, b, s)
            w_o = (swa_o_b, j)
        elif kind == 2:
            o = _nsa_mixer(xb, (nsa_in_b, j), nsa_w_in[j][:, nsa_main:], nsa_cmp_pos[j], nsa_w1_b[j],
                           nsa_cmp_w2[j], rel_bias, b, s)
            w_o = (nsa_o_b, j)
        else:
            o = _mla_mixer(xb, mla_w_down[j], mla_q_norm[j], mla_kv_norm[j], mla_w_uq[j], mla_w_ukv[j], b, s)
            w_o = (mla_o_b, j)
        xf, xb = _proj_ln(o, w_o, xf, ln1_g[i], ln1_b[i])
        hmid = _ffn_up(xb, (gate_b, i), (up_b, i))
        xf, xb = _proj_ln(hmid, (down_b, i), xf, ln2_g[i], ln2_b[i], tk=hmid.shape[1] // 2)
    return xf.reshape(b, s, d).astype(x.dtype)
```
